```python
import math
import jax, jax.numpy as jnp
from jax import lax
import numpy as np

D_MODEL = 1024
BATCH = 1
SEQ = 16384
DEPTH = 4

GRID_W = 64
CTX_LEN = 256
ROPE_DIM = 64
ROPE_BASE = 10000.0
EPS = 1e-6
HG_HEADS = 4
HG_DIM = 128
HG_W = HG_HEADS * HG_DIM
MLA_HEADS = 4
MLA_NOPE = 128
MLA_ROPE = ROPE_DIM
MLA_V = 128
MLA_Q_RANK = 384
MLA_KV_RANK = 256
DA_HEADS = 4
DA_DIM = ROPE_DIM
DA_V = 2 * DA_DIM
RT_HEADS = 4
RT_K = ROPE_DIM
RT_V = 128
MLP_HIDDEN = 4 * D_MODEL
CHUNK = 64
Q_BLOCK = 128
ADA_STD = 0.5
N_EVEN = (DEPTH + 1) // 2
N_ODD = DEPTH // 2
A_SPLITS = (HG_W, HG_W, HG_W, HG_W, HG_W, MLA_Q_RANK, MLA_KV_RANK, MLA_ROPE)
C_SPLITS = (DA_HEADS * 2 * DA_DIM, DA_HEADS * 2 * DA_DIM, DA_HEADS * DA_V,
            RT_HEADS * RT_K, RT_HEADS * RT_K, RT_HEADS * RT_V, RT_HEADS * RT_V)
A_IN = sum(A_SPLITS)
C_IN = sum(C_SPLITS)
MIX_W = HG_W + MLA_HEADS * MLA_V

kernel_name = "hybrid_hgrn2_mla_diffattn_retnet_dit"


def _split(p, sizes):
    return jnp.split(p, np.cumsum(sizes)[:-1].tolist(), axis=-1)


def _rms(x, w):
    xf = x.astype(jnp.float32)
    y = xf * lax.rsqrt(jnp.mean(xf * xf, axis=-1, keepdims=True) + EPS)
    return y.astype(x.dtype) * w


def _heads(a, h):
    return a.reshape(a.shape[0], a.shape[1], h, -1)


def _rope_tables(n, dim):
    rows = n // GRID_W
    row = jnp.repeat(jnp.arange(rows, dtype=jnp.float32), GRID_W)
    col = jnp.tile(jnp.arange(GRID_W, dtype=jnp.float32), rows)
    quarter = dim // 4
    inv_freq = ROPE_BASE ** (-jnp.arange(quarter, dtype=jnp.float32) / quarter)
    ang_r = row[:, None] * inv_freq
    ang_c = col[:, None] * inv_freq
    ang = jnp.concatenate([ang_r, ang_r, ang_c, ang_c], axis=-1)
    return jnp.cos(ang), jnp.sin(ang)


def _apply_rope(x, cos, sin):
    bshape = (1, x.shape[1]) + (1,) * (x.ndim - 3) + (x.shape[-1],)
    cs = cos.reshape(bshape).astype(x.dtype)
    sn = sin.reshape(bshape).astype(x.dtype)
    xr = x.reshape(x.shape[:-1] + (2, 2, x.shape[-1] // 4))
    rot = jnp.stack([-xr[..., 1, :], xr[..., 0, :]], axis=-2).reshape(x.shape)
    return x * cs + rot * sn


def _softmax_attend(q, k, v, scale):
    s = jnp.einsum('bqhd,bkhd->bhqk', q, k).astype(jnp.float32) * scale
    p = jax.nn.softmax(s, axis=-1).astype(v.dtype)
    return jnp.einsum('bhqk,bkhd->bqhd', p, v)


def _diff_attend(q, k, v, lam, scale):
    s = jnp.einsum('bqhcd,bkhcd->bhcqk', q, k).astype(jnp.float32) * scale
    p = jax.nn.softmax(s, axis=-1)
    a = (p[:, :, 0] - lam * p[:, :, 1]).astype(v.dtype)
    return jnp.einsum('bhqk,bkhd->bqhd', a, v)


def _sweep_queries(q, attend):
    b, n = q.shape[:2]
    qb = q.reshape((b, n // Q_BLOCK, Q_BLOCK) + q.shape[2:]).swapaxes(0, 1)
    out = lax.map(attend, qb)
    return out.swapaxes(0, 1).reshape((b, n) + out.shape[3:])


def _chunk_scan(q, k, v, logf, s0):
    b, t, h, _ = q.shape
    nc = t // CHUNK

    def to_chunks(a):
        return a.reshape(b, nc, CHUNK, h, a.shape[-1]).transpose(1, 0, 3, 2, 4)

    qc, kc, vc, gc = to_chunks(q), to_chunks(k), to_chunks(v), to_chunks(logf)
    mask = jnp.tril(jnp.ones((CHUNK, CHUNK), dtype=bool))[:, :, None]

    def step(s, inp):
        qi, ki, vi, gi = inp
        bcum = jnp.cumsum(gi.astype(jnp.float32), axis=2)
        diff = bcum[:, :, :, None, :] - bcum[:, :, None, :, :]
        dec = jnp.exp(jnp.where(mask, diff, -jnp.inf))
        a = jnp.sum(qi[:, :, :, None, :] * dec * ki[:, :, None, :, :], axis=-1)
        o = (jnp.einsum('bhts,bhsv->bhtv', a, vi)
             + jnp.einsum('bhtk,bhkv->bhtv', qi * jnp.exp(bcum), s))
        last = bcum[:, :, -1:, :]
        s_new = (jnp.exp(last[:, :, 0, :])[..., None] * s
                 + jnp.einsum('bhsk,bhsv->bhkv', ki * jnp.exp(last - bcum), vi))
        return s_new, o

    s_fin, oc = lax.scan(step, s0, (qc, kc, vc, gc))
    o = oc.transpose(1, 0, 3, 2, 4).reshape(b, t, h, -1)
    return o, s_fin


def _final_state(k, v, logf):
    bcum = jnp.cumsum(logf.astype(jnp.float32), axis=1)
    w = jnp.exp(bcum[:, -1:] - bcum)
    return jnp.einsum('bthk,bthv->bhkv', k * w, v)


def _bidir_recurrence(q_l, v_l, kg_l, q_c, v_c, kg_c, need_ctx):
    outs_l, outs_c = [], []
    for d in range(2):
        fl = (lambda a: a[:, ::-1]) if d == 1 else (lambda a: a)
        (k_l, g_l), (k_c, g_c) = kg_l[d], kg_c[d]
        if need_ctx:
            s0 = jnp.zeros((v_c.shape[0], v_c.shape[2], k_c.shape[-1], v_c.shape[-1]), jnp.float32)
            o_c, s_ctx = _chunk_scan(fl(q_c), fl(k_c), fl(v_c), fl(g_c), s0)
            outs_c.append(fl(o_c))
        else:
            s_ctx = _final_state(fl(k_c), fl(v_c), fl(g_c))
        o_l, _ = _chunk_scan(fl(q_l), fl(k_l), fl(v_l), fl(g_l), s_ctx)
        outs_l.append(fl(o_l))
    y_l = (outs_l[0] + outs_l[1]).astype(v_l.dtype)
    y_c = (outs_c[0] + outs_c[1]).astype(v_c.dtype) if need_ctx else None
    return y_l, y_c


def _hgrn2_gate(p_f, lb):
    z = p_f.astype(jnp.float32)
    logf = jnp.logaddexp(jnp.log(lb), jnp.log1p(-lb) + jax.nn.log_sigmoid(z))
    logf = _heads(logf, HG_HEADS)
    return (-jnp.expm1(logf)).astype(p_f.dtype), logf


def _even_mixer(h_l, h_c, cos, sin, lb, w_in, hg_norm, q_norm, kv_norm, w_uq, w_ukv,
                qk_q, qk_k, need_ctx):
    b, n, _ = h_l.shape
    m = h_c.shape[1]
    pl = _split(h_l @ w_in, A_SPLITS)
    pc = _split(h_c @ w_in, A_SPLITS)

    kg_l = tuple(_hgrn2_gate(pl[1 + d], lb[d]) for d in range(2))
    kg_c = tuple(_hgrn2_gate(pc[1 + d], lb[d]) for d in range(2))
    hq_l = _heads(jax.nn.silu(pl[0]), HG_HEADS)
    hq_c = _heads(jax.nn.silu(pc[0]), HG_HEADS) if need_ctx else None
    o_l, o_c = _bidir_recurrence(hq_l, _heads(pl[3], HG_HEADS), kg_l,
                                 hq_c, _heads(pc[3], HG_HEADS), kg_c, need_ctx)
    hg_l = _rms(o_l, hg_norm).reshape(b, n, HG_W) * jax.nn.silu(pl[4])

    def mla_kv(p):
        kv = _heads(_rms(p[6], kv_norm) @ w_ukv, MLA_HEADS)
        k_nope, v = kv[..., :MLA_NOPE], kv[..., MLA_NOPE:]
        k_rope = jnp.broadcast_to(p[7][:, :, None, :], k_nope.shape[:3] + (MLA_ROPE,))
        return _rms(jnp.concatenate([k_nope, k_rope], axis=-1), qk_k), v

    def mla_q(p):
        return _rms(_heads(_rms(p[5], q_norm) @ w_uq, MLA_HEADS), qk_q)

    def rope_tail(a):
        return jnp.concatenate([a[..., :MLA_NOPE], _apply_rope(a[..., MLA_NOPE:], cos, sin)], axis=-1)

    scale = (MLA_NOPE + MLA_ROPE) ** -0.5
    k_c, v_c = mla_kv(pc)
    k_l, v_l = mla_kv(pl)
    k_all = jnp.concatenate([k_c, rope_tail(k_l)], axis=1)
    v_all = jnp.concatenate([v_c, v_l], axis=1)
    a_l = _sweep_queries(rope_tail(mla_q(pl)), lambda qb: _softmax_attend(qb, k_all, v_all, scale))
    y_l = jnp.concatenate([hg_l, a_l.reshape(b, n, MLA_HEADS * MLA_V)], axis=-1)

    if need_ctx:
        hg_c = _rms(o_c, hg_norm).reshape(b, m, HG_W) * jax.nn.silu(pc[4])
        a_c = _softmax_attend(mla_q(pc), k_c, v_c, scale)
        y_c = jnp.concatenate([hg_c, a_c.reshape(b, m, MLA_HEADS * MLA_V)], axis=-1)
    else:
        y_c = None
    return y_l, y_c


def _odd_mixer(h_l, h_c, cos, sin, layer, w_in, lam, qk_q, qk_k, subln, rt_decay, rt_norm, need_ctx):
    b, n, _ = h_l.shape
    m = h_c.shape[1]
    pl = _split(h_l @ w_in, C_SPLITS)
    pc = _split(h_c @ w_in, C_SPLITS)

    lam_init = 0.8 - 0.6 * math.exp(-0.3 * layer)
    lf = lam.astype(jnp.float32)
    lam_full = jnp.exp(jnp.sum(lf[0] * lf[1])) - jnp.exp(jnp.sum(lf[2] * lf[3])) + lam_init

    def sub(a):
        return a.reshape(a.shape[0], a.shape[1], DA_HEADS, 2, DA_DIM)

    scale = DA_DIM ** -0.5
    dk_c = _rms(sub(pc[1]), qk_k)
    dv_c = _heads(pc[2], DA_HEADS)
    dk_l = _apply_rope(_rms(sub(pl[1]), qk_k), cos, sin)
    dq_l = _apply_rope(_rms(sub(pl[0]), qk_q), cos, sin)
    k_all = jnp.concatenate([dk_c, dk_l], axis=1)
    v_all = jnp.concatenate([dv_c, _heads(pl[2], DA_HEADS)], axis=1)
    d_l = _sweep_queries(dq_l, lambda qb: _diff_attend(qb, k_all, v_all, lam_full, scale))
    d_l = (_rms(d_l, subln) * (1.0 - lam_init)).reshape(b, n, DA_HEADS * DA_V)

    log_gamma = jax.nn.log_sigmoid(rt_decay.astype(jnp.float32))

    def rt_kg(k):
        shp = k.shape[:3] + (1,)
        return tuple((k, jnp.broadcast_to(log_gamma[d][:, None], shp)) for d in range(2))

    rq_l = _apply_rope(_heads(pl[3], RT_HEADS), cos, sin)
    rk_l = _apply_rope(_heads(pl[4], RT_HEADS) * RT_K ** -0.5, cos, sin)
    rk_c = _heads(pc[4], RT_HEADS) * RT_K ** -0.5
    rq_c = _heads(pc[3], RT_HEADS) if need_ctx else None
    r_l, r_c = _bidir_recurrence(rq_l, _heads(pl[5], RT_HEADS), rt_kg(rk_l),
                                 rq_c, _heads(pc[5], RT_HEADS), rt_kg(rk_c), need_ctx)
    r_l = _rms(r_l, rt_norm).reshape(b, n, RT_HEADS * RT_V) * jax.nn.silu(pl[6])
    y_l = jnp.concatenate([d_l, r_l], axis=-1)

    if need_ctx:
        d_c = _diff_attend(_rms(sub(pc[0]), qk_q), dk_c, dv_c, lam_full, scale)
        d_c = (_rms(d_c, subln) * (1.0 - lam_init)).reshape(b, m, DA_HEADS * DA_V)
        r_c = _rms(r_c, rt_norm).reshape(b, m, RT_HEADS * RT_V) * jax.nn.silu(pc[6])
        y_c = jnp.concatenate([d_c, r_c], axis=-1)
    else:
        y_c = None
    return y_l, y_c


def _mlp(h, w1, w2):
    return jnp.square(jax.nn.relu(h @ w1)) @ w2


def setup_inputs(seed: int = 0) -> dict:
    key = jax.random.key(seed)
    k = jax.random.split(key, 26)
    f32 = jnp.float32

    def nrm(kk, shape, std):
        return jax.random.normal(kk, shape, f32) * std

    def gain(kk, shape):
        return 1.0 + 0.02 * jax.random.normal(kk, shape, f32)

    rt_base = jnp.log(2.0 ** (5.0 + jnp.arange(RT_HEADS, dtype=f32)) - 1.0)
    return {
        "x": nrm(k[0], (BATCH, SEQ, D_MODEL), 1.0),
        "c": nrm(k[1], (BATCH, D_MODEL), 1.0),
        "ctx": nrm(k[2], (BATCH, CTX_LEN, D_MODEL), 1.0),
        "c_ctx": nrm(k[3], (D_MODEL,), 1.0),
        "ada_w": nrm(k[4], (DEPTH, D_MODEL, 6 * D_MODEL), ADA_STD * D_MODEL ** -0.5),
        "ada_b": nrm(k[5], (DEPTH, 6 * D_MODEL), 0.01),
        "norm_w": gain(k[6], (DEPTH, 2, D_MODEL)),
        "w_o": nrm(k[7], (DEPTH, MIX_W, D_MODEL), MIX_W ** -0.5),
        "mlp_w1": nrm(k[8], (DEPTH, D_MODEL, MLP_HIDDEN), D_MODEL ** -0.5),
        "mlp_w2": nrm(k[9], (DEPTH, MLP_HIDDEN, D_MODEL), MLP_HIDDEN ** -0.5),
        "a_w_in": nrm(k[10], (N_EVEN, D_MODEL, A_IN), D_MODEL ** -0.5),
        "hg_lb": nrm(k[11], (N_EVEN, 2, HG_W), 0.5),
        "hg_norm": gain(k[12], (N_EVEN, HG_DIM)),
        "mla_q_norm": gain(k[13], (N_EVEN, MLA_Q_RANK)),
        "mla_kv_norm": gain(k[14], (N_EVEN, MLA_KV_RANK)),
        "mla_w_uq": nrm(k[15], (N_EVEN, MLA_Q_RANK, MLA_HEADS * (MLA_NOPE + MLA_ROPE)), MLA_Q_RANK ** -0.5),
        "mla_w_ukv": nrm(k[16], (N_EVEN, MLA_KV_RANK, MLA_HEADS * (MLA_NOPE + MLA_V)), MLA_KV_RANK ** -0.5),
        "mla_qk_q": gain(k[17], (N_EVEN, MLA_NOPE + MLA_ROPE)),
        "mla_qk_k": gain(k[18], (N_EVEN, MLA_NOPE + MLA_ROPE)),
        "c_w_in": nrm(k[19], (N_ODD, D_MODEL, C_IN), D_MODEL ** -0.5),
        "da_lambda": nrm(k[20], (N_ODD, 4, DA_DIM), 0.1),
        "da_qk_q": gain(k[21], (N_ODD, DA_DIM)),
        "da_qk_k": gain(k[22], (N_ODD, DA_DIM)),
        "da_subln": gain(k[23], (N_ODD, DA_V)),
        "rt_decay": rt_base + nrm(k[24], (N_ODD, 2, RT_HEADS), 0.01),
        "rt_norm": gain(k[25], (N_ODD, RT_V)),
    }


def reference(x, c, ctx, c_ctx, ada_w, ada_b, norm_w, w_o, mlp_w1, mlp_w2,
              a_w_in, hg_lb, hg_norm, mla_q_norm, mla_kv_norm, mla_w_uq, mla_w_ukv, mla_qk_q, mla_qk_k,
              c_w_in, da_lambda, da_qk_q, da_qk_k, da_subln, rt_decay, rt_norm):
    n = x.shape[1]
    cos, sin = _rope_tables(n, ROPE_DIM)
    lb = jnp.cumsum(jax.nn.softmax(hg_lb.astype(jnp.float32), axis=0), axis=0)
    lb = lb - lb[:1]
    xc = ctx
    for l in range(DEPTH):
        last = l == DEPTH - 1
        mod_l = (jax.nn.silu(c) @ ada_w[l] + ada_b[l])[:, None, :]
        mod_c = jax.nn.silu(c_ctx) @ ada_w[l] + ada_b[l]
        sh1, sc1, g1, sh2, sc2, g2 = jnp.split(mod_l, 6, axis=-1)
        csh1, csc1, cg1, csh2, csc2, cg2 = jnp.split(mod_c, 6, axis=-1)
        h_l = _rms(x, norm_w[l, 0]) * (1 + sc1) + sh1
        h_c = _rms(xc, norm_w[l, 0]) * (1 + csc1) + csh1
        j = l // 2
        if l % 2 == 0:
            y_l, y_c = _even_mixer(h_l, h_c, cos, sin, lb[j], a_w_in[j], hg_norm[j], mla_q_norm[j],
                                   mla_kv_norm[j], mla_w_uq[j], mla_w_ukv[j], mla_qk_q[j], mla_qk_k[j],
                                   not last)
        else:
            y_l, y_c = _odd_mixer(h_l, h_c, cos, sin, l, c_w_in[j], da_lambda[j], da_qk_q[j], da_qk_k[j],
                                  da_subln[j], rt_decay[j], rt_norm[j], not last)
        x = x + g1 * (y_l @ w_o[l])
        x = x + g2 * _mlp(_rms(x, norm_w[l, 1]) * (1 + sc2) + sh2, mlp_w1[l], mlp_w2[l])
        if not last:
            xc = xc + cg1 * (y_c @ w_o[l])
            xc = xc + cg2 * _mlp(_rms(xc, norm_w[l, 1]) * (1 + csc2) + csh2, mlp_w1[l], mlp_w2[l])
    return x
```

```python
import functools
import math

import numpy as np
import jax
import jax.numpy as jnp
from jax import lax
from jax.experimental import pallas as pl
from jax.experimental.pallas import tpu as pltpu

F32 = jnp.float32
BF16 = jnp.bfloat16

GRID_W = 64
ROPE_DIM = 64
ROPE_BASE = 10000.0
EPS = 1e-6
HEADS = 4
HG_DIM = 128
HG_W = HEADS * HG_DIM
MLA_NOPE = 128
MLA_ROPE = ROPE_DIM
MLA_V = 128
MLA_QK = MLA_NOPE + MLA_ROPE
MLA_Q_RANK = 384
MLA_KV_RANK = 256
DA_DIM = ROPE_DIM
DA_V = 2 * DA_DIM
RT_K = ROPE_DIM
RT_V = 128
MLA_IN = 768

LANES = 128
VMEM_LIMIT = 56 * 1024 * 1024
ROW_BLOCK = 1280
ATT_BQ = 256
ATT_BK = 512
SCAN_CHUNK = 64
RET_CHUNK = 128


def _cparams(sem):
    return pltpu.CompilerParams(dimension_semantics=sem, vmem_limit_bytes=VMEM_LIMIT)


def _row_block(t, target):
    best = None
    for r in range(LANES, min(t, target) + 1, LANES):
        if t % r == 0:
            best = r
    assert best is not None, t
    return best


def _dot(a, b):
    return jnp.dot(a, b, preferred_element_type=F32)


def _dot_nt(a, b):
    return lax.dot_general(a, b, (((1,), (1,)), ((), ())), preferred_element_type=F32)


def _dot_tn(a, b):
    return lax.dot_general(a, b, (((0,), (0,)), ((), ())), preferred_element_type=F32)


def _silu(x):
    return x * (1.0 / (1.0 + jnp.exp(-x)))


def _row_mod(mod_ref, k, row0, rows, m_ctx):
    r = row0 + lax.broadcasted_iota(jnp.int32, (rows, 1), 0)
    return jnp.where(r < m_ctx, mod_ref[0, k:k + 1, :], mod_ref[1, k:k + 1, :])


def _modnorm(x, nw, sc, sh):
    y = x * lax.rsqrt(jnp.mean(x * x, axis=-1, keepdims=True) + EPS)
    return y * nw * (1.0 + sc) + sh


def _ada_kernel(c_ref, w_ref, b_ref, o_ref):
    cv = c_ref[...]
    o_ref[0] = jnp.dot(_silu(cv), w_ref[0], precision=lax.Precision.HIGHEST,
                       preferred_element_type=F32) + b_ref[0]


def _ada_table(cvec, ada_w, ada_b):
    depth, d, d6 = ada_w.shape
    tn = d6 // 4
    return pl.pallas_call(
        _ada_kernel,
        grid=(depth, d6 // tn),
        in_specs=[pl.BlockSpec((8, d), lambda l, j: (0, 0)),
                  pl.BlockSpec((1, d, tn), lambda l, j: (l, 0, j)),
                  pl.BlockSpec((1, 1, tn), lambda l, j: (l, 0, j))],
        out_specs=pl.BlockSpec((1, 8, tn), lambda l, j: (l, 0, j)),
        out_shape=jax.ShapeDtypeStruct((depth, 8, d6), F32),
        compiler_params=_cparams(("parallel", "parallel")),
        name="ada_table",
    )(cvec, ada_w, ada_b.reshape(depth, 1, d6))


def _proj_kernel(*refs, n_w, m_ctx, rows):
    x_ref, mod_ref, nw_ref = refs[:3]
    w_refs = refs[3:3 + n_w]
    o_refs = refs[3 + n_w:]
    row0 = pl.program_id(0) * rows
    h = _modnorm(x_ref[...], nw_ref[...],
                 _row_mod(mod_ref, 1, row0, rows, m_ctx),
                 _row_mod(mod_ref, 0, row0, rows, m_ctx)).astype(BF16)
    for w_ref, o_ref in zip(w_refs, o_refs):
        o_ref[...] = _dot(h, w_ref[...])


def _proj(xt, mod, nw, weights, m_ctx):
    t, d = xt.shape
    rows = _row_block(t, ROW_BLOCK // 2)
    full = lambda a: pl.BlockSpec(a.shape, lambda i: (0,) * a.ndim)
    return pl.pallas_call(
        functools.partial(_proj_kernel, n_w=len(weights), m_ctx=m_ctx, rows=rows),
        grid=(t // rows,),
        in_specs=[pl.BlockSpec((rows, d), lambda i: (i, 0)), full(mod), full(nw)]
                 + [full(w) for w in weights],
        out_specs=[pl.BlockSpec((rows, w.shape[1]), lambda i: (i, 0)) for w in weights],
        out_shape=[jax.ShapeDtypeStruct((t, w.shape[1]), F32) for w in weights],
        compiler_params=_cparams(("parallel",)),
        name="in_proj",
    )(xt, mod, nw, *weights)


def _hier_tables(c, reverse):
    levels = int(math.log2(c))
    assert 1 << levels == c
    cums = np.zeros(((levels + 1) * c, c), np.float32)
    roles = np.zeros((levels, c, LANES), np.float32)
    masks = np.zeros((levels + 1, c, c), np.float32)
    for li in range(levels):
        h = c >> (li + 1)
        for t in range(c):
            base = (t // (2 * h)) * 2 * h
            late = (t - base) >= h
            if not reverse:
                if late:
                    cums[li * c + t, base + h:t + 1] = 1.0
                else:
                    cums[li * c + t, t + 1:base + h] = 1.0
            else:
                if late:
                    cums[li * c + t, base + h:t] = 1.0
                else:
                    cums[li * c + t, t:base + h] = 1.0
            is_query = late != reverse
            roles[li, t, :] = 1.0 if is_query else 0.0
        for t in range(c):
            for s in range(c):
                same = (t // (2 * h)) == (s // (2 * h))
                if same and roles[li, t, 0] == 1.0 and roles[li, s, 0] == 0.0:
                    masks[li, t, s] = 1.0
    for t in range(c):
        if not reverse:
            cums[levels * c + t, :t + 1] = 1.0
        else:
            cums[levels * c + t, t:] = 1.0
    masks[levels] = np.eye(c, dtype=np.float32)
    return cums, roles, masks


def _split3(x):
    hi = x.astype(BF16)
    r1 = x - hi.astype(F32)
    mid = r1.astype(BF16)
    lo = (r1 - mid.astype(F32)).astype(BF16)
    return hi, mid, lo


def _gated_chunk(q, kk, v, logf, cum_ref, role_ref, mask_ref, st_ref, c, reverse):
    levels = role_ref.shape[0]
    cum = cum_ref[...]
    hi, mid, lo = _split3(logf)
    x = _dot(cum, hi) + _dot(cum, mid) + _dot(cum, lo)
    run = x[levels * c:(levels + 1) * c]
    tot = run[0:1] if reverse else run[c - 1:c]
    qb = q.astype(BF16)
    kb = kk.astype(BF16)
    a = _dot_nt(qb, kb) * mask_ref[levels]
    for li in range(levels):
        e = jnp.exp(x[li * c:(li + 1) * c])
        z = (jnp.where(role_ref[li] > 0.5, q, kk) * e).astype(BF16)
        a = a + _dot_nt(z, z) * mask_ref[li]
    st = st_ref[...]
    vb = v.astype(BF16)
    o = _dot(a.astype(BF16), vb) + _dot_nt((q * jnp.exp(run)).astype(BF16), st.astype(BF16))
    kd = (kk * jnp.exp(tot - run)).astype(BF16)
    st_ref[...] = st * jnp.exp(tot) + _dot_tn(vb, kd)
    return o


def _hgrn2_gate(z, log_lb, log1m_lb):
    soft = jnp.log1p(jnp.exp(-jnp.abs(z)))
    b = log1m_lb + (jnp.minimum(z, 0.0) - soft)
    logf = jnp.maximum(log_lb, b) + jnp.log1p(jnp.exp(-jnp.abs(log_lb - b)))
    return logf, jnp.exp(log1m_lb + (jnp.minimum(-z, 0.0) - soft))


def _hgrn2_kernel(qf_ref, zf_ref, vf_ref, qb_ref, zb_ref, vb_ref, llb_ref, l1m_ref,
                  cumf_ref, rolef_ref, maskf_ref, cumb_ref, roleb_ref, maskb_ref,
                  of_ref, ob_ref, sf_ref, sb_ref, *, c):
    @pl.when(pl.program_id(1) == 0)
    def _():
        sf_ref[...] = jnp.zeros_like(sf_ref)
        sb_ref[...] = jnp.zeros_like(sb_ref)

    logf, kk = _hgrn2_gate(zf_ref[...], llb_ref[0], l1m_ref[0])
    of_ref[...] = _gated_chunk(_silu(qf_ref[...]), kk, vf_ref[...], logf,
                               cumf_ref, rolef_ref, maskf_ref, sf_ref, c, False)
    logf, kk = _hgrn2_gate(zb_ref[...], llb_ref[1], l1m_ref[1])
    ob_ref[...] = _gated_chunk(_silu(qb_ref[...]), kk, vb_ref[...], logf,
                               cumb_ref, roleb_ref, maskb_ref, sb_ref, c, True)


def _bwd_chunk(i, mc, steps):
    return jnp.where(i < mc, mc - 1 - i, steps - 1 - i + mc)


def _hgrn2(p_main, log_lb, log1m_lb, m_ctx):
    t = p_main.shape[0]
    c = SCAN_CHUNK
    steps, mc = t // c, m_ctx // c
    tabs = [jnp.asarray(a) for rev in (False, True) for a in _hier_tables(c, rev)]
    tabs[0], tabs[3] = tabs[0].astype(BF16), tabs[3].astype(BF16)
    fwd = lambda sec: pl.BlockSpec((c, HG_DIM), lambda h, i: (i, sec * HEADS + h))
    bwd = lambda sec: pl.BlockSpec((c, HG_DIM), lambda h, i: (_bwd_chunk(i, mc, steps), sec * HEADS + h))
    full = lambda a: pl.BlockSpec(a.shape, lambda h, i: (0,) * a.ndim)
    lbspec = pl.BlockSpec((2, 1, HG_DIM), lambda h, i: (0, 0, h))
    return pl.pallas_call(
        functools.partial(_hgrn2_kernel, c=c),
        grid=(HEADS, steps),
        in_specs=[fwd(0), fwd(1), fwd(3), bwd(0), bwd(2), bwd(3), lbspec, lbspec]
                 + [full(a) for a in tabs],
        out_specs=[pl.BlockSpec((c, HG_DIM), lambda h, i: (i, h)),
                   pl.BlockSpec((c, HG_DIM), lambda h, i: (_bwd_chunk(i, mc, steps), h))],
        out_shape=[jax.ShapeDtypeStruct((t, HG_W), F32)] * 2,
        scratch_shapes=[pltpu.VMEM((HG_DIM, HG_DIM), F32)] * 2,
        compiler_params=_cparams(("parallel", "arbitrary")),
        name="hgrn2_scan",
    )(p_main, p_main, p_main, p_main, p_main, p_main, log_lb, log1m_lb, *tabs)


def _ret_chunk(q, k, v, lg, st_ref, c, reverse):
    t = lax.broadcasted_iota(jnp.int32, (c, c), 0)
    s = lax.broadcasted_iota(jnp.int32, (c, c), 1)
    dlt = (s - t) if reverse else (t - s)
    dec = jnp.where(dlt >= 0, jnp.exp(lg[:, :c] * jnp.maximum(dlt, 0).astype(F32)), 0.0)
    r = lax.broadcasted_iota(jnp.int32, (c, RT_K), 0).astype(F32)
    lk = lg[:, :RT_K]
    qdec = jnp.exp(lk * ((c - r) if reverse else (r + 1.0)))
    kdec = jnp.exp(lk * (r if reverse else (c - 1.0 - r)))
    a = _dot_nt(q.astype(BF16), k.astype(BF16)) * dec
    st = st_ref[...]
    vb = v.astype(BF16)
    o = _dot(a.astype(BF16), vb) + _dot_nt((q * qdec).astype(BF16), st.astype(BF16))
    st_ref[...] = st * jnp.exp(lk * float(c)) + _dot_tn(vb, (k * kdec).astype(BF16))
    return o


def _ret_kernel(qf_ref, kf_ref, vf_ref, qb_ref, kb_ref, vb_ref, lg_ref,
                of_ref, ob_ref, sf_ref, sb_ref, *, c):
    @pl.when(pl.program_id(1) == 0)
    def _():
        sf_ref[...] = jnp.zeros_like(sf_ref)
        sb_ref[...] = jnp.zeros_like(sb_ref)

    of_ref[...] = _ret_chunk(qf_ref[0], kf_ref[0], vf_ref[...], lg_ref[0, 0, 0:1, :], sf_ref, c, False)
    ob_ref[...] = _ret_chunk(qb_ref[0], kb_ref[0], vb_ref[...], lg_ref[1, 0, 0:1, :], sb_ref, c, True)


def _retention(rq, rk, p, v_col0, lg, m_ctx):
    t = p.shape[0]
    c = RET_CHUNK
    steps, mc = t // c, m_ctx // c
    fq = pl.BlockSpec((1, c, RT_K), lambda h, i: (h, i, 0))
    bq = pl.BlockSpec((1, c, RT_K), lambda h, i: (h, _bwd_chunk(i, mc, steps), 0))
    fv = pl.BlockSpec((c, RT_V), lambda h, i: (i, v_col0 + h))
    bv = pl.BlockSpec((c, RT_V), lambda h, i: (_bwd_chunk(i, mc, steps), v_col0 + h))
    return pl.pallas_call(
        functools.partial(_ret_kernel, c=c),
        grid=(HEADS, steps),
        in_specs=[fq, fq, fv, bq, bq, bv,
                  pl.BlockSpec((2, 1, 8, LANES), lambda h, i: (0, h, 0, 0))],
        out_specs=[pl.BlockSpec((c, RT_V), lambda h, i: (i, h)),
                   pl.BlockSpec((c, RT_V), lambda h, i: (_bwd_chunk(i, mc, steps), h))],
        out_shape=[jax.ShapeDtypeStruct((t, HEADS * RT_V), F32)] * 2,
        scratch_shapes=[pltpu.VMEM((RT_V, RT_K), F32)] * 2,
        compiler_params=_cparams(("parallel", "arbitrary")),
        name="retention_scan",
    )(rq, rk, p, rq, rk, p, lg)


def _rope128(y, cos, sin_a, sin_b):
    return y * cos + pltpu.roll(y, LANES - 16, 1) * sin_a + pltpu.roll(y, 16, 1) * sin_b


def _rope_tables(n, m_ctx):
    rows = n // GRID_W
    row = jnp.repeat(jnp.arange(rows, dtype=F32), GRID_W)
    col = jnp.tile(jnp.arange(GRID_W, dtype=F32), rows)
    quarter = ROPE_DIM // 4
    inv_freq = ROPE_BASE ** (-jnp.arange(quarter, dtype=F32) / quarter)
    ang_r = row[:, None] * inv_freq
    ang_c = col[:, None] * inv_freq
    ang = jnp.concatenate([ang_r, ang_r, ang_c, ang_c], axis=-1)
    cos, sin = jnp.cos(ang), jnp.sin(ang)
    first = (jnp.arange(ROPE_DIM) % 32) < 16
    sin_a = jnp.where(first, -sin, 0.0)
    sin_b = jnp.where(first, 0.0, sin)
    pad = lambda a, v: jnp.tile(jnp.concatenate([jnp.full((m_ctx, ROPE_DIM), v, F32), a], axis=0), (1, 2))
    return pad(cos, 1.0), pad(sin_a, 0.0), pad(sin_b, 0.0)


def _half_sums(sq):
    low = lax.broadcasted_iota(jnp.int32, sq.shape, 1) < 64
    lo = jnp.sum(jnp.where(low, sq, 0.0), axis=-1, keepdims=True)
    return low, lo, jnp.sum(sq, axis=-1, keepdims=True) - lo


def _mla_prep_kernel(p_ref, cos_ref, sa_ref, sb_ref, qn_ref, kvn_ref, wuq_ref, wukv_ref,
                     gq_ref, gk_ref, q_ref, kt_ref, v_ref):
    p = p_ref[...]
    cos, sa, sb = cos_ref[...], sa_ref[...], sb_ref[...]
    scale = MLA_QK ** -0.5

    def rms(x, w):
        return x * lax.rsqrt(jnp.mean(x * x, axis=-1, keepdims=True) + EPS) * w

    qu = _dot(rms(p[:, :MLA_Q_RANK], qn_ref[...]).astype(BF16), wuq_ref[...])
    kv = _dot(rms(p[:, MLA_Q_RANK:MLA_Q_RANK + MLA_KV_RANK], kvn_ref[...]).astype(BF16),
              wukv_ref[...])
    kr = p[:, MLA_Q_RANK + MLA_KV_RANK:]
    ss_kr = jnp.sum(kr * kr, axis=-1, keepdims=True)
    gq, gk = gq_ref[...], gk_ref[...]
    for b in range(HEADS // 2):
        qr = qu[:, HEADS * MLA_NOPE + b * LANES:HEADS * MLA_NOPE + (b + 1) * LANES]
        low, ss_lo, ss_hi = _half_sums(qr * qr)
        inv = []
        for j in range(2):
            h = 2 * b + j
            qn = qu[:, h * MLA_NOPE:(h + 1) * MLA_NOPE]
            ss = jnp.sum(qn * qn, axis=-1, keepdims=True) + (ss_lo, ss_hi)[j]
            inv.append(lax.rsqrt(ss * (1.0 / MLA_QK) + EPS))
            q_ref[h, :, 0:MLA_NOPE] = (qn * inv[j] * gq[0:1] * scale).astype(BF16)
        yr = _rope128(qr * jnp.where(low, inv[0], inv[1]) * gq[1:2], cos, sa, sb) * scale
        q_ref[2 * b, :, MLA_NOPE:MLA_QK] = yr[:, :MLA_ROPE].astype(BF16)
        q_ref[2 * b + 1, :, MLA_NOPE:MLA_QK] = yr[:, MLA_ROPE:].astype(BF16)
    low = lax.broadcasted_iota(jnp.int32, kr.shape, 1) < 64
    gk_rope = jnp.where(low, gk[1:2], 0.0)
    for h in range(HEADS):
        kn = kv[:, h * MLA_NOPE:(h + 1) * MLA_NOPE]
        inv = lax.rsqrt((jnp.sum(kn * kn, axis=-1, keepdims=True) + ss_kr) * (1.0 / MLA_QK) + EPS)
        kt_ref[h, 0:MLA_NOPE, :] = (kn * inv * gk[0:1]).T.astype(BF16)
        yr = _rope128(kr * inv * gk_rope, cos, sa, sb)
        kt_ref[h, MLA_NOPE:MLA_QK, :] = yr.T[0:MLA_ROPE].astype(BF16)
        v_ref[h] = kv[:, HEADS * MLA_NOPE + h * MLA_V:HEADS * MLA_NOPE + (h + 1) * MLA_V].astype(BF16)


def _mla_prep(p_mla, tables, qn, kvn, wuq, wukv, gq, gk):
    t = p_mla.shape[0]
    rows = _row_block(t, ROW_BLOCK // 2)
    full = lambda a: pl.BlockSpec(a.shape, lambda i: (0,) * a.ndim)
    rowb = lambda w: pl.BlockSpec((rows, w), lambda i: (i, 0))
    args = (qn, kvn, wuq, wukv, gq, gk)
    return pl.pallas_call(
        _mla_prep_kernel,
        grid=(t // rows,),
        in_specs=[rowb(MLA_IN), rowb(LANES), rowb(LANES), rowb(LANES)] + [full(a) for a in args],
        out_specs=[pl.BlockSpec((HEADS, rows, MLA_QK), lambda i: (0, i, 0)),
                   pl.BlockSpec((HEADS, MLA_QK, rows), lambda i: (0, 0, i)),
                   pl.BlockSpec((HEADS, rows, MLA_V), lambda i: (0, i, 0))],
        out_shape=[jax.ShapeDtypeStruct((HEADS, t, MLA_QK), BF16),
                   jax.ShapeDtypeStruct((HEADS, MLA_QK, t), BF16),
                   jax.ShapeDtypeStruct((HEADS, t, MLA_V), BF16)],
        compiler_params=_cparams(("parallel",)),
        name="mla_prep",
    )(p_mla, *tables, *args)


def _softmax_first(q, kt, v, m_sc, l_sc, acc_sc):
    s = _dot(q, kt)
    mx = jnp.max(s, axis=-1, keepdims=True)
    p = jnp.exp(s - mx)
    m_sc[...] = mx
    l_sc[...] = jnp.sum(p, axis=-1, keepdims=True)
    acc_sc[...] = _dot(p.astype(BF16), v)


def _softmax_next(q, kt, v, m_sc, l_sc, acc_sc):
    s = _dot(q, kt)
    m_prev = m_sc[...]
    m_new = jnp.maximum(m_prev, jnp.max(s, axis=-1, keepdims=True))
    alpha = jnp.exp(m_prev - m_new)
    p = jnp.exp(s - m_new)
    l_sc[...] = alpha * l_sc[...] + jnp.sum(p, axis=-1, keepdims=True)
    acc_sc[...] = alpha * acc_sc[...] + _dot(p.astype(BF16), v)
    m_sc[...] = m_new


def _mla_attn_kernel(q_ref, kt_ref, v_ref, o_ref, m_sc, l_sc, acc_sc, *, m_ctx, bq, bk, n_blocks):
    q = q_ref[0]
    _softmax_first(q, kt_ref[0, :, 0:m_ctx], v_ref[0, 0:m_ctx, :], m_sc, l_sc, acc_sc)

    @pl.when(pl.program_id(1) >= m_ctx // bq)
    def _():
        def body(j, carry):
            off = pl.multiple_of(m_ctx + j * bk, LANES)
            _softmax_next(q, kt_ref[0, :, pl.ds(off, bk)], v_ref[0, pl.ds(off, bk), :], m_sc, l_sc, acc_sc)
            return carry
        lax.fori_loop(0, n_blocks, body, 0)

    o_ref[...] = acc_sc[...] / l_sc[...]


def _attn_blocks(t, m_ctx):
    bq = ATT_BQ
    assert m_ctx % bq == 0 and t % bq == 0
    n_lat = t - m_ctx
    bk = ATT_BK if n_lat % ATT_BK == 0 else LANES
    assert n_lat % bk == 0
    return bq, bk, n_lat // bk


def _mla_attn(q, kt, v, m_ctx):
    t = q.shape[1]
    bq, bk, n_blocks = _attn_blocks(t, m_ctx)
    return pl.pallas_call(
        functools.partial(_mla_attn_kernel, m_ctx=m_ctx, bq=bq, bk=bk, n_blocks=n_blocks),
        grid=(HEADS, t // bq),
        in_specs=[pl.BlockSpec((1, bq, MLA_QK), lambda h, i: (h, i, 0)),
                  pl.BlockSpec((1, MLA_QK, t), lambda h, i: (h, 0, 0)),
                  pl.BlockSpec((1, t, MLA_V), lambda h, i: (h, 0, 0))],
        out_specs=pl.BlockSpec((bq, MLA_V), lambda h, i: (i, h)),
        out_shape=jax.ShapeDtypeStruct((t, HEADS * MLA_V), F32),
        scratch_shapes=[pltpu.VMEM((bq, 1), F32), pltpu.VMEM((bq, 1), F32), pltpu.VMEM((bq, MLA_V), F32)],
        compiler_params=_cparams(("parallel", "arbitrary")),
        name="mla_attn",
    )(q, kt, v)


def _diff_attn_kernel(q1_ref, q2_ref, k1_ref, k2_ref, v_ref, lam_ref, sub_ref, o_ref,
                      m1, l1, a1, m2, l2, a2, *, m_ctx, bq, bk, n_blocks, out_scale):
    q1, q2 = q1_ref[0], q2_ref[0]
    vc = v_ref[0, 0:m_ctx, :]
    _softmax_first(q1, k1_ref[0, :, 0:m_ctx], vc, m1, l1, a1)
    _softmax_first(q2, k2_ref[0, :, 0:m_ctx], vc, m2, l2, a2)

    @pl.when(pl.program_id(1) >= m_ctx // bq)
    def _():
        def body(j, carry):
            off = pl.multiple_of(m_ctx + j * bk, LANES)
            vv = v_ref[0, pl.ds(off, bk), :]
            _softmax_next(q1, k1_ref[0, :, pl.ds(off, bk)], vv, m1, l1, a1)
            _softmax_next(q2, k2_ref[0, :, pl.ds(off, bk)], vv, m2, l2, a2)
            return carry
        lax.fori_loop(0, n_blocks, body, 0)

    d = a1[...] / l1[...] - lam_ref[...] * (a2[...] / l2[...])
    y = d * lax.rsqrt(jnp.mean(d * d, axis=-1, keepdims=True) + EPS)
    o_ref[...] = y * sub_ref[...] * out_scale


def _diff_attn(dq, dkt, dv, lam, subln, out_scale, m_ctx):
    t = dq.shape[1]
    bq, bk, n_blocks = _attn_blocks(t, m_ctx)
    qspec = lambda c: pl.BlockSpec((1, bq, DA_DIM), lambda h, i: (2 * h + c, i, 0))
    kspec = lambda c: pl.BlockSpec((1, DA_DIM, t), lambda h, i: (2 * h + c, 0, 0))
    vec = pl.BlockSpec((1, DA_V), lambda h, i: (0, 0))
    stat = [pltpu.VMEM((bq, 1), F32), pltpu.VMEM((bq, 1), F32), pltpu.VMEM((bq, DA_V), F32)]
    return pl.pallas_call(
        functools.partial(_diff_attn_kernel, m_ctx=m_ctx, bq=bq, bk=bk, n_blocks=n_blocks,
                          out_scale=out_scale),
        grid=(HEADS, t // bq),
        in_specs=[qspec(0), qspec(1), kspec(0), kspec(1),
                  pl.BlockSpec((1, t, DA_V), lambda h, i: (h, 0, 0)), vec, vec],
        out_specs=pl.BlockSpec((bq, DA_V), lambda h, i: (i, h)),
        out_shape=jax.ShapeDtypeStruct((t, HEADS * DA_V), F32),
        scratch_shapes=stat + stat,
        compiler_params=_cparams(("parallel", "arbitrary")),
        name="diff_attn",
    )(dq, dq, dkt, dkt, dv, lam, subln)


def _odd_prep_kernel(p_ref, cos_ref, sa_ref, sb_ref, gq_ref, gk_ref,
                     dq_ref, dkt_ref, dv_ref, rq_ref, rk_ref):
    cos, sa, sb = cos_ref[...], sa_ref[...], sb_ref[...]
    da_w = HEADS * 2 * DA_DIM

    def sub_rms(x, g):
        low, ss_lo, ss_hi = _half_sums(x * x)
        inv = jnp.where(low, lax.rsqrt(ss_lo * (1.0 / DA_DIM) + EPS), lax.rsqrt(ss_hi * (1.0 / DA_DIM) + EPS))
        return x * inv * g

    for h in range(HEADS):
        col = h * LANES
        yq = _rope128(sub_rms(p_ref[:, col:col + LANES], gq_ref[...]), cos, sa, sb) * (DA_DIM ** -0.5)
        dq_ref[2 * h] = yq[:, :DA_DIM].astype(BF16)
        dq_ref[2 * h + 1] = yq[:, DA_DIM:].astype(BF16)
        ykt = _rope128(sub_rms(p_ref[:, da_w + col:da_w + col + LANES], gk_ref[...]), cos, sa, sb).T
        dkt_ref[2 * h] = ykt[:DA_DIM].astype(BF16)
        dkt_ref[2 * h + 1] = ykt[DA_DIM:].astype(BF16)
        dv_ref[h] = p_ref[:, 2 * da_w + col:2 * da_w + col + LANES].astype(BF16)
    r0 = 2 * da_w + HEADS * DA_V
    for b in range(HEADS // 2):
        col = r0 + b * LANES
        yq = _rope128(p_ref[:, col:col + LANES], cos, sa, sb)
        rq_ref[2 * b] = yq[:, :RT_K]
        rq_ref[2 * b + 1] = yq[:, RT_K:]
        col = r0 + HEADS * RT_K + b * LANES
        yk = _rope128(p_ref[:, col:col + LANES] * (RT_K ** -0.5), cos, sa, sb)
        rk_ref[2 * b] = yk[:, :RT_K]
        rk_ref[2 * b + 1] = yk[:, RT_K:]


def _odd_prep(p, tables, gq, gk):
    t = p.shape[0]
    rows = _row_block(t, ROW_BLOCK // 2)
    width = 3 * HEADS * 2 * DA_DIM + 2 * HEADS * RT_K
    full = lambda a: pl.BlockSpec(a.shape, lambda i: (0,) * a.ndim)
    rowb = lambda w: pl.BlockSpec((rows, w), lambda i: (i, 0))
    hm = lambda n, w: pl.BlockSpec((n, rows, w), lambda i: (0, i, 0))
    return pl.pallas_call(
        _odd_prep_kernel,
        grid=(t // rows,),
        in_specs=[rowb(width), rowb(LANES), rowb(LANES), rowb(LANES), full(gq), full(gk)],
        out_specs=[hm(2 * HEADS, DA_DIM),
                   pl.BlockSpec((2 * HEADS, DA_DIM, rows), lambda i: (0, 0, i)),
                   hm(HEADS, DA_V), hm(HEADS, RT_K), hm(HEADS, RT_K)],
        out_shape=[jax.ShapeDtypeStruct((2 * HEADS, t, DA_DIM), BF16),
                   jax.ShapeDtypeStruct((2 * HEADS, DA_DIM, t), BF16),
                   jax.ShapeDtypeStruct((HEADS, t, DA_V), BF16),
                   jax.ShapeDtypeStruct((HEADS, t, RT_K), F32),
                   jax.ShapeDtypeStruct((HEADS, t, RT_K), F32)],
        compiler_params=_cparams(("parallel",)),
        name="odd_prep",
    )(p, *tables, gq, gk)


def _outproj_kernel(x_ref, att_ref, of_ref, ob_ref, gate_ref, nrm_ref, wa_ref, wr_ref, mod_ref, o_ref,
                    *, m_ctx, rows):
    o = of_ref[...] + ob_ref[...]
    gate = gate_ref[...]
    rec = []
    for h in range(HEADS):
        oh = o[:, h * LANES:(h + 1) * LANES]
        inv = lax.rsqrt(jnp.mean(oh * oh, axis=-1, keepdims=True) + EPS)
        rec.append(oh * inv * nrm_ref[...] * _silu(gate[:, h * LANES:(h + 1) * LANES]))
    rec = jnp.concatenate(rec, axis=-1).astype(BF16)
    y = _dot(att_ref[...].astype(BF16), wa_ref[...]) + _dot(rec, wr_ref[...])
    g1 = _row_mod(mod_ref, 2, pl.program_id(0) * rows, rows, m_ctx)
    o_ref[...] = x_ref[...] + g1 * y


def _outproj(xt, att, o_f, o_b, gate_arr, gate_block, nrm, w_att, w_rec, mod, m_ctx):
    t, d = xt.shape
    rows = _row_block(t, ROW_BLOCK // 2)
    w = att.shape[1]
    full = lambda a: pl.BlockSpec(a.shape, lambda i: (0,) * a.ndim)
    rowb = lambda c: pl.BlockSpec((rows, c), lambda i: (i, 0))
    return pl.pallas_call(
        functools.partial(_outproj_kernel, m_ctx=m_ctx, rows=rows),
        grid=(t // rows,),
        in_specs=[rowb(d), rowb(w), rowb(w), rowb(w),
                  pl.BlockSpec((rows, w), lambda i: (i, gate_block)),
                  full(nrm), full(w_att), full(w_rec), full(mod)],
        out_specs=rowb(d),
        out_shape=jax.ShapeDtypeStruct((t, d), F32),
        compiler_params=_cparams(("parallel",)),
        name="out_proj",
    )(xt, att, o_f, o_b, gate_arr, nrm, w_att, w_rec, mod)


def _mlp_kernel(x_ref, mod_ref, nw_ref, w1_ref, w2_ref, o_ref, h_sc, acc_sc, *, m_ctx, rows):
    j = pl.program_id(1)
    row0 = pl.program_id(0) * rows

    @pl.when(j == 0)
    def _():
        h_sc[...] = _modnorm(x_ref[...], nw_ref[...],
                             _row_mod(mod_ref, 4, row0, rows, m_ctx),
                             _row_mod(mod_ref, 3, row0, rows, m_ctx)).astype(BF16)
        acc_sc[...] = jnp.zeros_like(acc_sc)

    u = jnp.maximum(_dot(h_sc[...], w1_ref[...]), 0.0)
    acc_sc[...] += _dot((u * u).astype(BF16), w2_ref[...])

    @pl.when(j == pl.num_programs(1) - 1)
    def _():
        o_ref[...] = x_ref[...] + _row_mod(mod_ref, 5, row0, rows, m_ctx) * acc_sc[...]


def _mlp(xt, mod, nw, w1, w2, m_ctx):
    t, d = xt.shape
    hid = w1.shape[1]
    rows = _row_block(t, ROW_BLOCK)
    th = 512
    full = lambda a: pl.BlockSpec(a.shape, lambda i, j: (0,) * a.ndim)
    return pl.pallas_call(
        functools.partial(_mlp_kernel, m_ctx=m_ctx, rows=rows),
        grid=(t // rows, hid // th),
        in_specs=[pl.BlockSpec((rows, d), lambda i, j: (i, 0)), full(mod), full(nw),
                  pl.BlockSpec((d, th), lambda i, j: (0, j)),
                  pl.BlockSpec((th, d), lambda i, j: (j, 0))],
        out_specs=pl.BlockSpec((rows, d), lambda i, j: (i, 0)),
        out_shape=jax.ShapeDtypeStruct((t, d), F32),
        scratch_shapes=[pltpu.VMEM((rows, d), BF16), pltpu.VMEM((rows, d), F32)],
        compiler_params=_cparams(("parallel", "arbitrary")),
        name="mlp",
    )(xt, mod, nw, w1, w2)


def _head_major(w, parts):
    k = w.shape[0]
    wh = w.reshape(k, HEADS, sum(parts))
    out, off = [], 0
    for width in parts:
        out.append(wh[:, :, off:off + width].reshape(k, HEADS * width))
        off += width
    return jnp.concatenate(out, axis=1)


def kernel(x, c, ctx, c_ctx, ada_w, ada_b, norm_w, w_o, mlp_w1, mlp_w2, a_w_in, hg_lb, hg_norm, mla_q_norm,
           mla_kv_norm, mla_w_uq, mla_w_ukv, mla_qk_q, mla_qk_k, c_w_in, da_lambda, da_qk_q, da_qk_k,
           da_subln, rt_decay, rt_norm):
    assert x.shape[0] == 1 and ctx.shape[0] == 1
    n, d = x.shape[1], x.shape[2]
    m_ctx = ctx.shape[1]
    depth = ada_w.shape[0]
    xt = jnp.concatenate([ctx[0], x[0]], axis=0)

    cvec = jnp.zeros((8, d), F32).at[0].set(c_ctx).at[1].set(c[0])
    mods = _ada_table(cvec, ada_w, ada_b)[:, :2].reshape(depth, 2, 6, d)
    tables = _rope_tables(n, m_ctx)

    lb = jnp.cumsum(jax.nn.softmax(hg_lb.astype(F32), axis=0), axis=0)
    lb = lb - lb[:1]
    log_lb = jnp.log(lb).reshape(-1, 2, 1, HG_W)
    log1m_lb = jnp.log1p(-lb).reshape(-1, 2, 1, HG_W)

    for l in range(depth):
        j = l // 2
        mod = mods[l]
        nw = norm_w[l]
        wo = w_o[l].astype(BF16)
        if l % 2 == 0:
            w_in = a_w_in[j]
            w_main = w_in[:, :5 * HG_W].astype(BF16)
            w_mla = jnp.pad(w_in[:, 5 * HG_W:], ((0, 0), (0, MLA_IN - (w_in.shape[1] - 5 * HG_W)))).astype(BF16)
            p_main, p_mla = _proj(xt, mod, nw[0:1], [w_main, w_mla], m_ctx)
            o_f, o_b = _hgrn2(p_main, log_lb[j], log1m_lb[j], m_ctx)
            gq = mla_qk_q[j]
            gk = mla_qk_k[j]
            q, kt, v = _mla_prep(
                p_mla, tables, mla_q_norm[j][None], mla_kv_norm[j][None],
                _head_major(mla_w_uq[j], (MLA_NOPE, MLA_ROPE)).astype(BF16),
                _head_major(mla_w_ukv[j], (MLA_NOPE, MLA_V)).astype(BF16),
                jnp.stack([gq[:MLA_NOPE], jnp.tile(gq[MLA_NOPE:], 2)]),
                jnp.stack([gk[:MLA_NOPE], jnp.tile(gk[MLA_NOPE:], 2)]))
            att = _mla_attn(q, kt, v, m_ctx)
            xt = _outproj(xt, att, o_f, o_b, p_main, 4, hg_norm[j][None], wo[HG_W:], wo[:HG_W], mod, m_ctx)
        else:
            (p,) = _proj(xt, mod, nw[0:1], [c_w_in[j].astype(BF16)], m_ctx)
            dq, dkt, dv, rq, rk = _odd_prep(p, tables, jnp.tile(da_qk_q[j], 2)[None], jnp.tile(da_qk_k[j], 2)[None])
            lam_init = 0.8 - 0.6 * math.exp(-0.3 * l)
            lf = da_lambda[j].astype(F32)
            lam = jnp.exp(jnp.sum(lf[0] * lf[1])) - jnp.exp(jnp.sum(lf[2] * lf[3])) + lam_init
            att = _diff_attn(dq, dkt, dv, jnp.full((1, DA_V), lam, F32), da_subln[j][None], 1.0 - lam_init, m_ctx)
            lg = jax.nn.log_sigmoid(rt_decay[j].astype(F32))
            lg = jnp.broadcast_to(lg[:, :, None, None], (2, HEADS, 8, LANES))
            r_f, r_b = _retention(rq, rk, p, (3 * HEADS * 2 * DA_DIM + 2 * HEADS * RT_K) // RT_V, lg, m_ctx)
            xt = _outproj(xt, att, r_f, r_b, p, 5, rt_norm[j][None], wo[:HG_W], wo[HG_W:], mod, m_ctx)
        xt = _mlp(xt, mod, nw[1:2], mlp_w1[l].astype(BF16), mlp_w2[l].astype(BF16), m_ctx)
    return xt[m_ctx:][None]
```

```python
import functools
import math

import numpy as np
import jax
import jax.numpy as jnp
from jax import lax
from jax.experimental import pallas as pl
from jax.experimental.pallas import tpu as pltpu

F32 = jnp.float32
BF16 = jnp.bfloat16

GRID_W = 64
ROPE_DIM = 64
ROPE_BASE = 10000.0
EPS = 1e-6
HEADS = 4
HG_DIM = 128
HG_W = HEADS * HG_DIM
MLA_NOPE = 128
MLA_ROPE = ROPE_DIM
MLA_V = 128
MLA_QK = MLA_NOPE + MLA_ROPE
MLA_Q_RANK = 384
MLA_KV_RANK = 256
DA_DIM = ROPE_DIM
DA_V = 2 * DA_DIM
RT_K = ROPE_DIM
RT_V = 128
MLA_IN = 768

LANES = 128
VMEM_LIMIT = 56 * 1024 * 1024
ROW_BLOCK = 1280
ATT_BQ = 256
ATT_BK = 4096
SCAN_CHUNK = 64
RET_CHUNK = 128
V_AUG = 2 * LANES
SCORE_LIMIT = 40.0


def _cparams(sem):
    return pltpu.CompilerParams(dimension_semantics=sem, vmem_limit_bytes=VMEM_LIMIT)


def _row_block(t, target):
    best = None
    for r in range(LANES, min(t, target) + 1, LANES):
        if t % r == 0:
            best = r
    assert best is not None, t
    return best


def _dot(a, b):
    return jnp.dot(a, b, preferred_element_type=F32)


def _dot_nt(a, b):
    return lax.dot_general(a, b, (((1,), (1,)), ((), ())), preferred_element_type=F32)


def _dot_tn(a, b):
    return lax.dot_general(a, b, (((0,), (0,)), ((), ())), preferred_element_type=F32)


def _silu(x):
    return x * (1.0 / (1.0 + jnp.exp(-x)))


def _row_mod(mod_ref, k, row0, rows, m_ctx):
    r = row0 + lax.broadcasted_iota(jnp.int32, (rows, 1), 0)
    return jnp.where(r < m_ctx, mod_ref[0, k:k + 1, :], mod_ref[1, k:k + 1, :])


def _modnorm(x, nw, sc, sh):
    y = x * lax.rsqrt(jnp.mean(x * x, axis=-1, keepdims=True) + EPS)
    return y * nw * (1.0 + sc) + sh


def _ada_kernel(c_ref, w_ref, b_ref, o_ref):
    cv = c_ref[...]
    o_ref[0] = jnp.dot(_silu(cv), w_ref[0], precision=lax.Precision.HIGHEST,
                       preferred_element_type=F32) + b_ref[0]


def _ada_table(cvec, ada_w, ada_b):
    depth, d, d6 = ada_w.shape
    tn = d6 // 4
    return pl.pallas_call(
        _ada_kernel,
        grid=(depth, d6 // tn),
        in_specs=[pl.BlockSpec((8, d), lambda l, j: (0, 0)),
                  pl.BlockSpec((1, d, tn), lambda l, j: (l, 0, j)),
                  pl.BlockSpec((1, 1, tn), lambda l, j: (l, 0, j))],
        out_specs=pl.BlockSpec((1, 8, tn), lambda l, j: (l, 0, j)),
        out_shape=jax.ShapeDtypeStruct((depth, 8, d6), F32),
        compiler_params=_cparams(("parallel", "parallel")),
        name="ada_table",
    )(cvec, ada_w, ada_b.reshape(depth, 1, d6))


def _proj_kernel(*refs, n_w, m_ctx, rows):
    x_ref, mod_ref, nw_ref = refs[:3]
    w_refs = refs[3:3 + n_w]
    o_refs = refs[3 + n_w:]
    row0 = pl.program_id(0) * rows
    h = _modnorm(x_ref[...], nw_ref[...],
                 _row_mod(mod_ref, 1, row0, rows, m_ctx),
                 _row_mod(mod_ref, 0, row0, rows, m_ctx)).astype(BF16)
    for w_ref, o_ref in zip(w_refs, o_refs):
        o_ref[...] = _dot(h, w_ref[...])


def _proj(xt, mod, nw, weights, m_ctx):
    t, d = xt.shape
    rows = _row_block(t, ROW_BLOCK // 2)
    full = lambda a: pl.BlockSpec(a.shape, lambda i: (0,) * a.ndim)
    return pl.pallas_call(
        functools.partial(_proj_kernel, n_w=len(weights), m_ctx=m_ctx, rows=rows),
        grid=(t // rows,),
        in_specs=[pl.BlockSpec((rows, d), lambda i: (i, 0)), full(mod), full(nw)]
                 + [full(w) for w in weights],
        out_specs=[pl.BlockSpec((rows, w.shape[1]), lambda i: (i, 0)) for w in weights],
        out_shape=[jax.ShapeDtypeStruct((t, w.shape[1]), F32) for w in weights],
        compiler_params=_cparams(("parallel",)),
        name="in_proj",
    )(xt, mod, nw, *weights)


def _hier_tables(c, reverse):
    levels = int(math.log2(c))
    assert 1 << levels == c
    cums = np.zeros(((levels + 1) * c, c), np.float32)
    roles = np.zeros((levels, c, LANES), np.float32)
    masks = np.zeros((levels + 1, c, c), np.float32)
    for li in range(levels):
        h = c >> (li + 1)
        for t in range(c):
            base = (t // (2 * h)) * 2 * h
            late = (t - base) >= h
            if not reverse:
                if late:
                    cums[li * c + t, base + h:t + 1] = 1.0
                else:
                    cums[li * c + t, t + 1:base + h] = 1.0
            else:
                if late:
                    cums[li * c + t, base + h:t] = 1.0
                else:
                    cums[li * c + t, t:base + h] = 1.0
            is_query = late != reverse
            roles[li, t, :] = 1.0 if is_query else 0.0
        for t in range(c):
            for s in range(c):
                same = (t // (2 * h)) == (s // (2 * h))
                if same and roles[li, t, 0] == 1.0 and roles[li, s, 0] == 0.0:
                    masks[li, t, s] = 1.0
    for t in range(c):
        if not reverse:
            cums[levels * c + t, :t + 1] = 1.0
        else:
            cums[levels * c + t, t:] = 1.0
    masks[levels] = np.eye(c, dtype=np.float32)
    return cums, roles, masks


def _split3(x):
    hi = x.astype(BF16)
    r1 = x - hi.astype(F32)
    mid = r1.astype(BF16)
    lo = (r1 - mid.astype(F32)).astype(BF16)
    return hi, mid, lo


def _gated_chunk(q, kk, v, logf, cum_ref, role_ref, mask_ref, st_ref, c, reverse):
    levels = role_ref.shape[0]
    cum = cum_ref[...]
    hi, mid, lo = _split3(logf)
    x = _dot(cum, hi) + _dot(cum, mid) + _dot(cum, lo)
    run = x[levels * c:(levels + 1) * c]
    tot = run[0:1] if reverse else run[c - 1:c]
    qb = q.astype(BF16)
    kb = kk.astype(BF16)
    a = _dot_nt(qb, kb) * mask_ref[levels]
    for li in range(levels):
        e = jnp.exp(x[li * c:(li + 1) * c])
        z = (jnp.where(role_ref[li] > 0.5, q, kk) * e).astype(BF16)
        a = a + _dot_nt(z, z) * mask_ref[li]
    st = st_ref[...]
    vb = v.astype(BF16)
    o = _dot(a.astype(BF16), vb) + _dot_nt((q * jnp.exp(run)).astype(BF16), st.astype(BF16))
    kd = (kk * jnp.exp(tot - run)).astype(BF16)
    st_ref[...] = st * jnp.exp(tot) + _dot_tn(vb, kd)
    return o


def _hgrn2_gate(z, log_lb, log1m_lb):
    soft = jnp.log1p(jnp.exp(-jnp.abs(z)))
    b = log1m_lb + (jnp.minimum(z, 0.0) - soft)
    logf = jnp.maximum(log_lb, b) + jnp.log1p(jnp.exp(-jnp.abs(log_lb - b)))
    return logf, jnp.exp(log1m_lb + (jnp.minimum(-z, 0.0) - soft))


def _hgrn2_kernel(qf_ref, zf_ref, vf_ref, qb_ref, zb_ref, vb_ref, llb_ref, l1m_ref,
                  cumf_ref, rolef_ref, maskf_ref, cumb_ref, roleb_ref, maskb_ref,
                  of_ref, ob_ref, sf_ref, sb_ref, *, c):
    @pl.when(pl.program_id(1) == 0)
    def _():
        sf_ref[...] = jnp.zeros_like(sf_ref)
        sb_ref[...] = jnp.zeros_like(sb_ref)

    logf, kk = _hgrn2_gate(zf_ref[...], llb_ref[0], l1m_ref[0])
    of_ref[...] = _gated_chunk(_silu(qf_ref[...]), kk, vf_ref[...], logf,
                               cumf_ref, rolef_ref, maskf_ref, sf_ref, c, False)
    logf, kk = _hgrn2_gate(zb_ref[...], llb_ref[1], l1m_ref[1])
    ob_ref[...] = _gated_chunk(_silu(qb_ref[...]), kk, vb_ref[...], logf,
                               cumb_ref, roleb_ref, maskb_ref, sb_ref, c, True)


def _bwd_chunk(i, mc, steps):
    return jnp.where(i < mc, mc - 1 - i, steps - 1 - i + mc)


def _hgrn2(p_main, log_lb, log1m_lb, m_ctx):
    t = p_main.shape[0]
    c = SCAN_CHUNK
    steps, mc = t // c, m_ctx // c
    tabs = [jnp.asarray(a) for rev in (False, True) for a in _hier_tables(c, rev)]
    tabs[0], tabs[3] = tabs[0].astype(BF16), tabs[3].astype(BF16)
    fwd = lambda sec: pl.BlockSpec((c, HG_DIM), lambda h, i: (i, sec * HEADS + h))
    bwd = lambda sec: pl.BlockSpec((c, HG_DIM), lambda h, i: (_bwd_chunk(i, mc, steps), sec * HEADS + h))
    full = lambda a: pl.BlockSpec(a.shape, lambda h, i: (0,) * a.ndim)
    lbspec = pl.BlockSpec((2, 1, HG_DIM), lambda h, i: (0, 0, h))
    return pl.pallas_call(
        functools.partial(_hgrn2_kernel, c=c),
        grid=(HEADS, steps),
        in_specs=[fwd(0), fwd(1), fwd(3), bwd(0), bwd(2), bwd(3), lbspec, lbspec]
                 + [full(a) for a in tabs],
        out_specs=[pl.BlockSpec((c, HG_DIM), lambda h, i: (i, h)),
                   pl.BlockSpec((c, HG_DIM), lambda h, i: (_bwd_chunk(i, mc, steps), h))],
        out_shape=[jax.ShapeDtypeStruct((t, HG_W), F32)] * 2,
        scratch_shapes=[pltpu.VMEM((HG_DIM, HG_DIM), F32)] * 2,
        compiler_params=_cparams(("parallel", "arbitrary")),
        name="hgrn2_scan",
    )(p_main, p_main, p_main, p_main, p_main, p_main, log_lb, log1m_lb, *tabs)


def _ret_chunk(q, k, v, lg, st_ref, c, reverse):
    t = lax.broadcasted_iota(jnp.int32, (c, c), 0)
    s = lax.broadcasted_iota(jnp.int32, (c, c), 1)
    dlt = (s - t) if reverse else (t - s)
    dec = jnp.where(dlt >= 0, jnp.exp(lg[:, :c] * jnp.maximum(dlt, 0).astype(F32)), 0.0)
    r = lax.broadcasted_iota(jnp.int32, (c, RT_K), 0).astype(F32)
    lk = lg[:, :RT_K]
    qdec = jnp.exp(lk * ((c - r) if reverse else (r + 1.0)))
    kdec = jnp.exp(lk * (r if reverse else (c - 1.0 - r)))
    a = _dot_nt(q.astype(BF16), k.astype(BF16)) * dec
    st = st_ref[...]
    vb = v.astype(BF16)
    o = _dot(a.astype(BF16), vb) + _dot_nt((q * qdec).astype(BF16), st.astype(BF16))
    st_ref[...] = st * jnp.exp(lk * float(c)) + _dot_tn(vb, (k * kdec).astype(BF16))
    return o


def _ret_kernel(qf_ref, kf_ref, vf_ref, qb_ref, kb_ref, vb_ref, lg_ref,
                of_ref, ob_ref, sf_ref, sb_ref, *, c):
    @pl.when(pl.program_id(1) == 0)
    def _():
        sf_ref[...] = jnp.zeros_like(sf_ref)
        sb_ref[...] = jnp.zeros_like(sb_ref)

    of_ref[...] = _ret_chunk(qf_ref[0], kf_ref[0], vf_ref[...], lg_ref[0, 0, 0:1, :], sf_ref, c, False)
    ob_ref[...] = _ret_chunk(qb_ref[0], kb_ref[0], vb_ref[...], lg_ref[1, 0, 0:1, :], sb_ref, c, True)


def _retention(rq, rk, p, v_col0, lg, m_ctx):
    t = p.shape[0]
    c = RET_CHUNK
    steps, mc = t // c, m_ctx // c
    fq = pl.BlockSpec((1, c, RT_K), lambda h, i: (h, i, 0))
    bq = pl.BlockSpec((1, c, RT_K), lambda h, i: (h, _bwd_chunk(i, mc, steps), 0))
    fv = pl.BlockSpec((c, RT_V), lambda h, i: (i, v_col0 + h))
    bv = pl.BlockSpec((c, RT_V), lambda h, i: (_bwd_chunk(i, mc, steps), v_col0 + h))
    return pl.pallas_call(
        functools.partial(_ret_kernel, c=c),
        grid=(HEADS, steps),
        in_specs=[fq, fq, fv, bq, bq, bv,
                  pl.BlockSpec((2, 1, 8, LANES), lambda h, i: (0, h, 0, 0))],
        out_specs=[pl.BlockSpec((c, RT_V), lambda h, i: (i, h)),
                   pl.BlockSpec((c, RT_V), lambda h, i: (_bwd_chunk(i, mc, steps), h))],
        out_shape=[jax.ShapeDtypeStruct((t, HEADS * RT_V), F32)] * 2,
        scratch_shapes=[pltpu.VMEM((RT_V, RT_K), F32)] * 2,
        compiler_params=_cparams(("parallel", "arbitrary")),
        name="retention_scan",
    )(rq, rk, p, rq, rk, p, lg)


def _rope128(y, cos, sin_a, sin_b):
    return y * cos + pltpu.roll(y, LANES - 16, 1) * sin_a + pltpu.roll(y, 16, 1) * sin_b


def _rope_tables(n, m_ctx):
    rows = n // GRID_W
    row = jnp.repeat(jnp.arange(rows, dtype=F32), GRID_W)
    col = jnp.tile(jnp.arange(GRID_W, dtype=F32), rows)
    quarter = ROPE_DIM // 4
    inv_freq = ROPE_BASE ** (-jnp.arange(quarter, dtype=F32) / quarter)
    ang_r = row[:, None] * inv_freq
    ang_c = col[:, None] * inv_freq
    ang = jnp.concatenate([ang_r, ang_r, ang_c, ang_c], axis=-1)
    cos, sin = jnp.cos(ang), jnp.sin(ang)
    first = (jnp.arange(ROPE_DIM) % 32) < 16
    sin_a = jnp.where(first, -sin, 0.0)
    sin_b = jnp.where(first, 0.0, sin)
    pad = lambda a, v: jnp.tile(jnp.concatenate([jnp.full((m_ctx, ROPE_DIM), v, F32), a], axis=0), (1, 2))
    return pad(cos, 1.0), pad(sin_a, 0.0), pad(sin_b, 0.0)


def _half_sums(sq):
    low = lax.broadcasted_iota(jnp.int32, sq.shape, 1) < 64
    lo = jnp.sum(jnp.where(low, sq, 0.0), axis=-1, keepdims=True)
    return low, lo, jnp.sum(sq, axis=-1, keepdims=True) - lo


def _mla_prep_kernel(p_ref, cos_ref, sa_ref, sb_ref, qn_ref, kvn_ref, wuq_ref, wukv_ref,
                     gq_ref, gk_ref, q_ref, kt_ref, v_ref):
    p = p_ref[...]
    cos, sa, sb = cos_ref[...], sa_ref[...], sb_ref[...]
    scale = MLA_QK ** -0.5

    def rms(x, w):
        return x * lax.rsqrt(jnp.mean(x * x, axis=-1, keepdims=True) + EPS) * w

    qu = _dot(rms(p[:, :MLA_Q_RANK], qn_ref[...]).astype(BF16), wuq_ref[...])
    kv = _dot(rms(p[:, MLA_Q_RANK:MLA_Q_RANK + MLA_KV_RANK], kvn_ref[...]).astype(BF16),
              wukv_ref[...])
    kr = p[:, MLA_Q_RANK + MLA_KV_RANK:]
    ss_kr = jnp.sum(kr * kr, axis=-1, keepdims=True)
    gq, gk = gq_ref[...], gk_ref[...]
    for b in range(HEADS // 2):
        qr = qu[:, HEADS * MLA_NOPE + b * LANES:HEADS * MLA_NOPE + (b + 1) * LANES]
        low, ss_lo, ss_hi = _half_sums(qr * qr)
        inv = []
        for j in range(2):
            h = 2 * b + j
            qn = qu[:, h * MLA_NOPE:(h + 1) * MLA_NOPE]
            ss = jnp.sum(qn * qn, axis=-1, keepdims=True) + (ss_lo, ss_hi)[j]
            inv.append(lax.rsqrt(ss * (1.0 / MLA_QK) + EPS))
            q_ref[h, :, 0:MLA_NOPE] = (qn * inv[j] * gq[0:1] * scale).astype(BF16)
        yr = _rope128(qr * jnp.where(low, inv[0], inv[1]) * gq[1:2], cos, sa, sb) * scale
        q_ref[2 * b, :, MLA_NOPE:MLA_QK] = yr[:, :MLA_ROPE].astype(BF16)
        q_ref[2 * b + 1, :, MLA_NOPE:MLA_QK] = yr[:, MLA_ROPE:].astype(BF16)
    low = lax.broadcasted_iota(jnp.int32, kr.shape, 1) < 64
    gk_rope = jnp.where(low, gk[1:2], 0.0)
    for h in range(HEADS):
        kn = kv[:, h * MLA_NOPE:(h + 1) * MLA_NOPE]
        inv = lax.rsqrt((jnp.sum(kn * kn, axis=-1, keepdims=True) + ss_kr) * (1.0 / MLA_QK) + EPS)
        kt_ref[h, 0:MLA_NOPE, :] = (kn * inv * gk[0:1]).T.astype(BF16)
        yr = _rope128(kr * inv * gk_rope, cos, sa, sb)
        kt_ref[h, MLA_NOPE:MLA_QK, :] = yr.T[0:MLA_ROPE].astype(BF16)
        v_ref[h, :, 0:MLA_V] = kv[:, HEADS * MLA_NOPE + h * MLA_V:HEADS * MLA_NOPE + (h + 1) * MLA_V].astype(BF16)
        v_ref[h, :, MLA_V:] = _ones_column(kr.shape[0])


def _mla_prep(p_mla, tables, qn, kvn, wuq, wukv, gq, gk):
    t = p_mla.shape[0]
    rows = _row_block(t, ROW_BLOCK // 2)
    full = lambda a: pl.BlockSpec(a.shape, lambda i: (0,) * a.ndim)
    rowb = lambda w: pl.BlockSpec((rows, w), lambda i: (i, 0))
    args = (qn, kvn, wuq, wukv, gq, gk)
    return pl.pallas_call(
        _mla_prep_kernel,
        grid=(t // rows,),
        in_specs=[rowb(MLA_IN), rowb(LANES), rowb(LANES), rowb(LANES)] + [full(a) for a in args],
        out_specs=[pl.BlockSpec((HEADS, rows, MLA_QK), lambda i: (0, i, 0)),
                   pl.BlockSpec((HEADS, MLA_QK, rows), lambda i: (0, 0, i)),
                   pl.BlockSpec((HEADS, rows, V_AUG), lambda i: (0, i, 0))],
        out_shape=[jax.ShapeDtypeStruct((HEADS, t, MLA_QK), BF16),
                   jax.ShapeDtypeStruct((HEADS, MLA_QK, t), BF16),
                   jax.ShapeDtypeStruct((HEADS, t, V_AUG), BF16)],
        compiler_params=_cparams(("parallel",)),
        name="mla_prep",
    )(p_mla, *tables, *args)


def _ones_column(rows):
    return (lax.broadcasted_iota(jnp.int32, (rows, LANES), 1) == 0).astype(BF16)


def _attend_first(q, kt, v, acc_sc, m_sc):
    s = _dot(q, kt)
    if m_sc is not None:
        mx = jnp.max(s, axis=-1, keepdims=True)
        m_sc[...] = mx
        s = s - mx
    acc_sc[...] = _dot(jnp.exp(s).astype(BF16), v)


def _attend_next(q, kt, v, acc_sc, m_sc):
    s = _dot(q, kt)
    if m_sc is None:
        acc_sc[...] += _dot(jnp.exp(s).astype(BF16), v)
    else:
        m_prev = m_sc[...]
        m_new = jnp.maximum(m_prev, jnp.max(s, axis=-1, keepdims=True))
        m_sc[...] = m_new
        acc_sc[...] = jnp.exp(m_prev - m_new) * acc_sc[...] + _dot(jnp.exp(s - m_new).astype(BF16), v)


def _attend_all(qs, kt_refs, v_ref, accs, ms, m_ctx, bq, bk, n_blocks):
    vc = v_ref[0, 0:m_ctx, :]
    for q, kt_ref, acc, m in zip(qs, kt_refs, accs, ms):
        _attend_first(q, kt_ref[0, :, 0:m_ctx], vc, acc, m)

    @pl.when(pl.program_id(1) >= m_ctx // bq)
    def _():
        def body(j, carry):
            off = pl.multiple_of(m_ctx + j * bk, LANES)
            vv = v_ref[0, pl.ds(off, bk), :]
            for q, kt_ref, acc, m in zip(qs, kt_refs, accs, ms):
                _attend_next(q, kt_ref[0, :, pl.ds(off, bk)], vv, acc, m)
            return carry
        lax.fori_loop(0, n_blocks, body, 0, unroll=True)


def _normalised(acc_sc):
    acc = acc_sc[...]
    return acc[:, :LANES] / acc[:, LANES:LANES + 1]


def _mla_attn_kernel(q_ref, kt_ref, v_ref, o_ref, acc_sc, *m_sc, m_ctx, bq, bk, n_blocks):
    m = m_sc[0] if m_sc else None
    _attend_all([q_ref[0]], [kt_ref], v_ref, [acc_sc], [m], m_ctx, bq, bk, n_blocks)
    o_ref[...] = _normalised(acc_sc)


def _attn_blocks(t, m_ctx):
    bq = ATT_BQ
    assert m_ctx % bq == 0 and t % bq == 0
    n_lat = t - m_ctx
    bk = _row_block(n_lat, ATT_BK)
    return bq, bk, n_lat // bk


def _attn_scratch(bq, n_pairs, shifted):
    return [pltpu.VMEM((bq, V_AUG), F32)] * n_pairs + ([pltpu.VMEM((bq, 1), F32)] * n_pairs if shifted else [])


def _by_score_bound(bound, attend, *operands):
    return lax.cond(bound <= SCORE_LIMIT,
                    functools.partial(attend, shifted=False),
                    functools.partial(attend, shifted=True), *operands)


def _mla_attn(q, kt, v, m_ctx, shifted):
    t = q.shape[1]
    bq, bk, n_blocks = _attn_blocks(t, m_ctx)
    return pl.pallas_call(
        functools.partial(_mla_attn_kernel, m_ctx=m_ctx, bq=bq, bk=bk, n_blocks=n_blocks),
        grid=(HEADS, t // bq),
        in_specs=[pl.BlockSpec((1, bq, MLA_QK), lambda h, i: (h, i, 0)),
                  pl.BlockSpec((1, MLA_QK, t), lambda h, i: (h, 0, 0)),
                  pl.BlockSpec((1, t, V_AUG), lambda h, i: (h, 0, 0))],
        out_specs=pl.BlockSpec((bq, MLA_V), lambda h, i: (i, h)),
        out_shape=jax.ShapeDtypeStruct((t, HEADS * MLA_V), F32),
        scratch_shapes=_attn_scratch(bq, 1, shifted),
        compiler_params=_cparams(("parallel", "arbitrary")),
        name="mla_attn_shifted" if shifted else "mla_attn",
    )(q, kt, v)


def _diff_attn_kernel(q1_ref, q2_ref, k1_ref, k2_ref, v_ref, lam_ref, sub_ref, o_ref, a1, a2, *m_sc,
                      m_ctx, bq, bk, n_blocks, out_scale):
    ms = list(m_sc) if m_sc else [None, None]
    _attend_all([q1_ref[0], q2_ref[0]], [k1_ref, k2_ref], v_ref, [a1, a2], ms, m_ctx, bq, bk, n_blocks)
    d = _normalised(a1) - lam_ref[...] * _normalised(a2)
    y = d * lax.rsqrt(jnp.mean(d * d, axis=-1, keepdims=True) + EPS)
    o_ref[...] = y * sub_ref[...] * out_scale


def _diff_attn(dq, dkt, dv, lam, subln, m_ctx, out_scale, shifted):
    t = dq.shape[1]
    bq, bk, n_blocks = _attn_blocks(t, m_ctx)
    qspec = lambda c: pl.BlockSpec((1, bq, DA_DIM), lambda h, i: (2 * h + c, i, 0))
    kspec = lambda c: pl.BlockSpec((1, DA_DIM, t), lambda h, i: (2 * h + c, 0, 0))
    vec = pl.BlockSpec((1, DA_V), lambda h, i: (0, 0))
    return pl.pallas_call(
        functools.partial(_diff_attn_kernel, m_ctx=m_ctx, bq=bq, bk=bk, n_blocks=n_blocks,
                          out_scale=out_scale),
        grid=(HEADS, t // bq),
        in_specs=[qspec(0), qspec(1), kspec(0), kspec(1),
                  pl.BlockSpec((1, t, V_AUG), lambda h, i: (h, 0, 0)), vec, vec],
        out_specs=pl.BlockSpec((bq, DA_V), lambda h, i: (i, h)),
        out_shape=jax.ShapeDtypeStruct((t, HEADS * DA_V), F32),
        scratch_shapes=_attn_scratch(bq, 2, shifted),
        compiler_params=_cparams(("parallel", "arbitrary")),
        name="diff_attn_shifted" if shifted else "diff_attn",
    )(dq, dq, dkt, dkt, dv, lam, subln)


def _odd_prep_kernel(p_ref, cos_ref, sa_ref, sb_ref, gq_ref, gk_ref,
                     dq_ref, dkt_ref, dv_ref, rq_ref, rk_ref):
    cos, sa, sb = cos_ref[...], sa_ref[...], sb_ref[...]
    da_w = HEADS * 2 * DA_DIM

    def sub_rms(x, g):
        low, ss_lo, ss_hi = _half_sums(x * x)
        inv = jnp.where(low, lax.rsqrt(ss_lo * (1.0 / DA_DIM) + EPS), lax.rsqrt(ss_hi * (1.0 / DA_DIM) + EPS))
        return x * inv * g

    for h in range(HEADS):
        col = h * LANES
        yq = _rope128(sub_rms(p_ref[:, col:col + LANES], gq_ref[...]), cos, sa, sb) * (DA_DIM ** -0.5)
        dq_ref[2 * h] = yq[:, :DA_DIM].astype(BF16)
        dq_ref[2 * h + 1] = yq[:, DA_DIM:].astype(BF16)
        ykt = _rope128(sub_rms(p_ref[:, da_w + col:da_w + col + LANES], gk_ref[...]), cos, sa, sb).T
        dkt_ref[2 * h] = ykt[:DA_DIM].astype(BF16)
        dkt_ref[2 * h + 1] = ykt[DA_DIM:].astype(BF16)
        dv_ref[h, :, 0:DA_V] = p_ref[:, 2 * da_w + col:2 * da_w + col + LANES].astype(BF16)
        dv_ref[h, :, DA_V:] = _ones_column(cos.shape[0])
    r0 = 2 * da_w + HEADS * DA_V
    for b in range(HEADS // 2):
        col = r0 + b * LANES
        yq = _rope128(p_ref[:, col:col + LANES], cos, sa, sb)
        rq_ref[2 * b] = yq[:, :RT_K]
        rq_ref[2 * b + 1] = yq[:, RT_K:]
        col = r0 + HEADS * RT_K + b * LANES
        yk = _rope128(p_ref[:, col:col + LANES] * (RT_K ** -0.5), cos, sa, sb)
        rk_ref[2 * b] = yk[:, :RT_K]
        rk_ref[2 * b + 1] = yk[:, RT_K:]


def _odd_prep(p, tables, gq, gk):
    t = p.shape[0]
    rows = _row_block(t, ROW_BLOCK // 2)
    width = 3 * HEADS * 2 * DA_DIM + 2 * HEADS * RT_K
    full = lambda a: pl.BlockSpec(a.shape, lambda i: (0,) * a.ndim)
    rowb = lambda w: pl.BlockSpec((rows, w), lambda i: (i, 0))
    hm = lambda n, w: pl.BlockSpec((n, rows, w), lambda i: (0, i, 0))
    return pl.pallas_call(
        _odd_prep_kernel,
        grid=(t // rows,),
        in_specs=[rowb(width), rowb(LANES), rowb(LANES), rowb(LANES), full(gq), full(gk)],
        out_specs=[hm(2 * HEADS, DA_DIM),
                   pl.BlockSpec((2 * HEADS, DA_DIM, rows), lambda i: (0, 0, i)),
                   hm(HEADS, V_AUG), hm(HEADS, RT_K), hm(HEADS, RT_K)],
        out_shape=[jax.ShapeDtypeStruct((2 * HEADS, t, DA_DIM), BF16),
                   jax.ShapeDtypeStruct((2 * HEADS, DA_DIM, t), BF16),
                   jax.ShapeDtypeStruct((HEADS, t, V_AUG), BF16),
                   jax.ShapeDtypeStruct((HEADS, t, RT_K), F32),
                   jax.ShapeDtypeStruct((HEADS, t, RT_K), F32)],
        compiler_params=_cparams(("parallel",)),
        name="odd_prep",
    )(p, *tables, gq, gk)


def _outproj_kernel(x_ref, att_ref, of_ref, ob_ref, gate_ref, nrm_ref, wa_ref, wr_ref, mod_ref, o_ref,
                    *, m_ctx, rows):
    o = of_ref[...] + ob_ref[...]
    gate = gate_ref[...]
    rec = []
    for h in range(HEADS):
        oh = o[:, h * LANES:(h + 1) * LANES]
        inv = lax.rsqrt(jnp.mean(oh * oh, axis=-1, keepdims=True) + EPS)
        rec.append(oh * inv * nrm_ref[...] * _silu(gate[:, h * LANES:(h + 1) * LANES]))
    rec = jnp.concatenate(rec, axis=-1).astype(BF16)
    y = _dot(att_ref[...].astype(BF16), wa_ref[...]) + _dot(rec, wr_ref[...])
    g1 = _row_mod(mod_ref, 2, pl.program_id(0) * rows, rows, m_ctx)
    o_ref[...] = x_ref[...] + g1 * y


def _outproj(xt, att, o_f, o_b, gate_arr, gate_block, nrm, w_att, w_rec, mod, m_ctx):
    t, d = xt.shape
    rows = _row_block(t, ROW_BLOCK // 2)
    w = att.shape[1]
    full = lambda a: pl.BlockSpec(a.shape, lambda i: (0,) * a.ndim)
    rowb = lambda c: pl.BlockSpec((rows, c), lambda i: (i, 0))
    return pl.pallas_call(
        functools.partial(_outproj_kernel, m_ctx=m_ctx, rows=rows),
        grid=(t // rows,),
        in_specs=[rowb(d), rowb(w), rowb(w), rowb(w),
                  pl.BlockSpec((rows, w), lambda i: (i, gate_block)),
                  full(nrm), full(w_att), full(w_rec), full(mod)],
        out_specs=rowb(d),
        out_shape=jax.ShapeDtypeStruct((t, d), F32),
        compiler_params=_cparams(("parallel",)),
        name="out_proj",
    )(xt, att, o_f, o_b, gate_arr, nrm, w_att, w_rec, mod)


def _mlp_kernel(x_ref, mod_ref, nw_ref, w1_ref, w2_ref, o_ref, h_sc, acc_sc, *, m_ctx, rows):
    j = pl.program_id(1)
    row0 = pl.program_id(0) * rows

    @pl.when(j == 0)
    def _():
        h_sc[...] = _modnorm(x_ref[...], nw_ref[...],
                             _row_mod(mod_ref, 4, row0, rows, m_ctx),
                             _row_mod(mod_ref, 3, row0, rows, m_ctx)).astype(BF16)
        acc_sc[...] = jnp.zeros_like(acc_sc)

    u = jnp.maximum(_dot(h_sc[...], w1_ref[...]), 0.0)
    acc_sc[...] += _dot((u * u).astype(BF16), w2_ref[...])

    @pl.when(j == pl.num_programs(1) - 1)
    def _():
        o_ref[...] = x_ref[...] + _row_mod(mod_ref, 5, row0, rows, m_ctx) * acc_sc[...]


def _mlp(xt, mod, nw, w1, w2, m_ctx):
    t, d = xt.shape
    hid = w1.shape[1]
    rows = _row_block(t, ROW_BLOCK)
    th = 512
    full = lambda a: pl.BlockSpec(a.shape, lambda i, j: (0,) * a.ndim)
    return pl.pallas_call(
        functools.partial(_mlp_kernel, m_ctx=m_ctx, rows=rows),
        grid=(t // rows, hid // th),
        in_specs=[pl.BlockSpec((rows, d), lambda i, j: (i, 0)), full(mod), full(nw),
                  pl.BlockSpec((d, th), lambda i, j: (0, j)),
                  pl.BlockSpec((th, d), lambda i, j: (j, 0))],
        out_specs=pl.BlockSpec((rows, d), lambda i, j: (i, 0)),
        out_shape=jax.ShapeDtypeStruct((t, d), F32),
        scratch_shapes=[pltpu.VMEM((rows, d), BF16), pltpu.VMEM((rows, d), F32)],
        compiler_params=_cparams(("parallel", "arbitrary")),
        name="mlp",
    )(xt, mod, nw, w1, w2)


def _head_major(w, parts):
    k = w.shape[0]
    wh = w.reshape(k, HEADS, sum(parts))
    out, off = [], 0
    for width in parts:
        out.append(wh[:, :, off:off + width].reshape(k, HEADS * width))
        off += width
    return jnp.concatenate(out, axis=1)


def kernel(x, c, ctx, c_ctx, ada_w, ada_b, norm_w, w_o, mlp_w1, mlp_w2, a_w_in, hg_lb, hg_norm, mla_q_norm,
           mla_kv_norm, mla_w_uq, mla_w_ukv, mla_qk_q, mla_qk_k, c_w_in, da_lambda, da_qk_q, da_qk_k,
           da_subln, rt_decay, rt_norm):
    assert x.shape[0] == 1 and ctx.shape[0] == 1
    n, d = x.shape[1], x.shape[2]
    m_ctx = ctx.shape[1]
    depth = ada_w.shape[0]
    xt = jnp.concatenate([ctx[0], x[0]], axis=0)

    cvec = jnp.zeros((8, d), F32).at[0].set(c_ctx).at[1].set(c[0])
    mods = _ada_table(cvec, ada_w, ada_b)[:, :2].reshape(depth, 2, 6, d)
    tables = _rope_tables(n, m_ctx)

    lb = jnp.cumsum(jax.nn.softmax(hg_lb.astype(F32), axis=0), axis=0)
    lb = lb - lb[:1]
    log_lb = jnp.log(lb).reshape(-1, 2, 1, HG_W)
    log1m_lb = jnp.log1p(-lb).reshape(-1, 2, 1, HG_W)

    for l in range(depth):
        j = l // 2
        mod = mods[l]
        nw = norm_w[l]
        wo = w_o[l].astype(BF16)
        if l % 2 == 0:
            w_in = a_w_in[j]
            w_main = w_in[:, :5 * HG_W].astype(BF16)
            w_mla = jnp.pad(w_in[:, 5 * HG_W:], ((0, 0), (0, MLA_IN - (w_in.shape[1] - 5 * HG_W)))).astype(BF16)
            p_main, p_mla = _proj(xt, mod, nw[0:1], [w_main, w_mla], m_ctx)
            o_f, o_b = _hgrn2(p_main, log_lb[j], log1m_lb[j], m_ctx)
            gq = mla_qk_q[j]
            gk = mla_qk_k[j]
            q, kt, v = _mla_prep(
                p_mla, tables, mla_q_norm[j][None], mla_kv_norm[j][None],
                _head_major(mla_w_uq[j], (MLA_NOPE, MLA_ROPE)).astype(BF16),
                _head_major(mla_w_ukv[j], (MLA_NOPE, MLA_V)).astype(BF16),
                jnp.stack([gq[:MLA_NOPE], jnp.tile(gq[MLA_NOPE:], 2)]),
                jnp.stack([gk[:MLA_NOPE], jnp.tile(gk[MLA_NOPE:], 2)]))
            bound = 1.02 * MLA_QK ** 0.5 * jnp.max(jnp.abs(gq)) * jnp.max(jnp.abs(gk))
            att = _by_score_bound(bound, functools.partial(_mla_attn, m_ctx=m_ctx), q, kt, v)
            xt = _outproj(xt, att, o_f, o_b, p_main, 4, hg_norm[j][None], wo[HG_W:], wo[:HG_W], mod, m_ctx)
        else:
            (p,) = _proj(xt, mod, nw[0:1], [c_w_in[j].astype(BF16)], m_ctx)
            dq, dkt, dv, rq, rk = _odd_prep(p, tables, jnp.tile(da_qk_q[j], 2)[None], jnp.tile(da_qk_k[j], 2)[None])
            lam_init = 0.8 - 0.6 * math.exp(-0.3 * l)
            lf = da_lambda[j].astype(F32)
            lam = jnp.exp(jnp.sum(lf[0] * lf[1])) - jnp.exp(jnp.sum(lf[2] * lf[3])) + lam_init
            bound = 1.02 * DA_DIM ** 0.5 * jnp.max(jnp.abs(da_qk_q[j])) * jnp.max(jnp.abs(da_qk_k[j]))
            att = _by_score_bound(
                bound, functools.partial(_diff_attn, m_ctx=m_ctx, out_scale=1.0 - lam_init),
                dq, dkt, dv, jnp.full((1, DA_V), lam, F32), da_subln[j][None])
            lg = jax.nn.log_sigmoid(rt_decay[j].astype(F32))
            lg = jnp.broadcast_to(lg[:, :, None, None], (2, HEADS, 8, LANES))
            r_f, r_b = _retention(rq, rk, p, (3 * HEADS * 2 * DA_DIM + 2 * HEADS * RT_K) // RT_V, lg, m_ctx)
            xt = _outproj(xt, att, r_f, r_b, p, 5, rt_norm[j][None], wo[:HG_W], wo[HG_W:], mod, m_ctx)
        xt = _mlp(xt, mod, nw[1:2], mlp_w1[l].astype(BF16), mlp_w2[l].astype(BF16), m_ctx)
    return xt[m_ctx:][None]
```

```python
import functools
import math

import numpy as np
import jax
import jax.numpy as jnp
from jax import lax
from jax.experimental import pallas as pl
from jax.experimental.pallas import tpu as pltpu

F32 = jnp.float32
BF16 = jnp.bfloat16

GRID_W = 64
ROPE_DIM = 64
ROPE_BASE = 10000.0
EPS = 1e-6
HEADS = 4
HG_DIM = 128
HG_W = HEADS * HG_DIM
MLA_NOPE = 128
MLA_ROPE = ROPE_DIM
MLA_V = 128
MLA_QK = MLA_NOPE + MLA_ROPE
MLA_Q_RANK = 384
MLA_KV_RANK = 256
DA_DIM = ROPE_DIM
DA_V = 2 * DA_DIM
RT_K = ROPE_DIM
RT_V = 128
MLA_IN = 768

LANES = 128
VMEM_LIMIT = 56 * 1024 * 1024
ROW_BLOCK = 1280
ATT_BQ = 256
ATT_BK = 4096
SCAN_CHUNK = 64
RET_CHUNK = 128
V_AUG = 2 * LANES
SCORE_LIMIT = 40.0


def _cparams(sem):
    return pltpu.CompilerParams(dimension_semantics=sem, vmem_limit_bytes=VMEM_LIMIT)


def _row_block(t, target):
    best = None
    for r in range(LANES, min(t, target) + 1, LANES):
        if t % r == 0:
            best = r
    assert best is not None, t
    return best


def _dot(a, b):
    return jnp.dot(a, b, preferred_element_type=F32)


def _dot_nt(a, b):
    return lax.dot_general(a, b, (((1,), (1,)), ((), ())), preferred_element_type=F32)


def _dot_tn(a, b):
    return lax.dot_general(a, b, (((0,), (0,)), ((), ())), preferred_element_type=F32)


def _silu(x):
    return x * (1.0 / (1.0 + jnp.exp(-x)))


def _row_mod(mod_ref, k, row0, rows, m_ctx):
    r = row0 + lax.broadcasted_iota(jnp.int32, (rows, 1), 0)
    return jnp.where(r < m_ctx, mod_ref[0, k:k + 1, :], mod_ref[1, k:k + 1, :])


def _modnorm(x, nw, sc, sh):
    y = x * lax.rsqrt(jnp.mean(x * x, axis=-1, keepdims=True) + EPS)
    return y * nw * (1.0 + sc) + sh


def _ada_kernel(c_ref, w_ref, b_ref, o_ref):
    cv = c_ref[...]
    o_ref[0] = jnp.dot(_silu(cv), w_ref[0], precision=lax.Precision.HIGHEST,
                       preferred_element_type=F32) + b_ref[0]


def _ada_table(cvec, ada_w, ada_b):
    depth, d, d6 = ada_w.shape
    tn = d6 // 4
    return pl.pallas_call(
        _ada_kernel,
        grid=(depth, d6 // tn),
        in_specs=[pl.BlockSpec((8, d), lambda l, j: (0, 0)),
                  pl.BlockSpec((1, d, tn), lambda l, j: (l, 0, j)),
                  pl.BlockSpec((1, 1, tn), lambda l, j: (l, 0, j))],
        out_specs=pl.BlockSpec((1, 8, tn), lambda l, j: (l, 0, j)),
        out_shape=jax.ShapeDtypeStruct((depth, 8, d6), F32),
        compiler_params=_cparams(("parallel", "parallel")),
        name="ada_table",
    )(cvec, ada_w, ada_b.reshape(depth, 1, d6))


def _proj_kernel(*refs, n_w, m_ctx, rows):
    x_ref, mod_ref, nw_ref = refs[:3]
    w_refs = refs[3:3 + n_w]
    o_refs = refs[3 + n_w:]
    row0 = pl.program_id(0) * rows
    h = _modnorm(x_ref[...], nw_ref[...],
                 _row_mod(mod_ref, 1, row0, rows, m_ctx),
                 _row_mod(mod_ref, 0, row0, rows, m_ctx)).astype(BF16)
    for w_ref, o_ref in zip(w_refs, o_refs):
        o_ref[...] = _dot(h, w_ref[...])


def _proj(xt, mod, nw, weights, m_ctx):
    t, d = xt.shape
    rows = _row_block(t, ROW_BLOCK // 2)
    full = lambda a: pl.BlockSpec(a.shape, lambda i: (0,) * a.ndim)
    return pl.pallas_call(
        functools.partial(_proj_kernel, n_w=len(weights), m_ctx=m_ctx, rows=rows),
        grid=(t // rows,),
        in_specs=[pl.BlockSpec((rows, d), lambda i: (i, 0)), full(mod), full(nw)]
                 + [full(w) for w in weights],
        out_specs=[pl.BlockSpec((rows, w.shape[1]), lambda i: (i, 0)) for w in weights],
        out_shape=[jax.ShapeDtypeStruct((t, w.shape[1]), F32) for w in weights],
        compiler_params=_cparams(("parallel",)),
        name="in_proj",
    )(xt, mod, nw, *weights)


def _hier_tables(c, reverse):
    levels = int(math.log2(c))
    assert 1 << levels == c
    cums = np.zeros(((levels + 1) * c, c), np.float32)
    roles = np.zeros((levels, c, LANES), np.float32)
    masks = np.zeros((levels + 1, c, c), np.float32)
    for li in range(levels):
        h = c >> (li + 1)
        for t in range(c):
            base = (t // (2 * h)) * 2 * h
            late = (t - base) >= h
            if not reverse:
                if late:
                    cums[li * c + t, base + h:t + 1] = 1.0
                else:
                    cums[li * c + t, t + 1:base + h] = 1.0
            else:
                if late:
                    cums[li * c + t, base + h:t] = 1.0
                else:
                    cums[li * c + t, t:base + h] = 1.0
            is_query = late != reverse
            roles[li, t, :] = 1.0 if is_query else 0.0
        for t in range(c):
            for s in range(c):
                same = (t // (2 * h)) == (s // (2 * h))
                if same and roles[li, t, 0] == 1.0 and roles[li, s, 0] == 0.0:
                    masks[li, t, s] = 1.0
    for t in range(c):
        if not reverse:
            cums[levels * c + t, :t + 1] = 1.0
        else:
            cums[levels * c + t, t:] = 1.0
    masks[levels] = np.eye(c, dtype=np.float32)
    return cums, roles, masks


def _split3(x):
    hi = x.astype(BF16)
    r1 = x - hi.astype(F32)
    mid = r1.astype(BF16)
    lo = (r1 - mid.astype(F32)).astype(BF16)
    return hi, mid, lo


def _gated_chunks(q_all, kk_all, v_all, logf_all, cum_ref, role_ref, mask_ref, st_ref, o_ref, c, reverse):
    levels = role_ref.shape[0]
    x_all = _dot(cum_ref[...], jnp.concatenate(_split3(logf_all), axis=0))
    for h in range(HEADS):
        sl = slice(h * HG_DIM, (h + 1) * HG_DIM)
        x, q, kk = x_all[:, sl], q_all[:, sl], kk_all[:, sl]
        run = x[levels * c:(levels + 1) * c]
        tot = run[0:1] if reverse else run[c - 1:c]
        a = _dot_nt(q.astype(BF16), kk.astype(BF16)) * mask_ref[levels]
        for li in range(levels):
            e = jnp.exp(x[li * c:(li + 1) * c])
            z = (jnp.where(role_ref[li] > 0.5, q, kk) * e).astype(BF16)
            a = a + _dot_nt(z, z) * mask_ref[li]
        st = st_ref[h]
        vb = v_all[:, sl].astype(BF16)
        o_ref[:, sl] = _dot(a.astype(BF16), vb) + _dot_nt((q * jnp.exp(run)).astype(BF16), st.astype(BF16))
        kd = (kk * jnp.exp(tot - run)).astype(BF16)
        st_ref[h] = st * jnp.exp(tot) + _dot_tn(vb, kd)


def _hgrn2_gate(z, log_lb, log1m_lb):
    soft = jnp.log1p(jnp.exp(-jnp.abs(z)))
    b = log1m_lb + (jnp.minimum(z, 0.0) - soft)
    logf = jnp.maximum(log_lb, b) + jnp.log1p(jnp.exp(-jnp.abs(log_lb - b)))
    return logf, jnp.exp(log1m_lb + (jnp.minimum(-z, 0.0) - soft))


def _hgrn2_kernel(qf_ref, zf_ref, vf_ref, qb_ref, zb_ref, vb_ref, llb_ref, l1m_ref,
                  cumf_ref, rolef_ref, maskf_ref, cumb_ref, roleb_ref, maskb_ref,
                  of_ref, ob_ref, sf_ref, sb_ref, *, c):
    @pl.when(pl.program_id(0) == 0)
    def _():
        sf_ref[...] = jnp.zeros_like(sf_ref)
        sb_ref[...] = jnp.zeros_like(sb_ref)

    logf, kk = _hgrn2_gate(zf_ref[...], llb_ref[0], l1m_ref[0])
    _gated_chunks(_silu(qf_ref[...]), kk, vf_ref[...], logf,
                  cumf_ref, rolef_ref, maskf_ref, sf_ref, of_ref, c, False)
    logf, kk = _hgrn2_gate(zb_ref[...], llb_ref[1], l1m_ref[1])
    _gated_chunks(_silu(qb_ref[...]), kk, vb_ref[...], logf,
                  cumb_ref, roleb_ref, maskb_ref, sb_ref, ob_ref, c, True)


def _bwd_chunk(i, mc, steps):
    return jnp.where(i < mc, mc - 1 - i, steps - 1 - i + mc)


def _hgrn2(p_main, log_lb, log1m_lb, m_ctx):
    t = p_main.shape[0]
    c = SCAN_CHUNK
    steps, mc = t // c, m_ctx // c
    tabs = [jnp.asarray(a) for rev in (False, True) for a in _hier_tables(c, rev)]
    for k in (0, 3):
        tabs[k] = jnp.tile(tabs[k], (1, 3)).astype(BF16)
    fwd = lambda sec: pl.BlockSpec((c, HG_W), lambda i: (i, sec))
    bwd = lambda sec: pl.BlockSpec((c, HG_W), lambda i: (_bwd_chunk(i, mc, steps), sec))
    full = lambda a: pl.BlockSpec(a.shape, lambda i: (0,) * a.ndim)
    return pl.pallas_call(
        functools.partial(_hgrn2_kernel, c=c),
        grid=(steps,),
        in_specs=[fwd(0), fwd(1), fwd(3), bwd(0), bwd(2), bwd(3), full(log_lb), full(log1m_lb)]
                 + [full(a) for a in tabs],
        out_specs=[pl.BlockSpec((c, HG_W), lambda i: (i, 0)),
                   pl.BlockSpec((c, HG_W), lambda i: (_bwd_chunk(i, mc, steps), 0))],
        out_shape=[jax.ShapeDtypeStruct((t, HG_W), F32)] * 2,
        scratch_shapes=[pltpu.VMEM((HEADS, HG_DIM, HG_DIM), F32)] * 2,
        compiler_params=_cparams(("arbitrary",)),
        name="hgrn2_scan",
    )(p_main, p_main, p_main, p_main, p_main, p_main, log_lb, log1m_lb, *tabs)


def _ret_chunk(q, k, v, lg, st_ref, c, reverse):
    t = lax.broadcasted_iota(jnp.int32, (c, c), 0)
    s = lax.broadcasted_iota(jnp.int32, (c, c), 1)
    dlt = (s - t) if reverse else (t - s)
    dec = jnp.where(dlt >= 0, jnp.exp(lg[:, :c] * jnp.maximum(dlt, 0).astype(F32)), 0.0)
    r = lax.broadcasted_iota(jnp.int32, (c, RT_K), 0).astype(F32)
    lk = lg[:, :RT_K]
    qdec = jnp.exp(lk * ((c - r) if reverse else (r + 1.0)))
    kdec = jnp.exp(lk * (r if reverse else (c - 1.0 - r)))
    a = _dot_nt(q.astype(BF16), k.astype(BF16)) * dec
    st = st_ref[...]
    vb = v.astype(BF16)
    o = _dot(a.astype(BF16), vb) + _dot_nt((q * qdec).astype(BF16), st.astype(BF16))
    st_ref[...] = st * jnp.exp(lk * float(c)) + _dot_tn(vb, (k * kdec).astype(BF16))
    return o


def _ret_kernel(qf_ref, kf_ref, vf_ref, qb_ref, kb_ref, vb_ref, lg_ref,
                of_ref, ob_ref, sf_ref, sb_ref, *, c):
    @pl.when(pl.program_id(0) == 0)
    def _():
        sf_ref[...] = jnp.zeros_like(sf_ref)
        sb_ref[...] = jnp.zeros_like(sb_ref)

    for h in range(HEADS):
        sl = slice(h * RT_V, (h + 1) * RT_V)
        of_ref[:, sl] = _ret_chunk(qf_ref[h], kf_ref[h], vf_ref[:, sl], lg_ref[0, h, 0:1, :], sf_ref.at[h], c, False)
        ob_ref[:, sl] = _ret_chunk(qb_ref[h], kb_ref[h], vb_ref[:, sl], lg_ref[1, h, 0:1, :], sb_ref.at[h], c, True)


def _retention(rq, rk, p, v_block, lg, m_ctx):
    t = p.shape[0]
    c = RET_CHUNK
    steps, mc = t // c, m_ctx // c
    width = HEADS * RT_V
    fq = pl.BlockSpec((HEADS, c, RT_K), lambda i: (0, i, 0))
    bq = pl.BlockSpec((HEADS, c, RT_K), lambda i: (0, _bwd_chunk(i, mc, steps), 0))
    fv = pl.BlockSpec((c, width), lambda i: (i, v_block))
    bv = pl.BlockSpec((c, width), lambda i: (_bwd_chunk(i, mc, steps), v_block))
    return pl.pallas_call(
        functools.partial(_ret_kernel, c=c),
        grid=(steps,),
        in_specs=[fq, fq, fv, bq, bq, bv, pl.BlockSpec(lg.shape, lambda i: (0, 0, 0, 0))],
        out_specs=[pl.BlockSpec((c, width), lambda i: (i, 0)),
                   pl.BlockSpec((c, width), lambda i: (_bwd_chunk(i, mc, steps), 0))],
        out_shape=[jax.ShapeDtypeStruct((t, width), F32)] * 2,
        scratch_shapes=[pltpu.VMEM((HEADS, RT_V, RT_K), F32)] * 2,
        compiler_params=_cparams(("arbitrary",)),
        name="retention_scan",
    )(rq, rk, p, rq, rk, p, lg)


def _rope128(y, cos, sin_a, sin_b):
    return y * cos + pltpu.roll(y, LANES - 16, 1) * sin_a + pltpu.roll(y, 16, 1) * sin_b


def _rope_tables(n, m_ctx):
    rows = n // GRID_W
    row = jnp.repeat(jnp.arange(rows, dtype=F32), GRID_W)
    col = jnp.tile(jnp.arange(GRID_W, dtype=F32), rows)
    quarter = ROPE_DIM // 4
    inv_freq = ROPE_BASE ** (-jnp.arange(quarter, dtype=F32) / quarter)
    ang_r = row[:, None] * inv_freq
    ang_c = col[:, None] * inv_freq
    ang = jnp.concatenate([ang_r, ang_r, ang_c, ang_c], axis=-1)
    cos, sin = jnp.cos(ang), jnp.sin(ang)
    first = (jnp.arange(ROPE_DIM) % 32) < 16
    sin_a = jnp.where(first, -sin, 0.0)
    sin_b = jnp.where(first, 0.0, sin)
    pad = lambda a, v: jnp.tile(jnp.concatenate([jnp.full((m_ctx, ROPE_DIM), v, F32), a], axis=0), (1, 2))
    return pad(cos, 1.0), pad(sin_a, 0.0), pad(sin_b, 0.0)


def _half_sums(sq):
    low = lax.broadcasted_iota(jnp.int32, sq.shape, 1) < 64
    lo = jnp.sum(jnp.where(low, sq, 0.0), axis=-1, keepdims=True)
    return low, lo, jnp.sum(sq, axis=-1, keepdims=True) - lo


def _mla_prep_kernel(p_ref, cos_ref, sa_ref, sb_ref, qn_ref, kvn_ref, wuq_ref, wukv_ref,
                     gq_ref, gk_ref, q_ref, kt_ref, v_ref):
    p = p_ref[...]
    cos, sa, sb = cos_ref[...], sa_ref[...], sb_ref[...]
    scale = MLA_QK ** -0.5

    def rms(x, w):
        return x * lax.rsqrt(jnp.mean(x * x, axis=-1, keepdims=True) + EPS) * w

    qu = _dot(rms(p[:, :MLA_Q_RANK], qn_ref[...]).astype(BF16), wuq_ref[...])
    kv = _dot(rms(p[:, MLA_Q_RANK:MLA_Q_RANK + MLA_KV_RANK], kvn_ref[...]).astype(BF16),
              wukv_ref[...])
    kr = p[:, MLA_Q_RANK + MLA_KV_RANK:]
    ss_kr = jnp.sum(kr * kr, axis=-1, keepdims=True)
    gq, gk = gq_ref[...], gk_ref[...]
    for b in range(HEADS // 2):
        qr = qu[:, HEADS * MLA_NOPE + b * LANES:HEADS * MLA_NOPE + (b + 1) * LANES]
        low, ss_lo, ss_hi = _half_sums(qr * qr)
        inv = []
        for j in range(2):
            h = 2 * b + j
            qn = qu[:, h * MLA_NOPE:(h + 1) * MLA_NOPE]
            ss = jnp.sum(qn * qn, axis=-1, keepdims=True) + (ss_lo, ss_hi)[j]
            inv.append(lax.rsqrt(ss * (1.0 / MLA_QK) + EPS))
            q_ref[h, :, 0:MLA_NOPE] = (qn * inv[j] * gq[0:1] * scale).astype(BF16)
        yr = _rope128(qr * jnp.where(low, inv[0], inv[1]) * gq[1:2], cos, sa, sb) * scale
        q_ref[2 * b, :, MLA_NOPE:MLA_QK] = yr[:, :MLA_ROPE].astype(BF16)
        q_ref[2 * b + 1, :, MLA_NOPE:MLA_QK] = yr[:, MLA_ROPE:].astype(BF16)
    low = lax.broadcasted_iota(jnp.int32, kr.shape, 1) < 64
    gk_rope = jnp.where(low, gk[1:2], 0.0)
    for h in range(HEADS):
        kn = kv[:, h * MLA_NOPE:(h + 1) * MLA_NOPE]
        inv = lax.rsqrt((jnp.sum(kn * kn, axis=-1, keepdims=True) + ss_kr) * (1.0 / MLA_QK) + EPS)
        kt_ref[h, 0:MLA_NOPE, :] = (kn * inv * gk[0:1]).T.astype(BF16)
        yr = _rope128(kr * inv * gk_rope, cos, sa, sb)
        kt_ref[h, MLA_NOPE:MLA_QK, :] = yr.T[0:MLA_ROPE].astype(BF16)
        v_ref[h, :, 0:MLA_V] = kv[:, HEADS * MLA_NOPE + h * MLA_V:HEADS * MLA_NOPE + (h + 1) * MLA_V].astype(BF16)
        v_ref[h, :, MLA_V:] = _ones_column(kr.shape[0])


def _mla_prep(p_mla, tables, qn, kvn, wuq, wukv, gq, gk):
    t = p_mla.shape[0]
    rows = _row_block(t, ROW_BLOCK // 2)
    full = lambda a: pl.BlockSpec(a.shape, lambda i: (0,) * a.ndim)
    rowb = lambda w: pl.BlockSpec((rows, w), lambda i: (i, 0))
    args = (qn, kvn, wuq, wukv, gq, gk)
    return pl.pallas_call(
        _mla_prep_kernel,
        grid=(t // rows,),
        in_specs=[rowb(MLA_IN), rowb(LANES), rowb(LANES), rowb(LANES)] + [full(a) for a in args],
        out_specs=[pl.BlockSpec((HEADS, rows, MLA_QK), lambda i: (0, i, 0)),
                   pl.BlockSpec((HEADS, MLA_QK, rows), lambda i: (0, 0, i)),
                   pl.BlockSpec((HEADS, rows, V_AUG), lambda i: (0, i, 0))],
        out_shape=[jax.ShapeDtypeStruct((HEADS, t, MLA_QK), BF16),
                   jax.ShapeDtypeStruct((HEADS, MLA_QK, t), BF16),
                   jax.ShapeDtypeStruct((HEADS, t, V_AUG), BF16)],
        compiler_params=_cparams(("parallel",)),
        name="mla_prep",
    )(p_mla, *tables, *args)


def _ones_column(rows):
    return (lax.broadcasted_iota(jnp.int32, (rows, LANES), 1) == 0).astype(BF16)


def _attend_first(q, kt, v, acc_sc, m_sc):
    s = _dot(q, kt)
    if m_sc is not None:
        mx = jnp.max(s, axis=-1, keepdims=True)
        m_sc[...] = mx
        s = s - mx
    acc_sc[...] = _dot(jnp.exp(s).astype(BF16), v)


def _attend_next(q, kt, v, acc_sc, m_sc):
    s = _dot(q, kt)
    if m_sc is None:
        acc_sc[...] += _dot(jnp.exp(s).astype(BF16), v)
    else:
        m_prev = m_sc[...]
        m_new = jnp.maximum(m_prev, jnp.max(s, axis=-1, keepdims=True))
        m_sc[...] = m_new
        acc_sc[...] = jnp.exp(m_prev - m_new) * acc_sc[...] + _dot(jnp.exp(s - m_new).astype(BF16), v)


def _attend_all(qs, kt_refs, v_ref, accs, ms, m_ctx, bq, bk, n_blocks):
    vc = v_ref[0, 0:m_ctx, :]
    for q, kt_ref, acc, m in zip(qs, kt_refs, accs, ms):
        _attend_first(q, kt_ref[0, :, 0:m_ctx], vc, acc, m)

    @pl.when(pl.program_id(1) >= m_ctx // bq)
    def _():
        def body(j, carry):
            off = pl.multiple_of(m_ctx + j * bk, LANES)
            vv = v_ref[0, pl.ds(off, bk), :]
            for q, kt_ref, acc, m in zip(qs, kt_refs, accs, ms):
                _attend_next(q, kt_ref[0, :, pl.ds(off, bk)], vv, acc, m)
            return carry
        lax.fori_loop(0, n_blocks, body, 0, unroll=True)


def _normalised(acc_sc):
    acc = acc_sc[...]
    return acc[:, :LANES] / acc[:, LANES:LANES + 1]


def _mla_attn_kernel(q_ref, kt_ref, v_ref, o_ref, acc_sc, *m_sc, m_ctx, bq, bk, n_blocks):
    m = m_sc[0] if m_sc else None
    _attend_all([q_ref[0]], [kt_ref], v_ref, [acc_sc], [m], m_ctx, bq, bk, n_blocks)
    o_ref[...] = _normalised(acc_sc)


def _attn_blocks(t, m_ctx):
    bq = ATT_BQ
    assert m_ctx % bq == 0 and t % bq == 0
    n_lat = t - m_ctx
    bk = _row_block(n_lat, ATT_BK)
    return bq, bk, n_lat // bk


def _attn_scratch(bq, n_pairs, shifted):
    return [pltpu.VMEM((bq, V_AUG), F32)] * n_pairs + ([pltpu.VMEM((bq, 1), F32)] * n_pairs if shifted else [])


def _by_score_bound(bound, attend, *operands):
    return lax.cond(bound <= SCORE_LIMIT,
                    functools.partial(attend, shifted=False),
                    functools.partial(attend, shifted=True), *operands)


def _mla_attn(q, kt, v, m_ctx, shifted):
    t = q.shape[1]
    bq, bk, n_blocks = _attn_blocks(t, m_ctx)
    return pl.pallas_call(
        functools.partial(_mla_attn_kernel, m_ctx=m_ctx, bq=bq, bk=bk, n_blocks=n_blocks),
        grid=(HEADS, t // bq),
        in_specs=[pl.BlockSpec((1, bq, MLA_QK), lambda h, i: (h, i, 0)),
                  pl.BlockSpec((1, MLA_QK, t), lambda h, i: (h, 0, 0)),
                  pl.BlockSpec((1, t, V_AUG), lambda h, i: (h, 0, 0))],
        out_specs=pl.BlockSpec((bq, MLA_V), lambda h, i: (i, h)),
        out_shape=jax.ShapeDtypeStruct((t, HEADS * MLA_V), F32),
        scratch_shapes=_attn_scratch(bq, 1, shifted),
        compiler_params=_cparams(("parallel", "arbitrary")),
        name="mla_attn_shifted" if shifted else "mla_attn",
    )(q, kt, v)


def _diff_attn_kernel(q1_ref, q2_ref, k1_ref, k2_ref, v_ref, lam_ref, sub_ref, o_ref, a1, a2, *m_sc,
                      m_ctx, bq, bk, n_blocks, out_scale):
    ms = list(m_sc) if m_sc else [None, None]
    _attend_all([q1_ref[0], q2_ref[0]], [k1_ref, k2_ref], v_ref, [a1, a2], ms, m_ctx, bq, bk, n_blocks)
    d = _normalised(a1) - lam_ref[...] * _normalised(a2)
    y = d * lax.rsqrt(jnp.mean(d * d, axis=-1, keepdims=True) + EPS)
    o_ref[...] = y * sub_ref[...] * out_scale


def _diff_attn(dq, dkt, dv, lam, subln, m_ctx, out_scale, shifted):
    t = dq.shape[1]
    bq, bk, n_blocks = _attn_blocks(t, m_ctx)
    qspec = lambda c: pl.BlockSpec((1, bq, DA_DIM), lambda h, i: (2 * h + c, i, 0))
    kspec = lambda c: pl.BlockSpec((1, DA_DIM, t), lambda h, i: (2 * h + c, 0, 0))
    vec = pl.BlockSpec((1, DA_V), lambda h, i: (0, 0))
    return pl.pallas_call(
        functools.partial(_diff_attn_kernel, m_ctx=m_ctx, bq=bq, bk=bk, n_blocks=n_blocks,
                          out_scale=out_scale),
        grid=(HEADS, t // bq),
        in_specs=[qspec(0), qspec(1), kspec(0), kspec(1),
                  pl.BlockSpec((1, t, V_AUG), lambda h, i: (h, 0, 0)), vec, vec],
        out_specs=pl.BlockSpec((bq, DA_V), lambda h, i: (i, h)),
        out_shape=jax.ShapeDtypeStruct((t, HEADS * DA_V), F32),
        scratch_shapes=_attn_scratch(bq, 2, shifted),
        compiler_params=_cparams(("parallel", "arbitrary")),
        name="diff_attn_shifted" if shifted else "diff_attn",
    )(dq, dq, dkt, dkt, dv, lam, subln)


def _odd_prep_kernel(p_ref, cos_ref, sa_ref, sb_ref, gq_ref, gk_ref,
                     dq_ref, dkt_ref, dv_ref, rq_ref, rk_ref):
    cos, sa, sb = cos_ref[...], sa_ref[...], sb_ref[...]
    da_w = HEADS * 2 * DA_DIM

    def sub_rms(x, g):
        low, ss_lo, ss_hi = _half_sums(x * x)
        inv = jnp.where(low, lax.rsqrt(ss_lo * (1.0 / DA_DIM) + EPS), lax.rsqrt(ss_hi * (1.0 / DA_DIM) + EPS))
        return x * inv * g

    for h in range(HEADS):
        col = h * LANES
        yq = _rope128(sub_rms(p_ref[:, col:col + LANES], gq_ref[...]), cos, sa, sb) * (DA_DIM ** -0.5)
        dq_ref[2 * h] = yq[:, :DA_DIM].astype(BF16)
        dq_ref[2 * h + 1] = yq[:, DA_DIM:].astype(BF16)
        ykt = _rope128(sub_rms(p_ref[:, da_w + col:da_w + col + LANES], gk_ref[...]), cos, sa, sb).T
        dkt_ref[2 * h] = ykt[:DA_DIM].astype(BF16)
        dkt_ref[2 * h + 1] = ykt[DA_DIM:].astype(BF16)
        dv_ref[h, :, 0:DA_V] = p_ref[:, 2 * da_w + col:2 * da_w + col + LANES].astype(BF16)
        dv_ref[h, :, DA_V:] = _ones_column(cos.shape[0])
    r0 = 2 * da_w + HEADS * DA_V
    for b in range(HEADS // 2):
        col = r0 + b * LANES
        yq = _rope128(p_ref[:, col:col + LANES], cos, sa, sb)
        rq_ref[2 * b] = yq[:, :RT_K]
        rq_ref[2 * b + 1] = yq[:, RT_K:]
        col = r0 + HEADS * RT_K + b * LANES
        yk = _rope128(p_ref[:, col:col + LANES] * (RT_K ** -0.5), cos, sa, sb)
        rk_ref[2 * b] = yk[:, :RT_K]
        rk_ref[2 * b + 1] = yk[:, RT_K:]


def _odd_prep(p, tables, gq, gk):
    t = p.shape[0]
    rows = _row_block(t, ROW_BLOCK // 2)
    width = 3 * HEADS * 2 * DA_DIM + 2 * HEADS * RT_K
    full = lambda a: pl.BlockSpec(a.shape, lambda i: (0,) * a.ndim)
    rowb = lambda w: pl.BlockSpec((rows, w), lambda i: (i, 0))
    hm = lambda n, w: pl.BlockSpec((n, rows, w), lambda i: (0, i, 0))
    return pl.pallas_call(
        _odd_prep_kernel,
        grid=(t // rows,),
        in_specs=[rowb(width), rowb(LANES), rowb(LANES), rowb(LANES), full(gq), full(gk)],
        out_specs=[hm(2 * HEADS, DA_DIM),
                   pl.BlockSpec((2 * HEADS, DA_DIM, rows), lambda i: (0, 0, i)),
                   hm(HEADS, V_AUG), hm(HEADS, RT_K), hm(HEADS, RT_K)],
        out_shape=[jax.ShapeDtypeStruct((2 * HEADS, t, DA_DIM), BF16),
                   jax.ShapeDtypeStruct((2 * HEADS, DA_DIM, t), BF16),
                   jax.ShapeDtypeStruct((HEADS, t, V_AUG), BF16),
                   jax.ShapeDtypeStruct((HEADS, t, RT_K), F32),
                   jax.ShapeDtypeStruct((HEADS, t, RT_K), F32)],
        compiler_params=_cparams(("parallel",)),
        name="odd_prep",
    )(p, *tables, gq, gk)


def _outproj_kernel(x_ref, att_ref, of_ref, ob_ref, gate_ref, nrm_ref, wa_ref, wr_ref, mod_ref, o_ref,
                    *, m_ctx, rows):
    o = of_ref[...] + ob_ref[...]
    gate = gate_ref[...]
    rec = []
    for h in range(HEADS):
        oh = o[:, h * LANES:(h + 1) * LANES]
        inv = lax.rsqrt(jnp.mean(oh * oh, axis=-1, keepdims=True) + EPS)
        rec.append(oh * inv * nrm_ref[...] * _silu(gate[:, h * LANES:(h + 1) * LANES]))
    rec = jnp.concatenate(rec, axis=-1).astype(BF16)
    y = _dot(att_ref[...].astype(BF16), wa_ref[...]) + _dot(rec, wr_ref[...])
    g1 = _row_mod(mod_ref, 2, pl.program_id(0) * rows, rows, m_ctx)
    o_ref[...] = x_ref[...] + g1 * y


def _outproj(xt, att, o_f, o_b, gate_arr, gate_block, nrm, w_att, w_rec, mod, m_ctx):
    t, d = xt.shape
    rows = _row_block(t, ROW_BLOCK // 2)
    w = att.shape[1]
    full = lambda a: pl.BlockSpec(a.shape, lambda i: (0,) * a.ndim)
    rowb = lambda c: pl.BlockSpec((rows, c), lambda i: (i, 0))
    return pl.pallas_call(
        functools.partial(_outproj_kernel, m_ctx=m_ctx, rows=rows),
        grid=(t // rows,),
        in_specs=[rowb(d), rowb(w), rowb(w), rowb(w),
                  pl.BlockSpec((rows, w), lambda i: (i, gate_block)),
                  full(nrm), full(w_att), full(w_rec), full(mod)],
        out_specs=rowb(d),
        out_shape=jax.ShapeDtypeStruct((t, d), F32),
        compiler_params=_cparams(("parallel",)),
        name="out_proj",
    )(xt, att, o_f, o_b, gate_arr, nrm, w_att, w_rec, mod)


def _mlp_kernel(x_ref, mod_ref, nw_ref, w1_ref, w2_ref, o_ref, h_sc, acc_sc, *, m_ctx, rows):
    j = pl.program_id(1)
    row0 = pl.program_id(0) * rows

    @pl.when(j == 0)
    def _():
        h_sc[...] = _modnorm(x_ref[...], nw_ref[...],
                             _row_mod(mod_ref, 4, row0, rows, m_ctx),
                             _row_mod(mod_ref, 3, row0, rows, m_ctx)).astype(BF16)
        acc_sc[...] = jnp.zeros_like(acc_sc)

    u = jnp.maximum(_dot(h_sc[...], w1_ref[...]), 0.0)
    acc_sc[...] += _dot((u * u).astype(BF16), w2_ref[...])

    @pl.when(j == pl.num_programs(1) - 1)
    def _():
        o_ref[...] = x_ref[...] + _row_mod(mod_ref, 5, row0, rows, m_ctx) * acc_sc[...]


def _mlp(xt, mod, nw, w1, w2, m_ctx):
    t, d = xt.shape
    hid = w1.shape[1]
    rows = _row_block(t, ROW_BLOCK)
    th = 512
    full = lambda a: pl.BlockSpec(a.shape, lambda i, j: (0,) * a.ndim)
    return pl.pallas_call(
        functools.partial(_mlp_kernel, m_ctx=m_ctx, rows=rows),
        grid=(t // rows, hid // th),
        in_specs=[pl.BlockSpec((rows, d), lambda i, j: (i, 0)), full(mod), full(nw),
                  pl.BlockSpec((d, th), lambda i, j: (0, j)),
                  pl.BlockSpec((th, d), lambda i, j: (j, 0))],
        out_specs=pl.BlockSpec((rows, d), lambda i, j: (i, 0)),
        out_shape=jax.ShapeDtypeStruct((t, d), F32),
        scratch_shapes=[pltpu.VMEM((rows, d), BF16), pltpu.VMEM((rows, d), F32)],
        compiler_params=_cparams(("parallel", "arbitrary")),
        name="mlp",
    )(xt, mod, nw, w1, w2)


def _head_major(w, parts):
    k = w.shape[0]
    wh = w.reshape(k, HEADS, sum(parts))
    out, off = [], 0
    for width in parts:
        out.append(wh[:, :, off:off + width].reshape(k, HEADS * width))
        off += width
    return jnp.concatenate(out, axis=1)


def kernel(x, c, ctx, c_ctx, ada_w, ada_b, norm_w, w_o, mlp_w1, mlp_w2, a_w_in, hg_lb, hg_norm, mla_q_norm,
           mla_kv_norm, mla_w_uq, mla_w_ukv, mla_qk_q, mla_qk_k, c_w_in, da_lambda, da_qk_q, da_qk_k,
           da_subln, rt_decay, rt_norm):
    assert x.shape[0] == 1 and ctx.shape[0] == 1
    n, d = x.shape[1], x.shape[2]
    m_ctx = ctx.shape[1]
    depth = ada_w.shape[0]
    xt = jnp.concatenate([ctx[0], x[0]], axis=0)

    cvec = jnp.zeros((8, d), F32).at[0].set(c_ctx).at[1].set(c[0])
    mods = _ada_table(cvec, ada_w, ada_b)[:, :2].reshape(depth, 2, 6, d)
    tables = _rope_tables(n, m_ctx)

    lb = jnp.cumsum(jax.nn.softmax(hg_lb.astype(F32), axis=0), axis=0)
    lb = lb - lb[:1]
    log_lb = jnp.log(lb).reshape(-1, 2, 1, HG_W)
    log1m_lb = jnp.log1p(-lb).reshape(-1, 2, 1, HG_W)

    for l in range(depth):
        j = l // 2
        mod = mods[l]
        nw = norm_w[l]
        wo = w_o[l].astype(BF16)
        if l % 2 == 0:
            w_in = a_w_in[j]
            w_main = w_in[:, :5 * HG_W].astype(BF16)
            w_mla = jnp.pad(w_in[:, 5 * HG_W:], ((0, 0), (0, MLA_IN - (w_in.shape[1] - 5 * HG_W)))).astype(BF16)
            p_main, p_mla = _proj(xt, mod, nw[0:1], [w_main, w_mla], m_ctx)
            o_f, o_b = _hgrn2(p_main, log_lb[j], log1m_lb[j], m_ctx)
            gq = mla_qk_q[j]
            gk = mla_qk_k[j]
            q, kt, v = _mla_prep(
                p_mla, tables, mla_q_norm[j][None], mla_kv_norm[j][None],
                _head_major(mla_w_uq[j], (MLA_NOPE, MLA_ROPE)).astype(BF16),
                _head_major(mla_w_ukv[j], (MLA_NOPE, MLA_V)).astype(BF16),
                jnp.stack([gq[:MLA_NOPE], jnp.tile(gq[MLA_NOPE:], 2)]),
                jnp.stack([gk[:MLA_NOPE], jnp.tile(gk[MLA_NOPE:], 2)]))
            bound = 1.02 * MLA_QK ** 0.5 * jnp.max(jnp.abs(gq)) * jnp.max(jnp.abs(gk))
            att = _by_score_bound(bound, functools.partial(_mla_attn, m_ctx=m_ctx), q, kt, v)
            xt = _outproj(xt, att, o_f, o_b, p_main, 4, hg_norm[j][None], wo[HG_W:], wo[:HG_W], mod, m_ctx)
        else:
            (p,) = _proj(xt, mod, nw[0:1], [c_w_in[j].astype(BF16)], m_ctx)
            dq, dkt, dv, rq, rk = _odd_prep(p, tables, jnp.tile(da_qk_q[j], 2)[None], jnp.tile(da_qk_k[j], 2)[None])
            lam_init = 0.8 - 0.6 * math.exp(-0.3 * l)
            lf = da_lambda[j].astype(F32)
            lam = jnp.exp(jnp.sum(lf[0] * lf[1])) - jnp.exp(jnp.sum(lf[2] * lf[3])) + lam_init
            bound = 1.02 * DA_DIM ** 0.5 * jnp.max(jnp.abs(da_qk_q[j])) * jnp.max(jnp.abs(da_qk_k[j]))
            att = _by_score_bound(
                bound, functools.partial(_diff_attn, m_ctx=m_ctx, out_scale=1.0 - lam_init),
                dq, dkt, dv, jnp.full((1, DA_V), lam, F32), da_subln[j][None])
            lg = jax.nn.log_sigmoid(rt_decay[j].astype(F32))
            lg = jnp.broadcast_to(lg[:, :, None, None], (2, HEADS, 8, LANES))
            r_f, r_b = _retention(rq, rk, p, (3 * HEADS * 2 * DA_DIM + 2 * HEADS * RT_K) // (HEADS * RT_V), lg, m_ctx)
            xt = _outproj(xt, att, r_f, r_b, p, 5, rt_norm[j][None], wo[:HG_W], wo[HG_W:], mod, m_ctx)
        xt = _mlp(xt, mod, nw[1:2], mlp_w1[l].astype(BF16), mlp_w2[l].astype(BF16), m_ctx)
    return xt[m_ctx:][None]
```

```python
import functools
import math

import numpy as np
import jax
import jax.numpy as jnp
from jax import lax
from jax.experimental import pallas as pl
from jax.experimental.pallas import tpu as pltpu

F32 = jnp.float32
BF16 = jnp.bfloat16

GRID_W = 64
ROPE_DIM = 64
ROPE_BASE = 10000.0
EPS = 1e-6
HEADS = 4
HG_DIM = 128
HG_W = HEADS * HG_DIM
MLA_NOPE = 128
MLA_ROPE = ROPE_DIM
MLA_V = 128
MLA_QK = MLA_NOPE + MLA_ROPE
MLA_Q_RANK = 384
MLA_KV_RANK = 256
DA_DIM = ROPE_DIM
DA_V = 2 * DA_DIM
RT_K = ROPE_DIM
RT_V = 128
MLA_IN = 768

LANES = 128
VMEM_LIMIT = 56 * 1024 * 1024
ROW_BLOCK = 1280
ATT_BQ = 256
ATT_BK = 4096
SCAN_CHUNK = 64
RET_CHUNK = 128
V_AUG = LANES + 16
SCORE_LIMIT = 40.0


def _cparams(sem):
    return pltpu.CompilerParams(dimension_semantics=sem, vmem_limit_bytes=VMEM_LIMIT)


def _row_block(t, target):
    best = None
    for r in range(LANES, min(t, target) + 1, LANES):
        if t % r == 0:
            best = r
    assert best is not None, t
    return best


def _dot(a, b):
    return jnp.dot(a, b, preferred_element_type=F32)


def _dot_nt(a, b):
    return lax.dot_general(a, b, (((1,), (1,)), ((), ())), preferred_element_type=F32)


def _dot_tn(a, b):
    return lax.dot_general(a, b, (((0,), (0,)), ((), ())), preferred_element_type=F32)


def _silu(x):
    return x * (1.0 / (1.0 + jnp.exp(-x)))


def _row_mod(mod_ref, k, row0, rows, m_ctx):
    r = row0 + lax.broadcasted_iota(jnp.int32, (rows, 1), 0)
    return jnp.where(r < m_ctx, mod_ref[0, k:k + 1, :], mod_ref[1, k:k + 1, :])


def _modnorm(x, nw, sc, sh):
    y = x * lax.rsqrt(jnp.mean(x * x, axis=-1, keepdims=True) + EPS)
    return y * nw * (1.0 + sc) + sh


def _ada_kernel(c_ref, w_ref, b_ref, o_ref):
    cv = c_ref[...]
    o_ref[0] = jnp.dot(_silu(cv), w_ref[0], precision=lax.Precision.HIGHEST,
                       preferred_element_type=F32) + b_ref[0]


def _ada_table(cvec, ada_w, ada_b):
    depth, d, d6 = ada_w.shape
    tn = d6 // 4
    return pl.pallas_call(
        _ada_kernel,
        grid=(depth, d6 // tn),
        in_specs=[pl.BlockSpec((8, d), lambda l, j: (0, 0)),
                  pl.BlockSpec((1, d, tn), lambda l, j: (l, 0, j)),
                  pl.BlockSpec((1, 1, tn), lambda l, j: (l, 0, j))],
        out_specs=pl.BlockSpec((1, 8, tn), lambda l, j: (l, 0, j)),
        out_shape=jax.ShapeDtypeStruct((depth, 8, d6), F32),
        compiler_params=_cparams(("parallel", "parallel")),
        name="ada_table",
    )(cvec, ada_w, ada_b.reshape(depth, 1, d6))


def _proj_kernel(*refs, n_w, m_ctx, rows):
    x_ref, mod_ref, nw_ref = refs[:3]
    w_refs = refs[3:3 + n_w]
    o_refs = refs[3 + n_w:]
    row0 = pl.program_id(0) * rows
    h = _modnorm(x_ref[...], nw_ref[...],
                 _row_mod(mod_ref, 1, row0, rows, m_ctx),
                 _row_mod(mod_ref, 0, row0, rows, m_ctx)).astype(BF16)
    for w_ref, o_ref in zip(w_refs, o_refs):
        o_ref[...] = _dot(h, w_ref[...])


def _proj(xt, mod, nw, weights, m_ctx):
    t, d = xt.shape
    rows = _row_block(t, ROW_BLOCK // 2)
    full = lambda a: pl.BlockSpec(a.shape, lambda i: (0,) * a.ndim)
    return pl.pallas_call(
        functools.partial(_proj_kernel, n_w=len(weights), m_ctx=m_ctx, rows=rows),
        grid=(t // rows,),
        in_specs=[pl.BlockSpec((rows, d), lambda i: (i, 0)), full(mod), full(nw)]
                 + [full(w) for w in weights],
        out_specs=[pl.BlockSpec((rows, w.shape[1]), lambda i: (i, 0)) for w in weights],
        out_shape=[jax.ShapeDtypeStruct((t, w.shape[1]), F32) for w in weights],
        compiler_params=_cparams(("parallel",)),
        name="in_proj",
    )(xt, mod, nw, *weights)


def _hier_tables(c, reverse):
    levels = int(math.log2(c))
    assert 1 << levels == c
    cums = np.zeros(((levels + 1) * c, c), np.float32)
    roles = np.zeros((levels, c, LANES), np.float32)
    masks = np.zeros((levels + 1, c, c), np.float32)
    for li in range(levels):
        h = c >> (li + 1)
        for t in range(c):
            base = (t // (2 * h)) * 2 * h
            late = (t - base) >= h
            if not reverse:
                if late:
                    cums[li * c + t, base + h:t + 1] = 1.0
                else:
                    cums[li * c + t, t + 1:base + h] = 1.0
            else:
                if late:
                    cums[li * c + t, base + h:t] = 1.0
                else:
                    cums[li * c + t, t:base + h] = 1.0
            is_query = late != reverse
            roles[li, t, :] = 1.0 if is_query else 0.0
        for t in range(c):
            for s in range(c):
                same = (t // (2 * h)) == (s // (2 * h))
                if same and roles[li, t, 0] == 1.0 and roles[li, s, 0] == 0.0:
                    masks[li, t, s] = 1.0
    for t in range(c):
        if not reverse:
            cums[levels * c + t, :t + 1] = 1.0
        else:
            cums[levels * c + t, t:] = 1.0
    masks[levels] = np.eye(c, dtype=np.float32)
    return cums, roles, masks


def _split3(x):
    hi = x.astype(BF16)
    r1 = x - hi.astype(F32)
    mid = r1.astype(BF16)
    lo = (r1 - mid.astype(F32)).astype(BF16)
    return hi, mid, lo


def _gated_chunks(q_all, kk_all, v_all, logf_all, cum_ref, role_ref, mask_ref, st_ref, o_ref, c, reverse):
    levels = role_ref.shape[0]
    x_all = _dot(cum_ref[...], jnp.concatenate(_split3(logf_all), axis=0))
    for h in range(HEADS):
        sl = slice(h * HG_DIM, (h + 1) * HG_DIM)
        x, q, kk = x_all[:, sl], q_all[:, sl], kk_all[:, sl]
        run = x[levels * c:(levels + 1) * c]
        tot = run[0:1] if reverse else run[c - 1:c]
        a = _dot_nt(q.astype(BF16), kk.astype(BF16)) * mask_ref[levels]
        for li in range(levels):
            e = jnp.exp(x[li * c:(li + 1) * c])
            z = (jnp.where(role_ref[li] > 0.5, q, kk) * e).astype(BF16)
            a = a + _dot_nt(z, z) * mask_ref[li]
        st = st_ref[h]
        vb = v_all[:, sl].astype(BF16)
        o_ref[:, sl] = _dot(a.astype(BF16), vb) + _dot_nt((q * jnp.exp(run)).astype(BF16), st.astype(BF16))
        kd = (kk * jnp.exp(tot - run)).astype(BF16)
        st_ref[h] = st * jnp.exp(tot) + _dot_tn(vb, kd)


def _hgrn2_gate(z, log_lb, log1m_lb):
    soft = jnp.log1p(jnp.exp(-jnp.abs(z)))
    b = log1m_lb + (jnp.minimum(z, 0.0) - soft)
    logf = jnp.maximum(log_lb, b) + jnp.log1p(jnp.exp(-jnp.abs(log_lb - b)))
    return logf, jnp.exp(log1m_lb + (jnp.minimum(-z, 0.0) - soft))


def _hgrn2_kernel(qf_ref, zf_ref, vf_ref, qb_ref, zb_ref, vb_ref, llb_ref, l1m_ref,
                  cumf_ref, rolef_ref, maskf_ref, cumb_ref, roleb_ref, maskb_ref,
                  of_ref, ob_ref, sf_ref, sb_ref, *, c):
    @pl.when(pl.program_id(0) == 0)
    def _():
        sf_ref[...] = jnp.zeros_like(sf_ref)
        sb_ref[...] = jnp.zeros_like(sb_ref)

    logf, kk = _hgrn2_gate(zf_ref[...], llb_ref[0], l1m_ref[0])
    _gated_chunks(_silu(qf_ref[...]), kk, vf_ref[...], logf,
                  cumf_ref, rolef_ref, maskf_ref, sf_ref, of_ref, c, False)
    logf, kk = _hgrn2_gate(zb_ref[...], llb_ref[1], l1m_ref[1])
    _gated_chunks(_silu(qb_ref[...]), kk, vb_ref[...], logf,
                  cumb_ref, roleb_ref, maskb_ref, sb_ref, ob_ref, c, True)


def _bwd_chunk(i, mc, steps):
    return jnp.where(i < mc, mc - 1 - i, steps - 1 - i + mc)


def _hgrn2(p_main, log_lb, log1m_lb, m_ctx):
    t = p_main.shape[0]
    c = SCAN_CHUNK
    steps, mc = t // c, m_ctx // c
    tabs = [jnp.asarray(a) for rev in (False, True) for a in _hier_tables(c, rev)]
    for k in (0, 3):
        tabs[k] = jnp.tile(tabs[k], (1, 3)).astype(BF16)
    fwd = lambda sec: pl.BlockSpec((c, HG_W), lambda i: (i, sec))
    bwd = lambda sec: pl.BlockSpec((c, HG_W), lambda i: (_bwd_chunk(i, mc, steps), sec))
    full = lambda a: pl.BlockSpec(a.shape, lambda i: (0,) * a.ndim)
    return pl.pallas_call(
        functools.partial(_hgrn2_kernel, c=c),
        grid=(steps,),
        in_specs=[fwd(0), fwd(1), fwd(3), bwd(0), bwd(2), bwd(3), full(log_lb), full(log1m_lb)]
                 + [full(a) for a in tabs],
        out_specs=[pl.BlockSpec((c, HG_W), lambda i: (i, 0)),
                   pl.BlockSpec((c, HG_W), lambda i: (_bwd_chunk(i, mc, steps), 0))],
        out_shape=[jax.ShapeDtypeStruct((t, HG_W), F32)] * 2,
        scratch_shapes=[pltpu.VMEM((HEADS, HG_DIM, HG_DIM), F32)] * 2,
        compiler_params=_cparams(("arbitrary",)),
        name="hgrn2_scan",
    )(p_main, p_main, p_main, p_main, p_main, p_main, log_lb, log1m_lb, *tabs)


def _ret_chunk(q, k, v, lg, st_ref, c, reverse):
    t = lax.broadcasted_iota(jnp.int32, (c, c), 0)
    s = lax.broadcasted_iota(jnp.int32, (c, c), 1)
    dlt = (s - t) if reverse else (t - s)
    dec = jnp.where(dlt >= 0, jnp.exp(lg[:, :c] * jnp.maximum(dlt, 0).astype(F32)), 0.0)
    r = lax.broadcasted_iota(jnp.int32, (c, RT_K), 0).astype(F32)
    lk = lg[:, :RT_K]
    qdec = jnp.exp(lk * ((c - r) if reverse else (r + 1.0)))
    kdec = jnp.exp(lk * (r if reverse else (c - 1.0 - r)))
    a = _dot_nt(q.astype(BF16), k.astype(BF16)) * dec
    st = st_ref[...]
    vb = v.astype(BF16)
    o = _dot(a.astype(BF16), vb) + _dot_nt((q * qdec).astype(BF16), st.astype(BF16))
    st_ref[...] = st * jnp.exp(lk * float(c)) + _dot_tn(vb, (k * kdec).astype(BF16))
    return o


def _ret_kernel(qf_ref, kf_ref, vf_ref, qb_ref, kb_ref, vb_ref, lg_ref,
                of_ref, ob_ref, sf_ref, sb_ref, *, c):
    @pl.when(pl.program_id(0) == 0)
    def _():
        sf_ref[...] = jnp.zeros_like(sf_ref)
        sb_ref[...] = jnp.zeros_like(sb_ref)

    for h in range(HEADS):
        sl = slice(h * RT_V, (h + 1) * RT_V)
        of_ref[:, sl] = _ret_chunk(qf_ref[h], kf_ref[h], vf_ref[:, sl], lg_ref[0, h, 0:1, :], sf_ref.at[h], c, False)
        ob_ref[:, sl] = _ret_chunk(qb_ref[h], kb_ref[h], vb_ref[:, sl], lg_ref[1, h, 0:1, :], sb_ref.at[h], c, True)


def _retention(rq, rk, p, v_block, lg, m_ctx):
    t = p.shape[0]
    c = RET_CHUNK
    steps, mc = t // c, m_ctx // c
    width = HEADS * RT_V
    fq = pl.BlockSpec((HEADS, c, RT_K), lambda i: (0, i, 0))
    bq = pl.BlockSpec((HEADS, c, RT_K), lambda i: (0, _bwd_chunk(i, mc, steps), 0))
    fv = pl.BlockSpec((c, width), lambda i: (i, v_block))
    bv = pl.BlockSpec((c, width), lambda i: (_bwd_chunk(i, mc, steps), v_block))
    return pl.pallas_call(
        functools.partial(_ret_kernel, c=c),
        grid=(steps,),
        in_specs=[fq, fq, fv, bq, bq, bv, pl.BlockSpec(lg.shape, lambda i: (0, 0, 0, 0))],
        out_specs=[pl.BlockSpec((c, width), lambda i: (i, 0)),
                   pl.BlockSpec((c, width), lambda i: (_bwd_chunk(i, mc, steps), 0))],
        out_shape=[jax.ShapeDtypeStruct((t, width), F32)] * 2,
        scratch_shapes=[pltpu.VMEM((HEADS, RT_V, RT_K), F32)] * 2,
        compiler_params=_cparams(("arbitrary",)),
        name="retention_scan",
    )(rq, rk, p, rq, rk, p, lg)


def _rope128(y, cos, sin_a, sin_b):
    return y * cos + pltpu.roll(y, LANES - 16, 1) * sin_a + pltpu.roll(y, 16, 1) * sin_b


def _rope_tables(n, m_ctx):
    rows = n // GRID_W
    row = jnp.repeat(jnp.arange(rows, dtype=F32), GRID_W)
    col = jnp.tile(jnp.arange(GRID_W, dtype=F32), rows)
    quarter = ROPE_DIM // 4
    inv_freq = ROPE_BASE ** (-jnp.arange(quarter, dtype=F32) / quarter)
    ang_r = row[:, None] * inv_freq
    ang_c = col[:, None] * inv_freq
    ang = jnp.concatenate([ang_r, ang_r, ang_c, ang_c], axis=-1)
    cos, sin = jnp.cos(ang), jnp.sin(ang)
    first = (jnp.arange(ROPE_DIM) % 32) < 16
    sin_a = jnp.where(first, -sin, 0.0)
    sin_b = jnp.where(first, 0.0, sin)
    pad = lambda a, v: jnp.tile(jnp.concatenate([jnp.full((m_ctx, ROPE_DIM), v, F32), a], axis=0), (1, 2))
    return pad(cos, 1.0), pad(sin_a, 0.0), pad(sin_b, 0.0)


def _half_sums(sq):
    low = lax.broadcasted_iota(jnp.int32, sq.shape, 1) < 64
    lo = jnp.sum(jnp.where(low, sq, 0.0), axis=-1, keepdims=True)
    return low, lo, jnp.sum(sq, axis=-1, keepdims=True) - lo


def _mla_prep_kernel(p_ref, cos_ref, sa_ref, sb_ref, qn_ref, kvn_ref, wuq_ref, wukv_ref,
                     gq_ref, gk_ref, qt_ref, k_ref, vt_ref):
    p = p_ref[...]
    cos, sa, sb = cos_ref[...], sa_ref[...], sb_ref[...]
    scale = MLA_QK ** -0.5

    def rms(x, w):
        return x * lax.rsqrt(jnp.mean(x * x, axis=-1, keepdims=True) + EPS) * w

    qu = _dot(rms(p[:, :MLA_Q_RANK], qn_ref[...]).astype(BF16), wuq_ref[...])
    kv = _dot(rms(p[:, MLA_Q_RANK:MLA_Q_RANK + MLA_KV_RANK], kvn_ref[...]).astype(BF16),
              wukv_ref[...])
    kr = p[:, MLA_Q_RANK + MLA_KV_RANK:]
    ss_kr = jnp.sum(kr * kr, axis=-1, keepdims=True)
    gq, gk = gq_ref[...], gk_ref[...]
    for b in range(HEADS // 2):
        qr = qu[:, HEADS * MLA_NOPE + b * LANES:HEADS * MLA_NOPE + (b + 1) * LANES]
        low, ss_lo, ss_hi = _half_sums(qr * qr)
        inv = []
        for j in range(2):
            h = 2 * b + j
            qn = qu[:, h * MLA_NOPE:(h + 1) * MLA_NOPE]
            ss = jnp.sum(qn * qn, axis=-1, keepdims=True) + (ss_lo, ss_hi)[j]
            inv.append(lax.rsqrt(ss * (1.0 / MLA_QK) + EPS))
            qt_ref[h, 0:MLA_NOPE, :] = (qn * inv[j] * gq[0:1] * scale).T.astype(BF16)
        yrt = (_rope128(qr * jnp.where(low, inv[0], inv[1]) * gq[1:2], cos, sa, sb) * scale).T
        qt_ref[2 * b, MLA_NOPE:MLA_QK, :] = yrt[:MLA_ROPE].astype(BF16)
        qt_ref[2 * b + 1, MLA_NOPE:MLA_QK, :] = yrt[MLA_ROPE:].astype(BF16)
    low = lax.broadcasted_iota(jnp.int32, kr.shape, 1) < 64
    gk_rope = jnp.where(low, gk[1:2], 0.0)
    for h in range(HEADS):
        kn = kv[:, h * MLA_NOPE:(h + 1) * MLA_NOPE]
        inv = lax.rsqrt((jnp.sum(kn * kn, axis=-1, keepdims=True) + ss_kr) * (1.0 / MLA_QK) + EPS)
        k_ref[h, :, 0:MLA_NOPE] = (kn * inv * gk[0:1]).astype(BF16)
        yr = _rope128(kr * inv * gk_rope, cos, sa, sb)
        k_ref[h, :, MLA_NOPE:MLA_QK] = yr[:, :MLA_ROPE].astype(BF16)
        vt_ref[h, 0:MLA_V, :] = kv[:, HEADS * MLA_NOPE + h * MLA_V:HEADS * MLA_NOPE + (h + 1) * MLA_V].T.astype(BF16)
        vt_ref[h, MLA_V:, :] = _ones_row(kr.shape[0])


def _mla_prep(p_mla, tables, qn, kvn, wuq, wukv, gq, gk):
    t = p_mla.shape[0]
    rows = _row_block(t, ROW_BLOCK // 2)
    full = lambda a: pl.BlockSpec(a.shape, lambda i: (0,) * a.ndim)
    rowb = lambda w: pl.BlockSpec((rows, w), lambda i: (i, 0))
    args = (qn, kvn, wuq, wukv, gq, gk)
    return pl.pallas_call(
        _mla_prep_kernel,
        grid=(t // rows,),
        in_specs=[rowb(MLA_IN), rowb(LANES), rowb(LANES), rowb(LANES)] + [full(a) for a in args],
        out_specs=[pl.BlockSpec((HEADS, MLA_QK, rows), lambda i: (0, 0, i)),
                   pl.BlockSpec((HEADS, rows, MLA_QK), lambda i: (0, i, 0)),
                   pl.BlockSpec((HEADS, V_AUG, rows), lambda i: (0, 0, i))],
        out_shape=[jax.ShapeDtypeStruct((HEADS, MLA_QK, t), BF16),
                   jax.ShapeDtypeStruct((HEADS, t, MLA_QK), BF16),
                   jax.ShapeDtypeStruct((HEADS, V_AUG, t), BF16)],
        compiler_params=_cparams(("parallel",)),
        name="mla_prep",
    )(p_mla, *tables, *args)


def _ones_row(cols):
    return (lax.broadcasted_iota(jnp.int32, (V_AUG - LANES, cols), 0) == 0).astype(BF16)


def _attend(items, vt_ref, first):
    scores = [_dot(k_ref[0, rows, :], qt) for qt, k_ref, _, _, rows in items]
    for (_, _, acc_sc, m_sc, rows), s in zip(items, scores):
        vt = vt_ref[0, :, rows]
        if m_sc is None:
            pv = _dot(vt, jnp.exp(s).astype(BF16))
            acc_sc[...] = pv if first else acc_sc[...] + pv
        else:
            m_new = jnp.max(s, axis=0, keepdims=True)
            if not first:
                m_prev = m_sc[...]
                m_new = jnp.maximum(m_prev, m_new)
            pv = _dot(vt, jnp.exp(s - m_new).astype(BF16))
            acc_sc[...] = pv if first else jnp.exp(m_prev - m_new) * acc_sc[...] + pv
            m_sc[...] = m_new


def _attend_all(qts, k_refs, vt_ref, accs, ms, m_ctx, bq, bk, n_blocks):
    pairs = list(zip(qts, k_refs, accs, ms))
    _attend([p + (slice(0, m_ctx),) for p in pairs], vt_ref, True)
    split = max(1, 2 // len(pairs))
    sub = bk // split

    @pl.when(pl.program_id(1) >= m_ctx // bq)
    def _():
        def body(j, carry):
            items = [p + (pl.ds(pl.multiple_of(m_ctx + j * bk + u * sub, LANES), sub),)
                     for p in pairs for u in range(split)]
            _attend(items, vt_ref, False)
            return carry
        lax.fori_loop(0, n_blocks, body, 0, unroll=True)


def _normalised(acc_sc):
    acc = acc_sc[...]
    return (acc[:LANES] / acc[LANES:LANES + 1]).T


def _mla_attn_kernel(qt_ref, k_ref, vt_ref, o_ref, acc_sc, *m_sc, m_ctx, bq, bk, n_blocks):
    m = m_sc[0] if m_sc else None
    _attend_all([qt_ref[0]], [k_ref], vt_ref, [acc_sc], [m], m_ctx, bq, bk, n_blocks)
    o_ref[...] = _normalised(acc_sc)


def _attn_blocks(t, m_ctx):
    bq = ATT_BQ
    assert m_ctx % bq == 0 and t % bq == 0
    n_lat = t - m_ctx
    bk = _row_block(n_lat, ATT_BK)
    return bq, bk, n_lat // bk


def _attn_scratch(bq, n_pairs, shifted):
    return [pltpu.VMEM((V_AUG, bq), F32)] * n_pairs + ([pltpu.VMEM((1, bq), F32)] * n_pairs if shifted else [])


def _by_score_bound(bound, attend, *operands):
    return lax.cond(bound <= SCORE_LIMIT,
                    functools.partial(attend, shifted=False),
                    functools.partial(attend, shifted=True), *operands)


def _mla_attn(qt, k, vt, m_ctx, shifted):
    t = k.shape[1]
    bq, bk, n_blocks = _attn_blocks(t, m_ctx)
    return pl.pallas_call(
        functools.partial(_mla_attn_kernel, m_ctx=m_ctx, bq=bq, bk=bk, n_blocks=n_blocks),
        grid=(HEADS, t // bq),
        in_specs=[pl.BlockSpec((1, MLA_QK, bq), lambda h, i: (h, 0, i)),
                  pl.BlockSpec((1, t, MLA_QK), lambda h, i: (h, 0, 0)),
                  pl.BlockSpec((1, V_AUG, t), lambda h, i: (h, 0, 0))],
        out_specs=pl.BlockSpec((bq, MLA_V), lambda h, i: (i, h)),
        out_shape=jax.ShapeDtypeStruct((t, HEADS * MLA_V), F32),
        scratch_shapes=_attn_scratch(bq, 1, shifted),
        compiler_params=_cparams(("parallel", "arbitrary")),
        name="mla_attn_shifted" if shifted else "mla_attn",
    )(qt, k, vt)


def _diff_attn_kernel(q1_ref, q2_ref, k1_ref, k2_ref, v_ref, lam_ref, sub_ref, o_ref, a1, a2, *m_sc,
                      m_ctx, bq, bk, n_blocks, out_scale):
    ms = list(m_sc) if m_sc else [None, None]
    _attend_all([q1_ref[0], q2_ref[0]], [k1_ref, k2_ref], v_ref, [a1, a2], ms, m_ctx, bq, bk, n_blocks)
    d = _normalised(a1) - lam_ref[...] * _normalised(a2)
    y = d * lax.rsqrt(jnp.mean(d * d, axis=-1, keepdims=True) + EPS)
    o_ref[...] = y * sub_ref[...] * out_scale


def _diff_attn(dqt, dk, dvt, lam, subln, m_ctx, out_scale, shifted):
    t = dk.shape[1]
    bq, bk, n_blocks = _attn_blocks(t, m_ctx)
    qspec = lambda c: pl.BlockSpec((1, DA_DIM, bq), lambda h, i: (2 * h + c, 0, i))
    kspec = lambda c: pl.BlockSpec((1, t, DA_DIM), lambda h, i: (2 * h + c, 0, 0))
    vec = pl.BlockSpec((1, DA_V), lambda h, i: (0, 0))
    return pl.pallas_call(
        functools.partial(_diff_attn_kernel, m_ctx=m_ctx, bq=bq, bk=bk, n_blocks=n_blocks,
                          out_scale=out_scale),
        grid=(HEADS, t // bq),
        in_specs=[qspec(0), qspec(1), kspec(0), kspec(1),
                  pl.BlockSpec((1, V_AUG, t), lambda h, i: (h, 0, 0)), vec, vec],
        out_specs=pl.BlockSpec((bq, DA_V), lambda h, i: (i, h)),
        out_shape=jax.ShapeDtypeStruct((t, HEADS * DA_V), F32),
        scratch_shapes=_attn_scratch(bq, 2, shifted),
        compiler_params=_cparams(("parallel", "arbitrary")),
        name="diff_attn_shifted" if shifted else "diff_attn",
    )(dqt, dqt, dk, dk, dvt, lam, subln)


def _odd_prep_kernel(p_ref, cos_ref, sa_ref, sb_ref, gq_ref, gk_ref,
                     dqt_ref, dk_ref, dvt_ref, rq_ref, rk_ref):
    cos, sa, sb = cos_ref[...], sa_ref[...], sb_ref[...]
    da_w = HEADS * 2 * DA_DIM

    def sub_rms(x, g):
        low, ss_lo, ss_hi = _half_sums(x * x)
        inv = jnp.where(low, lax.rsqrt(ss_lo * (1.0 / DA_DIM) + EPS), lax.rsqrt(ss_hi * (1.0 / DA_DIM) + EPS))
        return x * inv * g

    for h in range(HEADS):
        col = h * LANES
        yqt = (_rope128(sub_rms(p_ref[:, col:col + LANES], gq_ref[...]), cos, sa, sb) * (DA_DIM ** -0.5)).T
        dqt_ref[2 * h] = yqt[:DA_DIM].astype(BF16)
        dqt_ref[2 * h + 1] = yqt[DA_DIM:].astype(BF16)
        yk = _rope128(sub_rms(p_ref[:, da_w + col:da_w + col + LANES], gk_ref[...]), cos, sa, sb)
        dk_ref[2 * h] = yk[:, :DA_DIM].astype(BF16)
        dk_ref[2 * h + 1] = yk[:, DA_DIM:].astype(BF16)
        dvt_ref[h, 0:DA_V, :] = p_ref[:, 2 * da_w + col:2 * da_w + col + LANES].T.astype(BF16)
        dvt_ref[h, DA_V:, :] = _ones_row(cos.shape[0])
    r0 = 2 * da_w + HEADS * DA_V
    for b in range(HEADS // 2):
        col = r0 + b * LANES
        yq = _rope128(p_ref[:, col:col + LANES], cos, sa, sb)
        rq_ref[2 * b] = yq[:, :RT_K]
        rq_ref[2 * b + 1] = yq[:, RT_K:]
        col = r0 + HEADS * RT_K + b * LANES
        yk = _rope128(p_ref[:, col:col + LANES] * (RT_K ** -0.5), cos, sa, sb)
        rk_ref[2 * b] = yk[:, :RT_K]
        rk_ref[2 * b + 1] = yk[:, RT_K:]


def _odd_prep(p, tables, gq, gk):
    t = p.shape[0]
    rows = _row_block(t, ROW_BLOCK // 2)
    width = 3 * HEADS * 2 * DA_DIM + 2 * HEADS * RT_K
    full = lambda a: pl.BlockSpec(a.shape, lambda i: (0,) * a.ndim)
    rowb = lambda w: pl.BlockSpec((rows, w), lambda i: (i, 0))
    hm = lambda n, w: pl.BlockSpec((n, rows, w), lambda i: (0, i, 0))
    return pl.pallas_call(
        _odd_prep_kernel,
        grid=(t // rows,),
        in_specs=[rowb(width), rowb(LANES), rowb(LANES), rowb(LANES), full(gq), full(gk)],
        out_specs=[pl.BlockSpec((2 * HEADS, DA_DIM, rows), lambda i: (0, 0, i)),
                   hm(2 * HEADS, DA_DIM),
                   pl.BlockSpec((HEADS, V_AUG, rows), lambda i: (0, 0, i)), hm(HEADS, RT_K), hm(HEADS, RT_K)],
        out_shape=[jax.ShapeDtypeStruct((2 * HEADS, DA_DIM, t), BF16),
                   jax.ShapeDtypeStruct((2 * HEADS, t, DA_DIM), BF16),
                   jax.ShapeDtypeStruct((HEADS, V_AUG, t), BF16),
                   jax.ShapeDtypeStruct((HEADS, t, RT_K), F32),
                   jax.ShapeDtypeStruct((HEADS, t, RT_K), F32)],
        compiler_params=_cparams(("parallel",)),
        name="odd_prep",
    )(p, *tables, gq, gk)


def _outproj_kernel(x_ref, att_ref, of_ref, ob_ref, gate_ref, nrm_ref, wa_ref, wr_ref, mod_ref, o_ref,
                    *, m_ctx, rows):
    o = of_ref[...] + ob_ref[...]
    gate = gate_ref[...]
    rec = []
    for h in range(HEADS):
        oh = o[:, h * LANES:(h + 1) * LANES]
        inv = lax.rsqrt(jnp.mean(oh * oh, axis=-1, keepdims=True) + EPS)
        rec.append(oh * inv * nrm_ref[...] * _silu(gate[:, h * LANES:(h + 1) * LANES]))
    rec = jnp.concatenate(rec, axis=-1).astype(BF16)
    y = _dot(att_ref[...].astype(BF16), wa_ref[...]) + _dot(rec, wr_ref[...])
    g1 = _row_mod(mod_ref, 2, pl.program_id(0) * rows, rows, m_ctx)
    o_ref[...] = x_ref[...] + g1 * y


def _outproj(xt, att, o_f, o_b, gate_arr, gate_block, nrm, w_att, w_rec, mod, m_ctx):
    t, d = xt.shape
    rows = _row_block(t, ROW_BLOCK // 2)
    w = att.shape[1]
    full = lambda a: pl.BlockSpec(a.shape, lambda i: (0,) * a.ndim)
    rowb = lambda c: pl.BlockSpec((rows, c), lambda i: (i, 0))
    return pl.pallas_call(
        functools.partial(_outproj_kernel, m_ctx=m_ctx, rows=rows),
        grid=(t // rows,),
        in_specs=[rowb(d), rowb(w), rowb(w), rowb(w),
                  pl.BlockSpec((rows, w), lambda i: (i, gate_block)),
                  full(nrm), full(w_att), full(w_rec), full(mod)],
        out_specs=rowb(d),
        out_shape=jax.ShapeDtypeStruct((t, d), F32),
        compiler_params=_cparams(("parallel",)),
        name="out_proj",
    )(xt, att, o_f, o_b, gate_arr, nrm, w_att, w_rec, mod)


def _mlp_kernel(x_ref, mod_ref, nw_ref, w1_ref, w2_ref, o_ref, h_sc, acc_sc, *, m_ctx, rows):
    j = pl.program_id(1)
    row0 = pl.program_id(0) * rows

    @pl.when(j == 0)
    def _():
        h_sc[...] = _modnorm(x_ref[...], nw_ref[...],
                             _row_mod(mod_ref, 4, row0, rows, m_ctx),
                             _row_mod(mod_ref, 3, row0, rows, m_ctx)).astype(BF16)
        acc_sc[...] = jnp.zeros_like(acc_sc)

    u = jnp.maximum(_dot(h_sc[...], w1_ref[...]), 0.0)
    acc_sc[...] += _dot((u * u).astype(BF16), w2_ref[...])

    @pl.when(j == pl.num_programs(1) - 1)
    def _():
        o_ref[...] = x_ref[...] + _row_mod(mod_ref, 5, row0, rows, m_ctx) * acc_sc[...]


def _mlp(xt, mod, nw, w1, w2, m_ctx):
    t, d = xt.shape
    hid = w1.shape[1]
    rows = _row_block(t, ROW_BLOCK)
    th = 512
    full = lambda a: pl.BlockSpec(a.shape, lambda i, j: (0,) * a.ndim)
    return pl.pallas_call(
        functools.partial(_mlp_kernel, m_ctx=m_ctx, rows=rows),
        grid=(t // rows, hid // th),
        in_specs=[pl.BlockSpec((rows, d), lambda i, j: (i, 0)), full(mod), full(nw),
                  pl.BlockSpec((d, th), lambda i, j: (0, j)),
                  pl.BlockSpec((th, d), lambda i, j: (j, 0))],
        out_specs=pl.BlockSpec((rows, d), lambda i, j: (i, 0)),
        out_shape=jax.ShapeDtypeStruct((t, d), F32),
        scratch_shapes=[pltpu.VMEM((rows, d), BF16), pltpu.VMEM((rows, d), F32)],
        compiler_params=_cparams(("parallel", "arbitrary")),
        name="mlp",
    )(xt, mod, nw, w1, w2)


def _head_major(w, parts):
    k = w.shape[0]
    wh = w.reshape(k, HEADS, sum(parts))
    out, off = [], 0
    for width in parts:
        out.append(wh[:, :, off:off + width].reshape(k, HEADS * width))
        off += width
    return jnp.concatenate(out, axis=1)


def kernel(x, c, ctx, c_ctx, ada_w, ada_b, norm_w, w_o, mlp_w1, mlp_w2, a_w_in, hg_lb, hg_norm, mla_q_norm,
           mla_kv_norm, mla_w_uq, mla_w_ukv, mla_qk_q, mla_qk_k, c_w_in, da_lambda, da_qk_q, da_qk_k,
           da_subln, rt_decay, rt_norm):
    assert x.shape[0] == 1 and ctx.shape[0] == 1
    n, d = x.shape[1], x.shape[2]
    m_ctx = ctx.shape[1]
    depth = ada_w.shape[0]
    xt = jnp.concatenate([ctx[0], x[0]], axis=0)

    cvec = jnp.zeros((8, d), F32).at[0].set(c_ctx).at[1].set(c[0])
    mods = _ada_table(cvec, ada_w, ada_b)[:, :2].reshape(depth, 2, 6, d)
    tables = _rope_tables(n, m_ctx)

    lb = jnp.cumsum(jax.nn.softmax(hg_lb.astype(F32), axis=0), axis=0)
    lb = lb - lb[:1]
    log_lb = jnp.log(lb).reshape(-1, 2, 1, HG_W)
    log1m_lb = jnp.log1p(-lb).reshape(-1, 2, 1, HG_W)

    for l in range(depth):
        j = l // 2
        mod = mods[l]
        nw = norm_w[l]
        wo = w_o[l].astype(BF16)
        if l % 2 == 0:
            w_in = a_w_in[j]
            w_main = w_in[:, :5 * HG_W].astype(BF16)
            w_mla = jnp.pad(w_in[:, 5 * HG_W:], ((0, 0), (0, MLA_IN - (w_in.shape[1] - 5 * HG_W)))).astype(BF16)
            p_main, p_mla = _proj(xt, mod, nw[0:1], [w_main, w_mla], m_ctx)
            o_f, o_b = _hgrn2(p_main, log_lb[j], log1m_lb[j], m_ctx)
            gq = mla_qk_q[j]
            gk = mla_qk_k[j]
            q, kt, v = _mla_prep(
                p_mla, tables, mla_q_norm[j][None], mla_kv_norm[j][None],
                _head_major(mla_w_uq[j], (MLA_NOPE, MLA_ROPE)).astype(BF16),
                _head_major(mla_w_ukv[j], (MLA_NOPE, MLA_V)).astype(BF16),
                jnp.stack([gq[:MLA_NOPE], jnp.tile(gq[MLA_NOPE:], 2)]),
                jnp.stack([gk[:MLA_NOPE], jnp.tile(gk[MLA_NOPE:], 2)]))
            bound = 1.02 * MLA_QK ** 0.5 * jnp.max(jnp.abs(gq)) * jnp.max(jnp.abs(gk))
            att = _by_score_bound(bound, functools.partial(_mla_attn, m_ctx=m_ctx), q, kt, v)
            xt = _outproj(xt, att, o_f, o_b, p_main, 4, hg_norm[j][None], wo[HG_W:], wo[:HG_W], mod, m_ctx)
        else:
            (p,) = _proj(xt, mod, nw[0:1], [c_w_in[j].astype(BF16)], m_ctx)
            dq, dkt, dv, rq, rk = _odd_prep(p, tables, jnp.tile(da_qk_q[j], 2)[None], jnp.tile(da_qk_k[j], 2)[None])
            lam_init = 0.8 - 0.6 * math.exp(-0.3 * l)
            lf = da_lambda[j].astype(F32)
            lam = jnp.exp(jnp.sum(lf[0] * lf[1])) - jnp.exp(jnp.sum(lf[2] * lf[3])) + lam_init
            bound = 1.02 * DA_DIM ** 0.5 * jnp.max(jnp.abs(da_qk_q[j])) * jnp.max(jnp.abs(da_qk_k[j]))
            att = _by_score_bound(
                bound, functools.partial(_diff_attn, m_ctx=m_ctx, out_scale=1.0 - lam_init),
                dq, dkt, dv, jnp.full((1, DA_V), lam, F32), da_subln[j][None])
            lg = jax.nn.log_sigmoid(rt_decay[j].astype(F32))
            lg = jnp.broadcast_to(lg[:, :, None, None], (2, HEADS, 8, LANES))
            r_f, r_b = _retention(rq, rk, p, (3 * HEADS * 2 * DA_DIM + 2 * HEADS * RT_K) // (HEADS * RT_V), lg, m_ctx)
            xt = _outproj(xt, att, r_f, r_b, p, 5, rt_norm[j][None], wo[:HG_W], wo[HG_W:], mod, m_ctx)
        xt = _mlp(xt, mod, nw[1:2], mlp_w1[l].astype(BF16), mlp_w2[l].astype(BF16), m_ctx)
    return xt[m_ctx:][None]
```

```python
import functools
import math

import numpy as np
import jax
import jax.numpy as jnp
from jax import lax
from jax.experimental import pallas as pl
from jax.experimental.pallas import tpu as pltpu

F32 = jnp.float32
BF16 = jnp.bfloat16

GRID_W = 64
ROPE_DIM = 64
ROPE_BASE = 10000.0
EPS = 1e-6
HEADS = 4
HG_DIM = 128
HG_W = HEADS * HG_DIM
MLA_NOPE = 128
MLA_ROPE = ROPE_DIM
MLA_V = 128
MLA_QK = MLA_NOPE + MLA_ROPE
MLA_Q_RANK = 384
MLA_KV_RANK = 256
DA_DIM = ROPE_DIM
DA_V = 2 * DA_DIM
RT_K = ROPE_DIM
RT_V = 128
MLA_IN = 768

LANES = 128
VMEM_LIMIT = 56 * 1024 * 1024
ROW_BLOCK = 1280
ATT_BQ = 256
ATT_BK = 4096
SCAN_CHUNK = 64
RET_CHUNK = 128
V_AUG = LANES + 16
SCORE_LIMIT = 40.0


def _cparams(sem):
    return pltpu.CompilerParams(dimension_semantics=sem, vmem_limit_bytes=VMEM_LIMIT)


def _row_block(t, target):
    best = None
    for r in range(LANES, min(t, target) + 1, LANES):
        if t % r == 0:
            best = r
    assert best is not None, t
    return best


def _dot(a, b):
    return jnp.dot(a, b, preferred_element_type=F32)


def _dot_nt(a, b):
    return lax.dot_general(a, b, (((1,), (1,)), ((), ())), preferred_element_type=F32)


def _dot_tn(a, b):
    return lax.dot_general(a, b, (((0,), (0,)), ((), ())), preferred_element_type=F32)


def _silu(x):
    return x * (1.0 / (1.0 + jnp.exp(-x)))


def _row_mod(mod_ref, k, row0, rows, m_ctx):
    r = row0 + lax.broadcasted_iota(jnp.int32, (rows, 1), 0)
    return jnp.where(r < m_ctx, mod_ref[0, k:k + 1, :], mod_ref[1, k:k + 1, :])


def _modnorm(x, nw, sc, sh):
    y = x * lax.rsqrt(jnp.mean(x * x, axis=-1, keepdims=True) + EPS)
    return y * nw * (1.0 + sc) + sh


def _ada_kernel(c_ref, w_ref, b_ref, o_ref):
    cv = c_ref[...]
    o_ref[0] = jnp.dot(_silu(cv), w_ref[0], precision=lax.Precision.HIGHEST,
                       preferred_element_type=F32) + b_ref[0]


def _ada_table(cvec, ada_w, ada_b):
    depth, d, d6 = ada_w.shape
    tn = d6 // 4
    return pl.pallas_call(
        _ada_kernel,
        grid=(depth, d6 // tn),
        in_specs=[pl.BlockSpec((8, d), lambda l, j: (0, 0)),
                  pl.BlockSpec((1, d, tn), lambda l, j: (l, 0, j)),
                  pl.BlockSpec((1, 1, tn), lambda l, j: (l, 0, j))],
        out_specs=pl.BlockSpec((1, 8, tn), lambda l, j: (l, 0, j)),
        out_shape=jax.ShapeDtypeStruct((depth, 8, d6), F32),
        compiler_params=_cparams(("parallel", "parallel")),
        name="ada_table",
    )(cvec, ada_w, ada_b.reshape(depth, 1, d6))


def _proj_kernel(*refs, n_w, m_ctx, rows):
    x_ref, mod_ref, nw_ref = refs[:3]
    w_refs = refs[3:3 + n_w]
    o_refs = refs[3 + n_w:]
    row0 = pl.program_id(0) * rows
    h = _modnorm(x_ref[...], nw_ref[...],
                 _row_mod(mod_ref, 1, row0, rows, m_ctx),
                 _row_mod(mod_ref, 0, row0, rows, m_ctx)).astype(BF16)
    for w_ref, o_ref in zip(w_refs, o_refs):
        o_ref[...] = _dot(h, w_ref[...])


def _proj(xt, mod, nw, weights, m_ctx):
    t, d = xt.shape
    rows = _row_block(t, ROW_BLOCK // 2)
    full = lambda a: pl.BlockSpec(a.shape, lambda i: (0,) * a.ndim)
    return pl.pallas_call(
        functools.partial(_proj_kernel, n_w=len(weights), m_ctx=m_ctx, rows=rows),
        grid=(t // rows,),
        in_specs=[pl.BlockSpec((rows, d), lambda i: (i, 0)), full(mod), full(nw)]
                 + [full(w) for w in weights],
        out_specs=[pl.BlockSpec((rows, w.shape[1]), lambda i: (i, 0)) for w in weights],
        out_shape=[jax.ShapeDtypeStruct((t, w.shape[1]), F32) for w in weights],
        compiler_params=_cparams(("parallel",)),
        name="in_proj",
    )(xt, mod, nw, *weights)


def _hier_tables(c, reverse):
    levels = int(math.log2(c))
    assert 1 << levels == c
    cums = np.zeros(((levels + 1) * c, c), np.float32)
    roles = np.zeros((levels, c, LANES), np.float32)
    masks = np.zeros((levels + 1, c, c), np.float32)
    for li in range(levels):
        h = c >> (li + 1)
        for t in range(c):
            base = (t // (2 * h)) * 2 * h
            late = (t - base) >= h
            if not reverse:
                if late:
                    cums[li * c + t, base + h:t + 1] = 1.0
                else:
                    cums[li * c + t, t + 1:base + h] = 1.0
            else:
                if late:
                    cums[li * c + t, base + h:t] = 1.0
                else:
                    cums[li * c + t, t:base + h] = 1.0
            is_query = late != reverse
            roles[li, t, :] = 1.0 if is_query else 0.0
        for t in range(c):
            for s in range(c):
                same = (t // (2 * h)) == (s // (2 * h))
                if same and roles[li, t, 0] == 1.0 and roles[li, s, 0] == 0.0:
                    masks[li, t, s] = 1.0
    for t in range(c):
        if not reverse:
            cums[levels * c + t, :t + 1] = 1.0
        else:
            cums[levels * c + t, t:] = 1.0
    masks[levels] = np.eye(c, dtype=np.float32)
    return cums, roles, masks


def _split3(x):
    hi = x.astype(BF16)
    r1 = x - hi.astype(F32)
    mid = r1.astype(BF16)
    lo = (r1 - mid.astype(F32)).astype(BF16)
    return hi, mid, lo


def _gated_chunks(sides, c):
    levels = sides[0][5].shape[0]
    chains = []
    for q_all, kk_all, v_all, logf_all, cum_ref, role_ref, mask_ref, st_ref, o_ref, reverse in sides:
        x_all = _dot(cum_ref[...], jnp.concatenate(_split3(logf_all), axis=0))
        for h in range(HEADS):
            sl = slice(h * HG_DIM, (h + 1) * HG_DIM)
            chains.append((x_all[:, sl], q_all[:, sl], kk_all[:, sl], v_all[:, sl].astype(BF16),
                           role_ref, mask_ref, st_ref, o_ref, reverse, h, sl))
    acc = [_dot_nt(q.astype(BF16), kk.astype(BF16)) * mask_ref[levels]
           for _, q, kk, _, _, mask_ref, *_ in chains]
    for li in range(levels):
        for n, (x, q, kk, _, role_ref, mask_ref, *_) in enumerate(chains):
            z = (jnp.where(role_ref[li] > 0.5, q, kk) * jnp.exp(x[li * c:(li + 1) * c])).astype(BF16)
            acc[n] = acc[n] + _dot_nt(z, z) * mask_ref[li]
    for a, (x, q, kk, vb, _, _, st_ref, o_ref, reverse, h, sl) in zip(acc, chains):
        run = x[levels * c:(levels + 1) * c]
        tot = run[0:1] if reverse else run[c - 1:c]
        st = st_ref[h]
        o_ref[:, sl] = _dot(a.astype(BF16), vb) + _dot_nt((q * jnp.exp(run)).astype(BF16), st.astype(BF16))
        kd = (kk * jnp.exp(tot - run)).astype(BF16)
        st_ref[h] = st * jnp.exp(tot) + _dot_tn(vb, kd)


def _hgrn2_gate(z, log_lb, log1m_lb):
    soft = jnp.log1p(jnp.exp(-jnp.abs(z)))
    b = log1m_lb + (jnp.minimum(z, 0.0) - soft)
    logf = jnp.maximum(log_lb, b) + jnp.log1p(jnp.exp(-jnp.abs(log_lb - b)))
    return logf, jnp.exp(log1m_lb + (jnp.minimum(-z, 0.0) - soft))


def _hgrn2_kernel(qf_ref, zf_ref, vf_ref, qb_ref, zb_ref, vb_ref, llb_ref, l1m_ref,
                  cumf_ref, rolef_ref, maskf_ref, cumb_ref, roleb_ref, maskb_ref,
                  of_ref, ob_ref, sf_ref, sb_ref, *, c):
    @pl.when(pl.program_id(0) == 0)
    def _():
        sf_ref[...] = jnp.zeros_like(sf_ref)
        sb_ref[...] = jnp.zeros_like(sb_ref)

    logf_f, kk_f = _hgrn2_gate(zf_ref[...], llb_ref[0], l1m_ref[0])
    logf_b, kk_b = _hgrn2_gate(zb_ref[...], llb_ref[1], l1m_ref[1])
    _gated_chunks(
        [(_silu(qf_ref[...]), kk_f, vf_ref[...], logf_f, cumf_ref, rolef_ref, maskf_ref, sf_ref, of_ref, False)], c)
    _gated_chunks(
        [(_silu(qb_ref[...]), kk_b, vb_ref[...], logf_b, cumb_ref, roleb_ref, maskb_ref, sb_ref, ob_ref, True)], c)


def _bwd_chunk(i, mc, steps):
    return jnp.where(i < mc, mc - 1 - i, steps - 1 - i + mc)


def _hgrn2(p_main, log_lb, log1m_lb, m_ctx):
    t = p_main.shape[0]
    c = SCAN_CHUNK
    steps, mc = t // c, m_ctx // c
    tabs = [jnp.asarray(a) for rev in (False, True) for a in _hier_tables(c, rev)]
    for k in (0, 3):
        tabs[k] = jnp.tile(tabs[k], (1, 3)).astype(BF16)
    fwd = lambda sec: pl.BlockSpec((c, HG_W), lambda i: (i, sec))
    bwd = lambda sec: pl.BlockSpec((c, HG_W), lambda i: (_bwd_chunk(i, mc, steps), sec))
    full = lambda a: pl.BlockSpec(a.shape, lambda i: (0,) * a.ndim)
    return pl.pallas_call(
        functools.partial(_hgrn2_kernel, c=c),
        grid=(steps,),
        in_specs=[fwd(0), fwd(1), fwd(3), bwd(0), bwd(2), bwd(3), full(log_lb), full(log1m_lb)]
                 + [full(a) for a in tabs],
        out_specs=[pl.BlockSpec((c, HG_W), lambda i: (i, 0)),
                   pl.BlockSpec((c, HG_W), lambda i: (_bwd_chunk(i, mc, steps), 0))],
        out_shape=[jax.ShapeDtypeStruct((t, HG_W), F32)] * 2,
        scratch_shapes=[pltpu.VMEM((HEADS, HG_DIM, HG_DIM), F32)] * 2,
        compiler_params=_cparams(("arbitrary",)),
        name="hgrn2_scan",
    )(p_main, p_main, p_main, p_main, p_main, p_main, log_lb, log1m_lb, *tabs)


def _ret_scores(q, k, lg, c, reverse):
    t = lax.broadcasted_iota(jnp.int32, (c, c), 0)
    s = lax.broadcasted_iota(jnp.int32, (c, c), 1)
    dlt = (s - t) if reverse else (t - s)
    dec = jnp.where(dlt >= 0, jnp.exp(lg[:, :c] * jnp.maximum(dlt, 0).astype(F32)), 0.0)
    return (_dot_nt(q.astype(BF16), k.astype(BF16)) * dec).astype(BF16)


def _ret_finish(a, q, k, v, lg, st_ref, c, reverse):
    r = lax.broadcasted_iota(jnp.int32, (c, RT_K), 0).astype(F32)
    lk = lg[:, :RT_K]
    qdec = jnp.exp(lk * ((c - r) if reverse else (r + 1.0)))
    kdec = jnp.exp(lk * (r if reverse else (c - 1.0 - r)))
    st = st_ref[...]
    vb = v.astype(BF16)
    o = _dot(a, vb) + _dot_nt((q * qdec).astype(BF16), st.astype(BF16))
    st_ref[...] = st * jnp.exp(lk * float(c)) + _dot_tn(vb, (k * kdec).astype(BF16))
    return o


def _ret_kernel(qf_ref, kf_ref, vf_ref, qb_ref, kb_ref, vb_ref, lg_ref,
                of_ref, ob_ref, sf_ref, sb_ref, *, c):
    @pl.when(pl.program_id(0) == 0)
    def _():
        sf_ref[...] = jnp.zeros_like(sf_ref)
        sb_ref[...] = jnp.zeros_like(sb_ref)

    chains = [(refs, d, h) for h in range(HEADS)
              for d, refs in enumerate(((qf_ref, kf_ref, vf_ref, of_ref, sf_ref), (qb_ref, kb_ref, vb_ref, ob_ref, sb_ref)))]
    scores = [_ret_scores(q_ref[h], k_ref[h], lg_ref[d, h, 0:1, :], c, d == 1)
              for (q_ref, k_ref, _, _, _), d, h in chains]
    for a, ((q_ref, k_ref, v_ref, o_ref, s_ref), d, h) in zip(scores, chains):
        sl = slice(h * RT_V, (h + 1) * RT_V)
        o_ref[:, sl] = _ret_finish(a, q_ref[h], k_ref[h], v_ref[:, sl], lg_ref[d, h, 0:1, :], s_ref.at[h], c, d == 1)


def _retention(rq, rk, p, v_block, lg, m_ctx):
    t = p.shape[0]
    c = RET_CHUNK
    steps, mc = t // c, m_ctx // c
    width = HEADS * RT_V
    fq = pl.BlockSpec((HEADS, c, RT_K), lambda i: (0, i, 0))
    bq = pl.BlockSpec((HEADS, c, RT_K), lambda i: (0, _bwd_chunk(i, mc, steps), 0))
    fv = pl.BlockSpec((c, width), lambda i: (i, v_block))
    bv = pl.BlockSpec((c, width), lambda i: (_bwd_chunk(i, mc, steps), v_block))
    return pl.pallas_call(
        functools.partial(_ret_kernel, c=c),
        grid=(steps,),
        in_specs=[fq, fq, fv, bq, bq, bv, pl.BlockSpec(lg.shape, lambda i: (0, 0, 0, 0))],
        out_specs=[pl.BlockSpec((c, width), lambda i: (i, 0)),
                   pl.BlockSpec((c, width), lambda i: (_bwd_chunk(i, mc, steps), 0))],
        out_shape=[jax.ShapeDtypeStruct((t, width), F32)] * 2,
        scratch_shapes=[pltpu.VMEM((HEADS, RT_V, RT_K), F32)] * 2,
        compiler_params=_cparams(("arbitrary",)),
        name="retention_scan",
    )(rq, rk, p, rq, rk, p, lg)


def _rope128(y, cos, sin_a, sin_b):
    return y * cos + pltpu.roll(y, LANES - 16, 1) * sin_a + pltpu.roll(y, 16, 1) * sin_b


def _rope_tables(n, m_ctx):
    rows = n // GRID_W
    row = jnp.repeat(jnp.arange(rows, dtype=F32), GRID_W)
    col = jnp.tile(jnp.arange(GRID_W, dtype=F32), rows)
    quarter = ROPE_DIM // 4
    inv_freq = ROPE_BASE ** (-jnp.arange(quarter, dtype=F32) / quarter)
    ang_r = row[:, None] * inv_freq
    ang_c = col[:, None] * inv_freq
    ang = jnp.concatenate([ang_r, ang_r, ang_c, ang_c], axis=-1)
    cos, sin = jnp.cos(ang), jnp.sin(ang)
    first = (jnp.arange(ROPE_DIM) % 32) < 16
    sin_a = jnp.where(first, -sin, 0.0)
    sin_b = jnp.where(first, 0.0, sin)
    pad = lambda a, v: jnp.tile(jnp.concatenate([jnp.full((m_ctx, ROPE_DIM), v, F32), a], axis=0), (1, 2))
    return pad(cos, 1.0), pad(sin_a, 0.0), pad(sin_b, 0.0)


def _half_sums(sq):
    low = lax.broadcasted_iota(jnp.int32, sq.shape, 1) < 64
    lo = jnp.sum(jnp.where(low, sq, 0.0), axis=-1, keepdims=True)
    return low, lo, jnp.sum(sq, axis=-1, keepdims=True) - lo


def _mla_prep_kernel(p_ref, cos_ref, sa_ref, sb_ref, qn_ref, kvn_ref, wuq_ref, wukv_ref,
                     gq_ref, gk_ref, qt_ref, k_ref, vt_ref):
    p = p_ref[...]
    cos, sa, sb = cos_ref[...], sa_ref[...], sb_ref[...]
    scale = MLA_QK ** -0.5

    def rms(x, w):
        return x * lax.rsqrt(jnp.mean(x * x, axis=-1, keepdims=True) + EPS) * w

    qu = _dot(rms(p[:, :MLA_Q_RANK], qn_ref[...]).astype(BF16), wuq_ref[...])
    kv = _dot(rms(p[:, MLA_Q_RANK:MLA_Q_RANK + MLA_KV_RANK], kvn_ref[...]).astype(BF16),
              wukv_ref[...])
    kr = p[:, MLA_Q_RANK + MLA_KV_RANK:]
    ss_kr = jnp.sum(kr * kr, axis=-1, keepdims=True)
    gq, gk = gq_ref[...], gk_ref[...]
    for b in range(HEADS // 2):
        qr = qu[:, HEADS * MLA_NOPE + b * LANES:HEADS * MLA_NOPE + (b + 1) * LANES]
        low, ss_lo, ss_hi = _half_sums(qr * qr)
        inv = []
        for j in range(2):
            h = 2 * b + j
            qn = qu[:, h * MLA_NOPE:(h + 1) * MLA_NOPE]
            ss = jnp.sum(qn * qn, axis=-1, keepdims=True) + (ss_lo, ss_hi)[j]
            inv.append(lax.rsqrt(ss * (1.0 / MLA_QK) + EPS))
            qt_ref[h, 0:MLA_NOPE, :] = (qn * inv[j] * gq[0:1] * scale).T.astype(BF16)
        yrt = (_rope128(qr * jnp.where(low, inv[0], inv[1]) * gq[1:2], cos, sa, sb) * scale).T
        qt_ref[2 * b, MLA_NOPE:MLA_QK, :] = yrt[:MLA_ROPE].astype(BF16)
        qt_ref[2 * b + 1, MLA_NOPE:MLA_QK, :] = yrt[MLA_ROPE:].astype(BF16)
    low = lax.broadcasted_iota(jnp.int32, kr.shape, 1) < 64
    gk_rope = jnp.where(low, gk[1:2], 0.0)
    for h in range(HEADS):
        kn = kv[:, h * MLA_NOPE:(h + 1) * MLA_NOPE]
        inv = lax.rsqrt((jnp.sum(kn * kn, axis=-1, keepdims=True) + ss_kr) * (1.0 / MLA_QK) + EPS)
        k_ref[h, :, 0:MLA_NOPE] = (kn * inv * gk[0:1]).astype(BF16)
        yr = _rope128(kr * inv * gk_rope, cos, sa, sb)
        k_ref[h, :, MLA_NOPE:MLA_QK] = yr[:, :MLA_ROPE].astype(BF16)
        vt_ref[h, 0:MLA_V, :] = kv[:, HEADS * MLA_NOPE + h * MLA_V:HEADS * MLA_NOPE + (h + 1) * MLA_V].T.astype(BF16)
        vt_ref[h, MLA_V:, :] = _ones_row(kr.shape[0])


def _mla_prep(p_mla, tables, qn, kvn, wuq, wukv, gq, gk):
    t = p_mla.shape[0]
    rows = _row_block(t, ROW_BLOCK // 2)
    full = lambda a: pl.BlockSpec(a.shape, lambda i: (0,) * a.ndim)
    rowb = lambda w: pl.BlockSpec((rows, w), lambda i: (i, 0))
    args = (qn, kvn, wuq, wukv, gq, gk)
    return pl.pallas_call(
        _mla_prep_kernel,
        grid=(t // rows,),
        in_specs=[rowb(MLA_IN), rowb(LANES), rowb(LANES), rowb(LANES)] + [full(a) for a in args],
        out_specs=[pl.BlockSpec((HEADS, MLA_QK, rows), lambda i: (0, 0, i)),
                   pl.BlockSpec((HEADS, rows, MLA_QK), lambda i: (0, i, 0)),
                   pl.BlockSpec((HEADS, V_AUG, rows), lambda i: (0, 0, i))],
        out_shape=[jax.ShapeDtypeStruct((HEADS, MLA_QK, t), BF16),
                   jax.ShapeDtypeStruct((HEADS, t, MLA_QK), BF16),
                   jax.ShapeDtypeStruct((HEADS, V_AUG, t), BF16)],
        compiler_params=_cparams(("parallel",)),
        name="mla_prep",
    )(p_mla, *tables, *args)


def _ones_row(cols):
    return (lax.broadcasted_iota(jnp.int32, (V_AUG - LANES, cols), 0) == 0).astype(BF16)


def _attend(items, vt_ref, first):
    scores = [_dot(k_ref[0, rows, :], qt) for qt, k_ref, _, _, rows in items]
    for (_, _, acc_sc, m_sc, rows), s in zip(items, scores):
        vt = vt_ref[0, :, rows]
        if m_sc is None:
            pv = _dot(vt, jnp.exp(s).astype(BF16))
            acc_sc[...] = pv if first else acc_sc[...] + pv
        else:
            m_new = jnp.max(s, axis=0, keepdims=True)
            if not first:
                m_prev = m_sc[...]
                m_new = jnp.maximum(m_prev, m_new)
            pv = _dot(vt, jnp.exp(s - m_new).astype(BF16))
            acc_sc[...] = pv if first else jnp.exp(m_prev - m_new) * acc_sc[...] + pv
            m_sc[...] = m_new


def _attend_all(qts, k_refs, vt_ref, accs, ms, m_ctx, bq, bk, n_blocks, finish):
    pairs = list(zip(qts, k_refs, accs, ms))
    ctx_items = [p + (slice(0, m_ctx),) for p in pairs]
    split = max(1, 2 // len(pairs))
    sub = bk // split
    is_latent = pl.program_id(1) >= m_ctx // bq

    @pl.when(jnp.logical_not(is_latent))
    def _():
        _attend(ctx_items, vt_ref, True)
        finish()

    @pl.when(is_latent)
    def _():
        _attend(ctx_items, vt_ref, True)
        for j in range(n_blocks):
            _attend([p + (slice(m_ctx + j * bk + u * sub, m_ctx + j * bk + (u + 1) * sub),)
                     for p in pairs for u in range(split)], vt_ref, False)
        finish()


def _normalised(acc_sc):
    acc = acc_sc[...]
    return (acc[:LANES] / acc[LANES:LANES + 1]).T


def _mla_attn_kernel(qt_ref, k_ref, vt_ref, o_ref, acc_sc, *m_sc, m_ctx, bq, bk, n_blocks):
    m = m_sc[0] if m_sc else None

    def finish():
        o_ref[...] = _normalised(acc_sc)

    _attend_all([qt_ref[0]], [k_ref], vt_ref, [acc_sc], [m], m_ctx, bq, bk, n_blocks, finish)


def _attn_blocks(t, m_ctx):
    bq = ATT_BQ
    assert m_ctx % bq == 0 and t % bq == 0
    n_lat = t - m_ctx
    bk = _row_block(n_lat, ATT_BK)
    return bq, bk, n_lat // bk


def _attn_scratch(bq, n_pairs, shifted):
    return [pltpu.VMEM((V_AUG, bq), F32)] * n_pairs + ([pltpu.VMEM((1, bq), F32)] * n_pairs if shifted else [])


def _by_score_bound(bound, attend, *operands):
    return lax.cond(bound <= SCORE_LIMIT,
                    functools.partial(attend, shifted=False),
                    functools.partial(attend, shifted=True), *operands)


def _mla_attn(qt, k, vt, m_ctx, shifted):
    t = k.shape[1]
    bq, bk, n_blocks = _attn_blocks(t, m_ctx)
    return pl.pallas_call(
        functools.partial(_mla_attn_kernel, m_ctx=m_ctx, bq=bq, bk=bk, n_blocks=n_blocks),
        grid=(HEADS, t // bq),
        in_specs=[pl.BlockSpec((1, MLA_QK, bq), lambda h, i: (h, 0, i)),
                  pl.BlockSpec((1, t, MLA_QK), lambda h, i: (h, 0, 0)),
                  pl.BlockSpec((1, V_AUG, t), lambda h, i: (h, 0, 0))],
        out_specs=pl.BlockSpec((bq, MLA_V), lambda h, i: (i, h)),
        out_shape=jax.ShapeDtypeStruct((t, HEADS * MLA_V), F32),
        scratch_shapes=_attn_scratch(bq, 1, shifted),
        compiler_params=_cparams(("parallel", "arbitrary")),
        name="mla_attn_shifted" if shifted else "mla_attn",
    )(qt, k, vt)


def _diff_attn_kernel(q1_ref, q2_ref, k1_ref, k2_ref, v_ref, lam_ref, sub_ref, o_ref, a1, a2, *m_sc,
                      m_ctx, bq, bk, n_blocks, out_scale):
    ms = list(m_sc) if m_sc else [None, None]

    def finish():
        d = _normalised(a1) - lam_ref[...] * _normalised(a2)
        y = d * lax.rsqrt(jnp.mean(d * d, axis=-1, keepdims=True) + EPS)
        o_ref[...] = y * sub_ref[...] * out_scale

    _attend_all([q1_ref[0], q2_ref[0]], [k1_ref, k2_ref], v_ref, [a1, a2], ms, m_ctx, bq, bk, n_blocks, finish)


def _diff_attn(dqt, dk, dvt, lam, subln, m_ctx, out_scale, shifted):
    t = dk.shape[1]
    bq, bk, n_blocks = _attn_blocks(t, m_ctx)
    qspec = lambda c: pl.BlockSpec((1, DA_DIM, bq), lambda h, i: (2 * h + c, 0, i))
    kspec = lambda c: pl.BlockSpec((1, t, DA_DIM), lambda h, i: (2 * h + c, 0, 0))
    vec = pl.BlockSpec((1, DA_V), lambda h, i: (0, 0))
    return pl.pallas_call(
        functools.partial(_diff_attn_kernel, m_ctx=m_ctx, bq=bq, bk=bk, n_blocks=n_blocks,
                          out_scale=out_scale),
        grid=(HEADS, t // bq),
        in_specs=[qspec(0), qspec(1), kspec(0), kspec(1),
                  pl.BlockSpec((1, V_AUG, t), lambda h, i: (h, 0, 0)), vec, vec],
        out_specs=pl.BlockSpec((bq, DA_V), lambda h, i: (i, h)),
        out_shape=jax.ShapeDtypeStruct((t, HEADS * DA_V), F32),
        scratch_shapes=_attn_scratch(bq, 2, shifted),
        compiler_params=_cparams(("parallel", "arbitrary")),
        name="diff_attn_shifted" if shifted else "diff_attn",
    )(dqt, dqt, dk, dk, dvt, lam, subln)


def _odd_prep_kernel(p_ref, cos_ref, sa_ref, sb_ref, gq_ref, gk_ref,
                     dqt_ref, dk_ref, dvt_ref, rq_ref, rk_ref):
    cos, sa, sb = cos_ref[...], sa_ref[...], sb_ref[...]
    da_w = HEADS * 2 * DA_DIM

    def sub_rms(x, g):
        low, ss_lo, ss_hi = _half_sums(x * x)
        inv = jnp.where(low, lax.rsqrt(ss_lo * (1.0 / DA_DIM) + EPS), lax.rsqrt(ss_hi * (1.0 / DA_DIM) + EPS))
        return x * inv * g

    for h in range(HEADS):
        col = h * LANES
        yqt = (_rope128(sub_rms(p_ref[:, col:col + LANES], gq_ref[...]), cos, sa, sb) * (DA_DIM ** -0.5)).T
        dqt_ref[2 * h] = yqt[:DA_DIM].astype(BF16)
        dqt_ref[2 * h + 1] = yqt[DA_DIM:].astype(BF16)
        yk = _rope128(sub_rms(p_ref[:, da_w + col:da_w + col + LANES], gk_ref[...]), cos, sa, sb)
        dk_ref[2 * h] = yk[:, :DA_DIM].astype(BF16)
        dk_ref[2 * h + 1] = yk[:, DA_DIM:].astype(BF16)
        dvt_ref[h, 0:DA_V, :] = p_ref[:, 2 * da_w + col:2 * da_w + col + LANES].T.astype(BF16)
        dvt_ref[h, DA_V:, :] = _ones_row(cos.shape[0])
    r0 = 2 * da_w + HEADS * DA_V
    for b in range(HEADS // 2):
        col = r0 + b * LANES
        yq = _rope128(p_ref[:, col:col + LANES], cos, sa, sb)
        rq_ref[2 * b] = yq[:, :RT_K]
        rq_ref[2 * b + 1] = yq[:, RT_K:]
        col = r0 + HEADS * RT_K + b * LANES
        yk = _rope128(p_ref[:, col:col + LANES] * (RT_K ** -0.5), cos, sa, sb)
        rk_ref[2 * b] = yk[:, :RT_K]
        rk_ref[2 * b + 1] = yk[:, RT_K:]


def _odd_prep(p, tables, gq, gk):
    t = p.shape[0]
    rows = _row_block(t, ROW_BLOCK // 2)
    width = 3 * HEADS * 2 * DA_DIM + 2 * HEADS * RT_K
    full = lambda a: pl.BlockSpec(a.shape, lambda i: (0,) * a.ndim)
    rowb = lambda w: pl.BlockSpec((rows, w), lambda i: (i, 0))
    hm = lambda n, w: pl.BlockSpec((n, rows, w), lambda i: (0, i, 0))
    return pl.pallas_call(
        _odd_prep_kernel,
        grid=(t // rows,),
        in_specs=[rowb(width), rowb(LANES), rowb(LANES), rowb(LANES), full(gq), full(gk)],
        out_specs=[pl.BlockSpec((2 * HEADS, DA_DIM, rows), lambda i: (0, 0, i)),
                   hm(2 * HEADS, DA_DIM),
                   pl.BlockSpec((HEADS, V_AUG, rows), lambda i: (0, 0, i)), hm(HEADS, RT_K), hm(HEADS, RT_K)],
        out_shape=[jax.ShapeDtypeStruct((2 * HEADS, DA_DIM, t), BF16),
                   jax.ShapeDtypeStruct((2 * HEADS, t, DA_DIM), BF16),
                   jax.ShapeDtypeStruct((HEADS, V_AUG, t), BF16),
                   jax.ShapeDtypeStruct((HEADS, t, RT_K), F32),
                   jax.ShapeDtypeStruct((HEADS, t, RT_K), F32)],
        compiler_params=_cparams(("parallel",)),
        name="odd_prep",
    )(p, *tables, gq, gk)


def _outproj_kernel(x_ref, att_ref, of_ref, ob_ref, gate_ref, nrm_ref, wa_ref, wr_ref, mod_ref, o_ref,
                    *, m_ctx, rows):
    o = of_ref[...] + ob_ref[...]
    gate = gate_ref[...]
    rec = []
    for h in range(HEADS):
        oh = o[:, h * LANES:(h + 1) * LANES]
        inv = lax.rsqrt(jnp.mean(oh * oh, axis=-1, keepdims=True) + EPS)
        rec.append(oh * inv * nrm_ref[...] * _silu(gate[:, h * LANES:(h + 1) * LANES]))
    rec = jnp.concatenate(rec, axis=-1).astype(BF16)
    y = _dot(att_ref[...].astype(BF16), wa_ref[...]) + _dot(rec, wr_ref[...])
    g1 = _row_mod(mod_ref, 2, pl.program_id(0) * rows, rows, m_ctx)
    o_ref[...] = x_ref[...] + g1 * y


def _outproj(xt, att, o_f, o_b, gate_arr, gate_block, nrm, w_att, w_rec, mod, m_ctx):
    t, d = xt.shape
    rows = _row_block(t, ROW_BLOCK // 2)
    w = att.shape[1]
    full = lambda a: pl.BlockSpec(a.shape, lambda i: (0,) * a.ndim)
    rowb = lambda c: pl.BlockSpec((rows, c), lambda i: (i, 0))
    return pl.pallas_call(
        functools.partial(_outproj_kernel, m_ctx=m_ctx, rows=rows),
        grid=(t // rows,),
        in_specs=[rowb(d), rowb(w), rowb(w), rowb(w),
                  pl.BlockSpec((rows, w), lambda i: (i, gate_block)),
                  full(nrm), full(w_att), full(w_rec), full(mod)],
        out_specs=rowb(d),
        out_shape=jax.ShapeDtypeStruct((t, d), F32),
        compiler_params=_cparams(("parallel",)),
        name="out_proj",
    )(xt, att, o_f, o_b, gate_arr, nrm, w_att, w_rec, mod)


def _mlp_kernel(x_ref, mod_ref, nw_ref, w1_ref, w2_ref, o_ref, h_sc, acc_sc, *, m_ctx, rows):
    j = pl.program_id(1)
    row0 = pl.program_id(0) * rows

    @pl.when(j == 0)
    def _():
        h_sc[...] = _modnorm(x_ref[...], nw_ref[...],
                             _row_mod(mod_ref, 4, row0, rows, m_ctx),
                             _row_mod(mod_ref, 3, row0, rows, m_ctx)).astype(BF16)
        acc_sc[...] = jnp.zeros_like(acc_sc)

    u = jnp.maximum(_dot(h_sc[...], w1_ref[...]), 0.0)
    acc_sc[...] += _dot((u * u).astype(BF16), w2_ref[...])

    @pl.when(j == pl.num_programs(1) - 1)
    def _():
        o_ref[...] = x_ref[...] + _row_mod(mod_ref, 5, row0, rows, m_ctx) * acc_sc[...]


def _mlp(xt, mod, nw, w1, w2, m_ctx):
    t, d = xt.shape
    hid = w1.shape[1]
    rows = _row_block(t, ROW_BLOCK)
    th = 512
    full = lambda a: pl.BlockSpec(a.shape, lambda i, j: (0,) * a.ndim)
    return pl.pallas_call(
        functools.partial(_mlp_kernel, m_ctx=m_ctx, rows=rows),
        grid=(t // rows, hid // th),
        in_specs=[pl.BlockSpec((rows, d), lambda i, j: (i, 0)), full(mod), full(nw),
                  pl.BlockSpec((d, th), lambda i, j: (0, j)),
                  pl.BlockSpec((th, d), lambda i, j: (j, 0))],
        out_specs=pl.BlockSpec((rows, d), lambda i, j: (i, 0)),
        out_shape=jax.ShapeDtypeStruct((t, d), F32),
        scratch_shapes=[pltpu.VMEM((rows, d), BF16), pltpu.VMEM((rows, d), F32)],
        compiler_params=_cparams(("parallel", "arbitrary")),
        name="mlp",
    )(xt, mod, nw, w1, w2)


def _head_major(w, parts):
    k = w.shape[0]
    wh = w.reshape(k, HEADS, sum(parts))
    out, off = [], 0
    for width in parts:
        out.append(wh[:, :, off:off + width].reshape(k, HEADS * width))
        off += width
    return jnp.concatenate(out, axis=1)


def kernel(x, c, ctx, c_ctx, ada_w, ada_b, norm_w, w_o, mlp_w1, mlp_w2, a_w_in, hg_lb, hg_norm, mla_q_norm,
           mla_kv_norm, mla_w_uq, mla_w_ukv, mla_qk_q, mla_qk_k, c_w_in, da_lambda, da_qk_q, da_qk_k,
           da_subln, rt_decay, rt_norm):
    assert x.shape[0] == 1 and ctx.shape[0] == 1
    n, d = x.shape[1], x.shape[2]
    m_ctx = ctx.shape[1]
    depth = ada_w.shape[0]
    xt = jnp.concatenate([ctx[0], x[0]], axis=0)

    cvec = jnp.zeros((8, d), F32).at[0].set(c_ctx).at[1].set(c[0])
    mods = _ada_table(cvec, ada_w, ada_b)[:, :2].reshape(depth, 2, 6, d)
    tables = _rope_tables(n, m_ctx)

    lb = jnp.cumsum(jax.nn.softmax(hg_lb.astype(F32), axis=0), axis=0)
    lb = lb - lb[:1]
    log_lb = jnp.log(lb).reshape(-1, 2, 1, HG_W)
    log1m_lb = jnp.log1p(-lb).reshape(-1, 2, 1, HG_W)

    for l in range(depth):
        j = l // 2
        mod = mods[l]
        nw = norm_w[l]
        wo = w_o[l].astype(BF16)
        if l % 2 == 0:
            w_in = a_w_in[j]
            w_main = w_in[:, :5 * HG_W].astype(BF16)
            w_mla = jnp.pad(w_in[:, 5 * HG_W:], ((0, 0), (0, MLA_IN - (w_in.shape[1] - 5 * HG_W)))).astype(BF16)
            p_main, p_mla = _proj(xt, mod, nw[0:1], [w_main, w_mla], m_ctx)
            o_f, o_b = _hgrn2(p_main, log_lb[j], log1m_lb[j], m_ctx)
            gq = mla_qk_q[j]
            gk = mla_qk_k[j]
            q, kt, v = _mla_prep(
                p_mla, tables, mla_q_norm[j][None], mla_kv_norm[j][None],
                _head_major(mla_w_uq[j], (MLA_NOPE, MLA_ROPE)).astype(BF16),
                _head_major(mla_w_ukv[j], (MLA_NOPE, MLA_V)).astype(BF16),
                jnp.stack([gq[:MLA_NOPE], jnp.tile(gq[MLA_NOPE:], 2)]),
                jnp.stack([gk[:MLA_NOPE], jnp.tile(gk[MLA_NOPE:], 2)]))
            bound = 1.02 * MLA_QK ** 0.5 * jnp.max(jnp.abs(gq)) * jnp.max(jnp.abs(gk))
            att = _by_score_bound(bound, functools.partial(_mla_attn, m_ctx=m_ctx), q, kt, v)
            xt = _outproj(xt, att, o_f, o_b, p_main, 4, hg_norm[j][None], wo[HG_W:], wo[:HG_W], mod, m_ctx)
        else:
            (p,) = _proj(xt, mod, nw[0:1], [c_w_in[j].astype(BF16)], m_ctx)
            dq, dkt, dv, rq, rk = _odd_prep(p, tables, jnp.tile(da_qk_q[j], 2)[None], jnp.tile(da_qk_k[j], 2)[None])
            lam_init = 0.8 - 0.6 * math.exp(-0.3 * l)
            lf = da_lambda[j].astype(F32)
            lam = jnp.exp(jnp.sum(lf[0] * lf[1])) - jnp.exp(jnp.sum(lf[2] * lf[3])) + lam_init
            bound = 1.02 * DA_DIM ** 0.5 * jnp.max(jnp.abs(da_qk_q[j])) * jnp.max(jnp.abs(da_qk_k[j]))
            att = _by_score_bound(
                bound, functools.partial(_diff_attn, m_ctx=m_ctx, out_scale=1.0 - lam_init),
                dq, dkt, dv, jnp.full((1, DA_V), lam, F32), da_subln[j][None])
            lg = jax.nn.log_sigmoid(rt_decay[j].astype(F32))
            lg = jnp.broadcast_to(lg[:, :, None, None], (2, HEADS, 8, LANES))
            r_f, r_b = _retention(rq, rk, p, (3 * HEADS * 2 * DA_DIM + 2 * HEADS * RT_K) // (HEADS * RT_V), lg, m_ctx)
            xt = _outproj(xt, att, r_f, r_b, p, 5, rt_norm[j][None], wo[:HG_W], wo[HG_W:], mod, m_ctx)
        xt = _mlp(xt, mod, nw[1:2], mlp_w1[l].astype(BF16), mlp_w2[l].astype(BF16), m_ctx)
    return xt[m_ctx:][None]
```

```python
import functools
import math

import numpy as np
import jax
import jax.numpy as jnp
from jax import lax
from jax.experimental import pallas as pl
from jax.experimental.pallas import tpu as pltpu

F32 = jnp.float32
BF16 = jnp.bfloat16

GRID_W = 64
ROPE_DIM = 64
ROPE_BASE = 10000.0
EPS = 1e-6
HEADS = 4
HG_DIM = 128
HG_W = HEADS * HG_DIM
MLA_NOPE = 128
MLA_ROPE = ROPE_DIM
MLA_V = 128
MLA_QK = MLA_NOPE + MLA_ROPE
MLA_Q_RANK = 384
MLA_KV_RANK = 256
DA_DIM = ROPE_DIM
DA_V = 2 * DA_DIM
RT_K = ROPE_DIM
RT_V = 128
MLA_IN = 768

LANES = 128
VMEM_LIMIT = 56 * 1024 * 1024
ROW_BLOCK = 1280
ATT_BQ = 256
ATT_BK = 4096
SCAN_CHUNK = 64
RET_CHUNK = 128
RET_SUB = 2
V_AUG = LANES + 16
SCORE_LIMIT = 40.0


def _cparams(sem):
    return pltpu.CompilerParams(dimension_semantics=sem, vmem_limit_bytes=VMEM_LIMIT)


def _row_block(t, target):
    best = None
    for r in range(LANES, min(t, target) + 1, LANES):
        if t % r == 0:
            best = r
    assert best is not None, t
    return best


def _dot(a, b):
    return jnp.dot(a, b, preferred_element_type=F32)


def _dot_nt(a, b):
    return lax.dot_general(a, b, (((1,), (1,)), ((), ())), preferred_element_type=F32)


def _dot_tn(a, b):
    return lax.dot_general(a, b, (((0,), (0,)), ((), ())), preferred_element_type=F32)


def _silu(x):
    return x * (1.0 / (1.0 + jnp.exp(-x)))


def _row_mod(mod_ref, k, row0, rows, m_ctx):
    r = row0 + lax.broadcasted_iota(jnp.int32, (rows, 1), 0)
    return jnp.where(r < m_ctx, mod_ref[0, k:k + 1, :], mod_ref[1, k:k + 1, :])


def _modnorm(x, nw, sc, sh):
    y = x * lax.rsqrt(jnp.mean(x * x, axis=-1, keepdims=True) + EPS)
    return y * nw * (1.0 + sc) + sh


def _ada_kernel(c_ref, w_ref, b_ref, o_ref):
    cv = c_ref[...]
    o_ref[0] = jnp.dot(_silu(cv), w_ref[0], precision=lax.Precision.HIGHEST,
                       preferred_element_type=F32) + b_ref[0]


def _ada_table(cvec, ada_w, ada_b):
    depth, d, d6 = ada_w.shape
    tn = d6 // 4
    return pl.pallas_call(
        _ada_kernel,
        grid=(depth, d6 // tn),
        in_specs=[pl.BlockSpec((8, d), lambda l, j: (0, 0)),
                  pl.BlockSpec((1, d, tn), lambda l, j: (l, 0, j)),
                  pl.BlockSpec((1, 1, tn), lambda l, j: (l, 0, j))],
        out_specs=pl.BlockSpec((1, 8, tn), lambda l, j: (l, 0, j)),
        out_shape=jax.ShapeDtypeStruct((depth, 8, d6), F32),
        compiler_params=_cparams(("parallel", "parallel")),
        name="ada_table",
    )(cvec, ada_w, ada_b.reshape(depth, 1, d6))


def _mixer_input(x_ref, mod_ref, nw_ref, m_ctx, rows):
    row0 = pl.program_id(0) * rows
    return _modnorm(x_ref[...], nw_ref[...],
                    _row_mod(mod_ref, 1, row0, rows, m_ctx),
                    _row_mod(mod_ref, 0, row0, rows, m_ctx)).astype(BF16)


def _resident(a):
    return pl.BlockSpec(a.shape, lambda i: (0,) * a.ndim, pipeline_mode=pl.Buffered(1))


def _hier_tables(c, reverse):
    levels = int(math.log2(c))
    assert 1 << levels == c
    cums = np.zeros(((levels + 1) * c, c), np.float32)
    roles = np.zeros((levels, c, LANES), np.float32)
    masks = np.zeros((levels + 1, c, c), np.float32)
    for li in range(levels):
        h = c >> (li + 1)
        for t in range(c):
            base = (t // (2 * h)) * 2 * h
            late = (t - base) >= h
            if not reverse:
                if late:
                    cums[li * c + t, base + h:t + 1] = 1.0
                else:
                    cums[li * c + t, t + 1:base + h] = 1.0
            else:
                if late:
                    cums[li * c + t, base + h:t] = 1.0
                else:
                    cums[li * c + t, t:base + h] = 1.0
            is_query = late != reverse
            roles[li, t, :] = 1.0 if is_query else 0.0
        for t in range(c):
            for s in range(c):
                same = (t // (2 * h)) == (s // (2 * h))
                if same and roles[li, t, 0] == 1.0 and roles[li, s, 0] == 0.0:
                    masks[li, t, s] = 1.0
    for t in range(c):
        if not reverse:
            cums[levels * c + t, :t + 1] = 1.0
        else:
            cums[levels * c + t, t:] = 1.0
    masks[levels] = np.eye(c, dtype=np.float32)
    return cums, roles, masks


def _split3(x):
    hi = x.astype(BF16)
    r1 = x - hi.astype(F32)
    mid = r1.astype(BF16)
    lo = (r1 - mid.astype(F32)).astype(BF16)
    return hi, mid, lo


def _gated_chunks(sides, c):
    levels = sides[0][5].shape[0]
    chains = []
    for q_all, kk_all, v_all, logf_all, cum_ref, role_ref, mask_ref, st_ref, o_ref, reverse in sides:
        x_all = _dot(cum_ref[...], jnp.concatenate(_split3(logf_all), axis=0))
        for h in range(HEADS):
            sl = slice(h * HG_DIM, (h + 1) * HG_DIM)
            chains.append((x_all[:, sl], q_all[:, sl], kk_all[:, sl], v_all[:, sl].astype(BF16),
                           role_ref, mask_ref, st_ref, o_ref, reverse, h, sl))
    acc = [_dot_nt(q.astype(BF16), kk.astype(BF16)) * mask_ref[levels]
           for _, q, kk, _, _, mask_ref, *_ in chains]
    for li in range(levels):
        for n, (x, q, kk, _, role_ref, mask_ref, *_) in enumerate(chains):
            z = (jnp.where(role_ref[li] > 0.5, q, kk) * jnp.exp(x[li * c:(li + 1) * c])).astype(BF16)
            acc[n] = acc[n] + _dot_nt(z, z) * mask_ref[li]
    for a, (x, q, kk, vb, _, _, st_ref, o_ref, reverse, h, sl) in zip(acc, chains):
        run = x[levels * c:(levels + 1) * c]
        tot = run[0:1] if reverse else run[c - 1:c]
        st = st_ref[h]
        o_ref[:, sl] = _dot(a.astype(BF16), vb) + _dot_nt((q * jnp.exp(run)).astype(BF16), st.astype(BF16))
        kd = (kk * jnp.exp(tot - run)).astype(BF16)
        st_ref[h] = st * jnp.exp(tot) + _dot_tn(vb, kd)


def _hgrn2_gate(z, log_lb, log1m_lb):
    soft = jnp.log1p(jnp.exp(-jnp.abs(z)))
    b = log1m_lb + (jnp.minimum(z, 0.0) - soft)
    logf = jnp.maximum(log_lb, b) + jnp.log1p(jnp.exp(-jnp.abs(log_lb - b)))
    return logf, jnp.exp(log1m_lb + (jnp.minimum(-z, 0.0) - soft))


def _hgrn2_kernel(qf_ref, zf_ref, vf_ref, qb_ref, zb_ref, vb_ref, llb_ref, l1m_ref,
                  cumf_ref, rolef_ref, maskf_ref, cumb_ref, roleb_ref, maskb_ref,
                  of_ref, ob_ref, sf_ref, sb_ref, *, c):
    @pl.when(pl.program_id(0) == 0)
    def _():
        sf_ref[...] = jnp.zeros_like(sf_ref)
        sb_ref[...] = jnp.zeros_like(sb_ref)

    logf_f, kk_f = _hgrn2_gate(zf_ref[...], llb_ref[0], l1m_ref[0])
    logf_b, kk_b = _hgrn2_gate(zb_ref[...], llb_ref[1], l1m_ref[1])
    _gated_chunks(
        [(_silu(qf_ref[...]), kk_f, vf_ref[...], logf_f, cumf_ref, rolef_ref, maskf_ref, sf_ref, of_ref, False)], c)
    _gated_chunks(
        [(_silu(qb_ref[...]), kk_b, vb_ref[...], logf_b, cumb_ref, roleb_ref, maskb_ref, sb_ref, ob_ref, True)], c)


def _bwd_chunk(i, mc, steps):
    return jnp.where(i < mc, mc - 1 - i, steps - 1 - i + mc)


def _hgrn2(p_main, log_lb, log1m_lb, m_ctx):
    t = p_main.shape[0]
    c = SCAN_CHUNK
    steps, mc = t // c, m_ctx // c
    tabs = [jnp.asarray(a) for rev in (False, True) for a in _hier_tables(c, rev)]
    for k in (0, 3):
        tabs[k] = jnp.tile(tabs[k], (1, 3)).astype(BF16)
    fwd = lambda sec: pl.BlockSpec((c, HG_W), lambda i: (i, sec))
    bwd = lambda sec: pl.BlockSpec((c, HG_W), lambda i: (_bwd_chunk(i, mc, steps), sec))
    full = lambda a: pl.BlockSpec(a.shape, lambda i: (0,) * a.ndim)
    return pl.pallas_call(
        functools.partial(_hgrn2_kernel, c=c),
        grid=(steps,),
        in_specs=[fwd(0), fwd(1), fwd(3), bwd(0), bwd(2), bwd(3), full(log_lb), full(log1m_lb)]
                 + [full(a) for a in tabs],
        out_specs=[pl.BlockSpec((c, HG_W), lambda i: (i, 0)),
                   pl.BlockSpec((c, HG_W), lambda i: (_bwd_chunk(i, mc, steps), 0))],
        out_shape=[jax.ShapeDtypeStruct((t, HG_W), F32)] * 2,
        scratch_shapes=[pltpu.VMEM((HEADS, HG_DIM, HG_DIM), F32)] * 2,
        compiler_params=_cparams(("arbitrary",)),
        name="hgrn2_scan",
    )(p_main, p_main, p_main, p_main, p_main, p_main, log_lb, log1m_lb, *tabs)


def _ret_scores(q, k, lg, c, reverse):
    t = lax.broadcasted_iota(jnp.int32, (c, c), 0)
    s = lax.broadcasted_iota(jnp.int32, (c, c), 1)
    dlt = (s - t) if reverse else (t - s)
    dec = jnp.where(dlt >= 0, jnp.exp(lg[:, :c] * jnp.maximum(dlt, 0).astype(F32)), 0.0)
    return (_dot_nt(q.astype(BF16), k.astype(BF16)) * dec).astype(BF16)


def _ret_finish(a, q, k, v, lg, st_ref, c, reverse):
    r = lax.broadcasted_iota(jnp.int32, (c, RT_K), 0).astype(F32)
    lk = lg[:, :RT_K]
    qdec = jnp.exp(lk * ((c - r) if reverse else (r + 1.0)))
    kdec = jnp.exp(lk * (r if reverse else (c - 1.0 - r)))
    st = st_ref[...]
    vb = v.astype(BF16)
    o = _dot(a, vb) + _dot_nt((q * qdec).astype(BF16), st.astype(BF16))
    st_ref[...] = st * jnp.exp(lk * float(c)) + _dot_tn(vb, (k * kdec).astype(BF16))
    return o


def _ret_kernel(qf_ref, kf_ref, vf_ref, qb_ref, kb_ref, vb_ref, lg_ref,
                of_ref, ob_ref, sf_ref, sb_ref, *, c, sub):
    @pl.when(pl.program_id(0) == 0)
    def _():
        sf_ref[...] = jnp.zeros_like(sf_ref)
        sb_ref[...] = jnp.zeros_like(sb_ref)

    sides = ((qf_ref, kf_ref, vf_ref, of_ref, sf_ref), (qb_ref, kb_ref, vb_ref, ob_ref, sb_ref))
    chains = [(sides[d], d, h, pl.ds((sub - 1 - u if d else u) * c, c))
              for u in range(sub) for h in range(HEADS) for d in range(2)]
    scores = [_ret_scores(q_ref[h, rows], k_ref[h, rows], lg_ref[d, h, 0:1, :], c, d == 1)
              for (q_ref, k_ref, _, _, _), d, h, rows in chains]
    for a, ((q_ref, k_ref, v_ref, o_ref, s_ref), d, h, rows) in zip(scores, chains):
        sl = slice(h * RT_V, (h + 1) * RT_V)
        o_ref[rows, sl] = _ret_finish(a, q_ref[h, rows], k_ref[h, rows], v_ref[rows, sl], lg_ref[d, h, 0:1, :],
                                      s_ref.at[h], c, d == 1)


def _retention(rq, rk, p, v_block, lg, m_ctx):
    t = p.shape[0]
    c, sub = RET_CHUNK, RET_SUB
    rows = c * sub
    steps, mc = t // rows, m_ctx // rows
    assert steps * rows == t and mc * rows == m_ctx
    width = HEADS * RT_V
    fq = pl.BlockSpec((HEADS, rows, RT_K), lambda i: (0, i, 0))
    bq = pl.BlockSpec((HEADS, rows, RT_K), lambda i: (0, _bwd_chunk(i, mc, steps), 0))
    fv = pl.BlockSpec((rows, width), lambda i: (i, v_block))
    bv = pl.BlockSpec((rows, width), lambda i: (_bwd_chunk(i, mc, steps), v_block))
    return pl.pallas_call(
        functools.partial(_ret_kernel, c=c, sub=sub),
        grid=(steps,),
        in_specs=[fq, fq, fv, bq, bq, bv, pl.BlockSpec(lg.shape, lambda i: (0, 0, 0, 0))],
        out_specs=[pl.BlockSpec((rows, width), lambda i: (i, 0)),
                   pl.BlockSpec((rows, width), lambda i: (_bwd_chunk(i, mc, steps), 0))],
        out_shape=[jax.ShapeDtypeStruct((t, width), F32)] * 2,
        scratch_shapes=[pltpu.VMEM((HEADS, RT_V, RT_K), F32)] * 2,
        compiler_params=_cparams(("arbitrary",)),
        name="retention_scan",
    )(rq, rk, p, rq, rk, p, lg)


def _rope128(y, cos, sin_a, sin_b):
    return y * cos + pltpu.roll(y, LANES - 16, 1) * sin_a + pltpu.roll(y, 16, 1) * sin_b


def _rope_tables(n, m_ctx):
    rows = n // GRID_W
    row = jnp.repeat(jnp.arange(rows, dtype=F32), GRID_W)
    col = jnp.tile(jnp.arange(GRID_W, dtype=F32), rows)
    quarter = ROPE_DIM // 4
    inv_freq = ROPE_BASE ** (-jnp.arange(quarter, dtype=F32) / quarter)
    ang_r = row[:, None] * inv_freq
    ang_c = col[:, None] * inv_freq
    ang = jnp.concatenate([ang_r, ang_r, ang_c, ang_c], axis=-1)
    cos, sin = jnp.cos(ang), jnp.sin(ang)
    first = (jnp.arange(ROPE_DIM) % 32) < 16
    sin_a = jnp.where(first, -sin, 0.0)
    sin_b = jnp.where(first, 0.0, sin)
    pad = lambda a, v: jnp.tile(jnp.concatenate([jnp.full((m_ctx, ROPE_DIM), v, F32), a], axis=0), (1, 2))
    return pad(cos, 1.0), pad(sin_a, 0.0), pad(sin_b, 0.0)


def _half_sums(sq):
    low = lax.broadcasted_iota(jnp.int32, sq.shape, 1) < 64
    lo = jnp.sum(jnp.where(low, sq, 0.0), axis=-1, keepdims=True)
    return low, lo, jnp.sum(sq, axis=-1, keepdims=True) - lo


def _mla_up(p, qn_ref, kvn_ref, wuq_ref, wukv_ref):
    def rms(x, w):
        return x * lax.rsqrt(jnp.mean(x * x, axis=-1, keepdims=True) + EPS) * w

    qu = _dot(rms(p[:, :MLA_Q_RANK], qn_ref[...]).astype(BF16), wuq_ref[...])
    kv = _dot(rms(p[:, MLA_Q_RANK:MLA_Q_RANK + MLA_KV_RANK], kvn_ref[...]).astype(BF16),
              wukv_ref[...])
    return qu, kv, p[:, MLA_Q_RANK + MLA_KV_RANK:]


def _mla_heads(qu, kv, kr, cos, sa, sb, gq_ref, gk_ref, qt_ref, k_ref, vt_ref):
    scale = MLA_QK ** -0.5
    ss_kr = jnp.sum(kr * kr, axis=-1, keepdims=True)
    gq, gk = gq_ref[...], gk_ref[...]
    for b in range(HEADS // 2):
        qr = qu[:, HEADS * MLA_NOPE + b * LANES:HEADS * MLA_NOPE + (b + 1) * LANES]
        low, ss_lo, ss_hi = _half_sums(qr * qr)
        inv = []
        for j in range(2):
            h = 2 * b + j
            qn = qu[:, h * MLA_NOPE:(h + 1) * MLA_NOPE]
            ss = jnp.sum(qn * qn, axis=-1, keepdims=True) + (ss_lo, ss_hi)[j]
            inv.append(lax.rsqrt(ss * (1.0 / MLA_QK) + EPS))
            qt_ref[h, 0:MLA_NOPE, :] = (qn * inv[j] * gq[0:1] * scale).T.astype(BF16)
        yrt = (_rope128(qr * jnp.where(low, inv[0], inv[1]) * gq[1:2], cos, sa, sb) * scale).T
        qt_ref[2 * b, MLA_NOPE:MLA_QK, :] = yrt[:MLA_ROPE].astype(BF16)
        qt_ref[2 * b + 1, MLA_NOPE:MLA_QK, :] = yrt[MLA_ROPE:].astype(BF16)
    low = lax.broadcasted_iota(jnp.int32, kr.shape, 1) < 64
    gk_rope = jnp.where(low, gk[1:2], 0.0)
    for h in range(HEADS):
        kn = kv[:, h * MLA_NOPE:(h + 1) * MLA_NOPE]
        inv = lax.rsqrt((jnp.sum(kn * kn, axis=-1, keepdims=True) + ss_kr) * (1.0 / MLA_QK) + EPS)
        k_ref[h, :, 0:MLA_NOPE] = (kn * inv * gk[0:1]).astype(BF16)
        yr = _rope128(kr * inv * gk_rope, cos, sa, sb)
        k_ref[h, :, MLA_NOPE:MLA_QK] = yr[:, :MLA_ROPE].astype(BF16)
        vt_ref[h, 0:MLA_V, :] = kv[:, HEADS * MLA_NOPE + h * MLA_V:HEADS * MLA_NOPE + (h + 1) * MLA_V].T.astype(BF16)
        vt_ref[h, MLA_V:, :] = _ones_row(kr.shape[0])


def _even_front_kernel(x_ref, mod_ref, nw_ref, wm_ref, wa_ref, cos_ref, sa_ref, sb_ref, qn_ref, kvn_ref,
                       wuq_ref, wukv_ref, gq_ref, gk_ref, pm_ref, qt_ref, k_ref, vt_ref, *, m_ctx, rows):
    h = _mixer_input(x_ref, mod_ref, nw_ref, m_ctx, rows)
    qu, kv, kr = _mla_up(_dot(h, wa_ref[...]), qn_ref, kvn_ref, wuq_ref, wukv_ref)
    pm_ref[...] = _dot(h, wm_ref[...])
    _mla_heads(qu, kv, kr, cos_ref[...], sa_ref[...], sb_ref[...], gq_ref, gk_ref, qt_ref, k_ref, vt_ref)


def _even_front(xt, mod, nw, w_main, w_mla, tables, qn, kvn, wuq, wukv, gq, gk, m_ctx):
    t, d = xt.shape
    rows = _row_block(t, ROW_BLOCK // 2)
    rowb = lambda w: pl.BlockSpec((rows, w), lambda i: (i, 0))
    consts = (w_main, w_mla), (qn, kvn, wuq, wukv, gq, gk)
    return pl.pallas_call(
        functools.partial(_even_front_kernel, m_ctx=m_ctx, rows=rows),
        grid=(t // rows,),
        in_specs=[rowb(d), _resident(mod), _resident(nw)] + [_resident(a) for a in consts[0]]
                 + [rowb(LANES)] * 3 + [_resident(a) for a in consts[1]],
        out_specs=[rowb(w_main.shape[1]),
                   pl.BlockSpec((HEADS, MLA_QK, rows), lambda i: (0, 0, i)),
                   pl.BlockSpec((HEADS, rows, MLA_QK), lambda i: (0, i, 0)),
                   pl.BlockSpec((HEADS, V_AUG, rows), lambda i: (0, 0, i))],
        out_shape=[jax.ShapeDtypeStruct((t, w_main.shape[1]), F32),
                   jax.ShapeDtypeStruct((HEADS, MLA_QK, t), BF16),
                   jax.ShapeDtypeStruct((HEADS, t, MLA_QK), BF16),
                   jax.ShapeDtypeStruct((HEADS, V_AUG, t), BF16)],
        compiler_params=_cparams(("parallel",)),
        name="even_front",
    )(xt, mod, nw, w_main, w_mla, *tables, qn, kvn, wuq, wukv, gq, gk)


def _ones_row(cols):
    return (lax.broadcasted_iota(jnp.int32, (V_AUG - LANES, cols), 0) == 0).astype(BF16)


def _attend(items, vt_ref, first):
    scores = [_dot(k_ref[0, rows, :], qt) for qt, k_ref, _, _, rows in items]
    for (_, _, acc_sc, m_sc, rows), s in zip(items, scores):
        vt = vt_ref[0, :, rows]
        if m_sc is None:
            pv = _dot(vt, jnp.exp(s).astype(BF16))
            acc_sc[...] = pv if first else acc_sc[...] + pv
        else:
            m_new = jnp.max(s, axis=0, keepdims=True)
            if not first:
                m_prev = m_sc[...]
                m_new = jnp.maximum(m_prev, m_new)
            pv = _dot(vt, jnp.exp(s - m_new).astype(BF16))
            acc_sc[...] = pv if first else jnp.exp(m_prev - m_new) * acc_sc[...] + pv
            m_sc[...] = m_new


def _attend_all(qts, k_refs, vt_ref, accs, ms, m_ctx, bq, bk, n_blocks, finish):
    pairs = list(zip(qts, k_refs, accs, ms))
    ctx_items = [p + (slice(0, m_ctx),) for p in pairs]
    split = max(1, 2 // len(pairs))
    sub = bk // split
    is_latent = pl.program_id(1) >= m_ctx // bq

    @pl.when(jnp.logical_not(is_latent))
    def _():
        _attend(ctx_items, vt_ref, True)
        finish()

    @pl.when(is_latent)
    def _():
        _attend(ctx_items, vt_ref, True)
        for j in range(n_blocks):
            _attend([p + (slice(m_ctx + j * bk + u * sub, m_ctx + j * bk + (u + 1) * sub),)
                     for p in pairs for u in range(split)], vt_ref, False)
        finish()


def _normalised(acc_sc):
    acc = acc_sc[...]
    return (acc[:LANES] / acc[LANES:LANES + 1]).T


def _mla_attn_kernel(qt_ref, k_ref, vt_ref, o_ref, acc_sc, *m_sc, m_ctx, bq, bk, n_blocks):
    m = m_sc[0] if m_sc else None

    def finish():
        o_ref[...] = _normalised(acc_sc)

    _attend_all([qt_ref[0]], [k_ref], vt_ref, [acc_sc], [m], m_ctx, bq, bk, n_blocks, finish)


def _attn_blocks(t, m_ctx):
    bq = ATT_BQ
    assert m_ctx % bq == 0 and t % bq == 0
    n_lat = t - m_ctx
    bk = _row_block(n_lat, ATT_BK)
    return bq, bk, n_lat // bk


def _attn_scratch(bq, n_pairs, shifted):
    return [pltpu.VMEM((V_AUG, bq), F32)] * n_pairs + ([pltpu.VMEM((1, bq), F32)] * n_pairs if shifted else [])


def _by_score_bound(bound, attend, *operands):
    return lax.cond(bound <= SCORE_LIMIT,
                    functools.partial(attend, shifted=False),
                    functools.partial(attend, shifted=True), *operands)


def _mla_attn(qt, k, vt, m_ctx, shifted):
    t = k.shape[1]
    bq, bk, n_blocks = _attn_blocks(t, m_ctx)
    return pl.pallas_call(
        functools.partial(_mla_attn_kernel, m_ctx=m_ctx, bq=bq, bk=bk, n_blocks=n_blocks),
        grid=(HEADS, t // bq),
        in_specs=[pl.BlockSpec((1, MLA_QK, bq), lambda h, i: (h, 0, i)),
                  pl.BlockSpec((1, t, MLA_QK), lambda h, i: (h, 0, 0)),
                  pl.BlockSpec((1, V_AUG, t), lambda h, i: (h, 0, 0))],
        out_specs=pl.BlockSpec((bq, MLA_V), lambda h, i: (i, h)),
        out_shape=jax.ShapeDtypeStruct((t, HEADS * MLA_V), F32),
        scratch_shapes=_attn_scratch(bq, 1, shifted),
        compiler_params=_cparams(("parallel", "arbitrary")),
        name="mla_attn_shifted" if shifted else "mla_attn",
    )(qt, k, vt)


def _diff_attn_kernel(q1_ref, q2_ref, k1_ref, k2_ref, v_ref, lam_ref, sub_ref, o_ref, a1, a2, *m_sc,
                      m_ctx, bq, bk, n_blocks, out_scale):
    ms = list(m_sc) if m_sc else [None, None]

    def finish():
        d = _normalised(a1) - lam_ref[...] * _normalised(a2)
        y = d * lax.rsqrt(jnp.mean(d * d, axis=-1, keepdims=True) + EPS)
        o_ref[...] = y * sub_ref[...] * out_scale

    _attend_all([q1_ref[0], q2_ref[0]], [k1_ref, k2_ref], v_ref, [a1, a2], ms, m_ctx, bq, bk, n_blocks, finish)


def _diff_attn(dqt, dk, dvt, lam, subln, m_ctx, out_scale, shifted):
    t = dk.shape[1]
    bq, bk, n_blocks = _attn_blocks(t, m_ctx)
    qspec = lambda c: pl.BlockSpec((1, DA_DIM, bq), lambda h, i: (2 * h + c, 0, i))
    kspec = lambda c: pl.BlockSpec((1, t, DA_DIM), lambda h, i: (2 * h + c, 0, 0))
    vec = pl.BlockSpec((1, DA_V), lambda h, i: (0, 0))
    return pl.pallas_call(
        functools.partial(_diff_attn_kernel, m_ctx=m_ctx, bq=bq, bk=bk, n_blocks=n_blocks,
                          out_scale=out_scale),
        grid=(HEADS, t // bq),
        in_specs=[qspec(0), qspec(1), kspec(0), kspec(1),
                  pl.BlockSpec((1, V_AUG, t), lambda h, i: (h, 0, 0)), vec, vec],
        out_specs=pl.BlockSpec((bq, DA_V), lambda h, i: (i, h)),
        out_shape=jax.ShapeDtypeStruct((t, HEADS * DA_V), F32),
        scratch_shapes=_attn_scratch(bq, 2, shifted),
        compiler_params=_cparams(("parallel", "arbitrary")),
        name="diff_attn_shifted" if shifted else "diff_attn",
    )(dqt, dqt, dk, dk, dvt, lam, subln)


ODD_ATT_W = 3 * HEADS * 2 * DA_DIM + 2 * HEADS * RT_K


def _odd_prep(p, cos, sa, sb, gq_ref, gk_ref, dqt_ref, dk_ref, dvt_ref, rq_ref, rk_ref):
    da_w = HEADS * 2 * DA_DIM

    def sub_rms(x, g):
        low, ss_lo, ss_hi = _half_sums(x * x)
        inv = jnp.where(low, lax.rsqrt(ss_lo * (1.0 / DA_DIM) + EPS), lax.rsqrt(ss_hi * (1.0 / DA_DIM) + EPS))
        return x * inv * g

    for h in range(HEADS):
        col = h * LANES
        yqt = (_rope128(sub_rms(p[:, col:col + LANES], gq_ref[...]), cos, sa, sb) * (DA_DIM ** -0.5)).T
        dqt_ref[2 * h] = yqt[:DA_DIM].astype(BF16)
        dqt_ref[2 * h + 1] = yqt[DA_DIM:].astype(BF16)
        yk = _rope128(sub_rms(p[:, da_w + col:da_w + col + LANES], gk_ref[...]), cos, sa, sb)
        dk_ref[2 * h] = yk[:, :DA_DIM].astype(BF16)
        dk_ref[2 * h + 1] = yk[:, DA_DIM:].astype(BF16)
        dvt_ref[h, 0:DA_V, :] = p[:, 2 * da_w + col:2 * da_w + col + LANES].T.astype(BF16)
        dvt_ref[h, DA_V:, :] = _ones_row(cos.shape[0])
    r0 = 2 * da_w + HEADS * DA_V
    for b in range(HEADS // 2):
        col = r0 + b * LANES
        yq = _rope128(p[:, col:col + LANES], cos, sa, sb)
        rq_ref[2 * b] = yq[:, :RT_K]
        rq_ref[2 * b + 1] = yq[:, RT_K:]
        col = r0 + HEADS * RT_K + b * LANES
        yk = _rope128(p[:, col:col + LANES] * (RT_K ** -0.5), cos, sa, sb)
        rk_ref[2 * b] = yk[:, :RT_K]
        rk_ref[2 * b + 1] = yk[:, RT_K:]


def _odd_front_kernel(x_ref, mod_ref, nw_ref, wa_ref, wr_ref, cos_ref, sa_ref, sb_ref, gq_ref, gk_ref,
                      dqt_ref, dk_ref, dvt_ref, rq_ref, rk_ref, rest_ref, *, m_ctx, rows):
    h = _mixer_input(x_ref, mod_ref, nw_ref, m_ctx, rows)
    p_att = _dot(h, wa_ref[...])
    rest_ref[...] = _dot(h, wr_ref[...])
    _odd_prep(p_att, cos_ref[...], sa_ref[...], sb_ref[...], gq_ref, gk_ref,
              dqt_ref, dk_ref, dvt_ref, rq_ref, rk_ref)


def _odd_front(xt, mod, nw, w_in, tables, gq, gk, m_ctx):
    t, d = xt.shape
    rows = _row_block(t, ROW_BLOCK // 2)
    w_att, w_rest = w_in[:, :ODD_ATT_W], w_in[:, ODD_ATT_W:]
    rowb = lambda w: pl.BlockSpec((rows, w), lambda i: (i, 0))
    hm = lambda n, w: pl.BlockSpec((n, rows, w), lambda i: (0, i, 0))
    return pl.pallas_call(
        functools.partial(_odd_front_kernel, m_ctx=m_ctx, rows=rows),
        grid=(t // rows,),
        in_specs=[rowb(d), _resident(mod), _resident(nw), _resident(w_att), _resident(w_rest)]
                 + [rowb(LANES)] * 3 + [_resident(gq), _resident(gk)],
        out_specs=[pl.BlockSpec((2 * HEADS, DA_DIM, rows), lambda i: (0, 0, i)),
                   hm(2 * HEADS, DA_DIM),
                   pl.BlockSpec((HEADS, V_AUG, rows), lambda i: (0, 0, i)), hm(HEADS, RT_K), hm(HEADS, RT_K),
                   rowb(w_rest.shape[1])],
        out_shape=[jax.ShapeDtypeStruct((2 * HEADS, DA_DIM, t), BF16),
                   jax.ShapeDtypeStruct((2 * HEADS, t, DA_DIM), BF16),
                   jax.ShapeDtypeStruct((HEADS, V_AUG, t), BF16),
                   jax.ShapeDtypeStruct((HEADS, t, RT_K), F32),
                   jax.ShapeDtypeStruct((HEADS, t, RT_K), F32),
                   jax.ShapeDtypeStruct((t, w_rest.shape[1]), F32)],
        compiler_params=_cparams(("parallel",)),
        name="odd_front",
    )(xt, mod, nw, w_att, w_rest, *tables, gq, gk)


def _outproj_kernel(x_ref, att_ref, of_ref, ob_ref, gate_ref, nrm_ref, wa_ref, wr_ref, mod_ref, o_ref,
                    *, m_ctx, rows):
    o = of_ref[...] + ob_ref[...]
    gate = gate_ref[...]
    rec = []
    for h in range(HEADS):
        oh = o[:, h * LANES:(h + 1) * LANES]
        inv = lax.rsqrt(jnp.mean(oh * oh, axis=-1, keepdims=True) + EPS)
        rec.append(oh * inv * nrm_ref[...] * _silu(gate[:, h * LANES:(h + 1) * LANES]))
    rec = jnp.concatenate(rec, axis=-1).astype(BF16)
    y = _dot(att_ref[...].astype(BF16), wa_ref[...]) + _dot(rec, wr_ref[...])
    g1 = _row_mod(mod_ref, 2, pl.program_id(0) * rows, rows, m_ctx)
    o_ref[...] = x_ref[...] + g1 * y


def _outproj(xt, att, o_f, o_b, gate_arr, gate_block, nrm, w_att, w_rec, mod, m_ctx):
    t, d = xt.shape
    rows = _row_block(t, ROW_BLOCK // 2)
    w = att.shape[1]
    full = lambda a: pl.BlockSpec(a.shape, lambda i: (0,) * a.ndim)
    rowb = lambda c: pl.BlockSpec((rows, c), lambda i: (i, 0))
    return pl.pallas_call(
        functools.partial(_outproj_kernel, m_ctx=m_ctx, rows=rows),
        grid=(t // rows,),
        in_specs=[rowb(d), rowb(w), rowb(w), rowb(w),
                  pl.BlockSpec((rows, w), lambda i: (i, gate_block)),
                  full(nrm), full(w_att), full(w_rec), full(mod)],
        out_specs=rowb(d),
        out_shape=jax.ShapeDtypeStruct((t, d), F32),
        compiler_params=_cparams(("parallel",)),
        name="out_proj",
    )(xt, att, o_f, o_b, gate_arr, nrm, w_att, w_rec, mod)


def _mlp_kernel(x_ref, mod_ref, nw_ref, w1_ref, w2_ref, o_ref, h_sc, acc_sc, *, m_ctx, rows):
    j = pl.program_id(1)
    row0 = pl.program_id(0) * rows

    @pl.when(j == 0)
    def _():
        h_sc[...] = _modnorm(x_ref[...], nw_ref[...],
                             _row_mod(mod_ref, 4, row0, rows, m_ctx),
                             _row_mod(mod_ref, 3, row0, rows, m_ctx)).astype(BF16)
        acc_sc[...] = jnp.zeros_like(acc_sc)

    u = jnp.maximum(_dot(h_sc[...], w1_ref[...]), 0.0)
    acc_sc[...] += _dot((u * u).astype(BF16), w2_ref[...])

    @pl.when(j == pl.num_programs(1) - 1)
    def _():
        o_ref[...] = x_ref[...] + _row_mod(mod_ref, 5, row0, rows, m_ctx) * acc_sc[...]


def _mlp(xt, mod, nw, w1, w2, m_ctx):
    t, d = xt.shape
    hid = w1.shape[1]
    rows = _row_block(t, ROW_BLOCK)
    th = 512
    full = lambda a: pl.BlockSpec(a.shape, lambda i, j: (0,) * a.ndim)
    return pl.pallas_call(
        functools.partial(_mlp_kernel, m_ctx=m_ctx, rows=rows),
        grid=(t // rows, hid // th),
        in_specs=[pl.BlockSpec((rows, d), lambda i, j: (i, 0)), full(mod), full(nw),
                  pl.BlockSpec((d, th), lambda i, j: (0, j)),
                  pl.BlockSpec((th, d), lambda i, j: (j, 0))],
        out_specs=pl.BlockSpec((rows, d), lambda i, j: (i, 0)),
        out_shape=jax.ShapeDtypeStruct((t, d), F32),
        scratch_shapes=[pltpu.VMEM((rows, d), BF16), pltpu.VMEM((rows, d), F32)],
        compiler_params=_cparams(("parallel", "arbitrary")),
        name="mlp",
    )(xt, mod, nw, w1, w2)


def _head_major(w, parts):
    k = w.shape[0]
    wh = w.reshape(k, HEADS, sum(parts))
    out, off = [], 0
    for width in parts:
        out.append(wh[:, :, off:off + width].reshape(k, HEADS * width))
        off += width
    return jnp.concatenate(out, axis=1)


def kernel(x, c, ctx, c_ctx, ada_w, ada_b, norm_w, w_o, mlp_w1, mlp_w2, a_w_in, hg_lb, hg_norm, mla_q_norm,
           mla_kv_norm, mla_w_uq, mla_w_ukv, mla_qk_q, mla_qk_k, c_w_in, da_lambda, da_qk_q, da_qk_k,
           da_subln, rt_decay, rt_norm):
    assert x.shape[0] == 1 and ctx.shape[0] == 1
    n, d = x.shape[1], x.shape[2]
    m_ctx = ctx.shape[1]
    depth = ada_w.shape[0]
    xt = jnp.concatenate([ctx[0], x[0]], axis=0)

    cvec = jnp.zeros((8, d), F32).at[0].set(c_ctx).at[1].set(c[0])
    mods = _ada_table(cvec, ada_w, ada_b)[:, :2].reshape(depth, 2, 6, d)
    tables = _rope_tables(n, m_ctx)

    lb = jnp.cumsum(jax.nn.softmax(hg_lb.astype(F32), axis=0), axis=0)
    lb = lb - lb[:1]
    log_lb = jnp.log(lb).reshape(-1, 2, 1, HG_W)
    log1m_lb = jnp.log1p(-lb).reshape(-1, 2, 1, HG_W)

    for l in range(depth):
        j = l // 2
        mod = mods[l]
        nw = norm_w[l]
        wo = w_o[l].astype(BF16)
        if l % 2 == 0:
            w_in = a_w_in[j]
            w_main = w_in[:, :5 * HG_W].astype(BF16)
            w_mla = jnp.pad(w_in[:, 5 * HG_W:], ((0, 0), (0, MLA_IN - (w_in.shape[1] - 5 * HG_W)))).astype(BF16)
            gq = mla_qk_q[j]
            gk = mla_qk_k[j]
            p_main, qt, k, vt = _even_front(
                xt, mod, nw[0:1], w_main, w_mla, tables, mla_q_norm[j][None], mla_kv_norm[j][None],
                _head_major(mla_w_uq[j], (MLA_NOPE, MLA_ROPE)).astype(BF16),
                _head_major(mla_w_ukv[j], (MLA_NOPE, MLA_V)).astype(BF16),
                jnp.stack([gq[:MLA_NOPE], jnp.tile(gq[MLA_NOPE:], 2)]),
                jnp.stack([gk[:MLA_NOPE], jnp.tile(gk[MLA_NOPE:], 2)]), m_ctx)
            o_f, o_b = _hgrn2(p_main, log_lb[j], log1m_lb[j], m_ctx)
            bound = 1.02 * MLA_QK ** 0.5 * jnp.max(jnp.abs(gq)) * jnp.max(jnp.abs(gk))
            att = _by_score_bound(bound, functools.partial(_mla_attn, m_ctx=m_ctx), qt, k, vt)
            xt = _outproj(xt, att, o_f, o_b, p_main, 4, hg_norm[j][None], wo[HG_W:], wo[:HG_W], mod, m_ctx)
        else:
            dq, dkt, dv, rq, rk, p = _odd_front(xt, mod, nw[0:1], c_w_in[j].astype(BF16), tables,
                                                 jnp.tile(da_qk_q[j], 2)[None], jnp.tile(da_qk_k[j], 2)[None], m_ctx)
            lam_init = 0.8 - 0.6 * math.exp(-0.3 * l)
            lf = da_lambda[j].astype(F32)
            lam = jnp.exp(jnp.sum(lf[0] * lf[1])) - jnp.exp(jnp.sum(lf[2] * lf[3])) + lam_init
            bound = 1.02 * DA_DIM ** 0.5 * jnp.max(jnp.abs(da_qk_q[j])) * jnp.max(jnp.abs(da_qk_k[j]))
            att = _by_score_bound(
                bound, functools.partial(_diff_attn, m_ctx=m_ctx, out_scale=1.0 - lam_init),
                dq, dkt, dv, jnp.full((1, DA_V), lam, F32), da_subln[j][None])
            lg = jax.nn.log_sigmoid(rt_decay[j].astype(F32))
            lg = jnp.broadcast_to(lg[:, :, None, None], (2, HEADS, 8, LANES))
            r_f, r_b = _retention(rq, rk, p, 0, lg, m_ctx)
            xt = _outproj(xt, att, r_f, r_b, p, 1, rt_norm[j][None], wo[:HG_W], wo[HG_W:], mod, m_ctx)
        xt = _mlp(xt, mod, nw[1:2], mlp_w1[l].astype(BF16), mlp_w2[l].astype(BF16), m_ctx)
    return xt[m_ctx:][None]
```

```python
import functools
import math

import numpy as np
import jax
import jax.numpy as jnp
from jax import lax
from jax.experimental import pallas as pl
from jax.experimental.pallas import tpu as pltpu

F32 = jnp.float32
BF16 = jnp.bfloat16

GRID_W = 64
ROPE_DIM = 64
ROPE_BASE = 10000.0
EPS = 1e-6
HEADS = 4
HG_DIM = 128
HG_W = HEADS * HG_DIM
MLA_NOPE = 128
MLA_ROPE = ROPE_DIM
MLA_V = 128
MLA_QK = MLA_NOPE + MLA_ROPE
MLA_Q_RANK = 384
MLA_KV_RANK = 256
DA_DIM = ROPE_DIM
DA_V = 2 * DA_DIM
RT_K = ROPE_DIM
RT_V = 128
MLA_IN = 768

LANES = 128
VMEM_LIMIT = 56 * 1024 * 1024
ROW_BLOCK = 1280
ATT_BQ = 256
ATT_BK = 4096
SCAN_CHUNK = 64
RET_CHUNK = 128
RET_SUB = 2
SUBLANES = 8
ATT_V = 128
ACC_ROWS = ATT_V + SUBLANES
SCORE_LIMIT = 40.0


def _cparams(sem):
    return pltpu.CompilerParams(dimension_semantics=sem, vmem_limit_bytes=VMEM_LIMIT)


def _row_block(t, target):
    best = None
    for r in range(LANES, min(t, target) + 1, LANES):
        if t % r == 0:
            best = r
    assert best is not None, t
    return best


def _dot(a, b):
    return jnp.dot(a, b, preferred_element_type=F32)


def _dot_nt(a, b):
    return lax.dot_general(a, b, (((1,), (1,)), ((), ())), preferred_element_type=F32)


def _dot_tn(a, b):
    return lax.dot_general(a, b, (((0,), (0,)), ((), ())), preferred_element_type=F32)


def _silu(x):
    return x * (1.0 / (1.0 + jnp.exp(-x)))


def _row_mod(mod_ref, k, row0, rows, m_ctx):
    r = row0 + lax.broadcasted_iota(jnp.int32, (rows, 1), 0)
    return jnp.where(r < m_ctx, mod_ref[0, k:k + 1, :], mod_ref[1, k:k + 1, :])


def _modnorm(x, nw, sc, sh):
    y = x * lax.rsqrt(jnp.mean(x * x, axis=-1, keepdims=True) + EPS)
    return y * nw * (1.0 + sc) + sh


def _ada_kernel(c_ref, w_ref, b_ref, o_ref):
    cv = c_ref[...]
    o_ref[0] = jnp.dot(_silu(cv), w_ref[0], precision=lax.Precision.HIGHEST,
                       preferred_element_type=F32) + b_ref[0]


def _ada_table(cvec, ada_w, ada_b):
    depth, d, d6 = ada_w.shape
    tn = d6 // 4
    return pl.pallas_call(
        _ada_kernel,
        grid=(depth, d6 // tn),
        in_specs=[pl.BlockSpec((8, d), lambda l, j: (0, 0)),
                  pl.BlockSpec((1, d, tn), lambda l, j: (l, 0, j)),
                  pl.BlockSpec((1, 1, tn), lambda l, j: (l, 0, j))],
        out_specs=pl.BlockSpec((1, 8, tn), lambda l, j: (l, 0, j)),
        out_shape=jax.ShapeDtypeStruct((depth, 8, d6), F32),
        compiler_params=_cparams(("parallel", "parallel")),
        name="ada_table",
    )(cvec, ada_w, ada_b.reshape(depth, 1, d6))


def _mixer_input(x_ref, mod_ref, nw_ref, m_ctx, rows):
    row0 = pl.program_id(0) * rows
    return _modnorm(x_ref[...], nw_ref[...],
                    _row_mod(mod_ref, 1, row0, rows, m_ctx),
                    _row_mod(mod_ref, 0, row0, rows, m_ctx)).astype(BF16)


def _resident(a):
    return pl.BlockSpec(a.shape, lambda i: (0,) * a.ndim, pipeline_mode=pl.Buffered(1))


def _hier_tables(c, reverse):
    levels = int(math.log2(c))
    assert 1 << levels == c
    cums = np.zeros(((levels + 1) * c, c), np.float32)
    roles = np.zeros((levels, c, LANES), np.float32)
    masks = np.zeros((levels + 1, c, c), np.float32)
    for li in range(levels):
        h = c >> (li + 1)
        for t in range(c):
            base = (t // (2 * h)) * 2 * h
            late = (t - base) >= h
            if not reverse:
                if late:
                    cums[li * c + t, base + h:t + 1] = 1.0
                else:
                    cums[li * c + t, t + 1:base + h] = 1.0
            else:
                if late:
                    cums[li * c + t, base + h:t] = 1.0
                else:
                    cums[li * c + t, t:base + h] = 1.0
            is_query = late != reverse
            roles[li, t, :] = 1.0 if is_query else 0.0
        for t in range(c):
            for s in range(c):
                same = (t // (2 * h)) == (s // (2 * h))
                if same and roles[li, t, 0] == 1.0 and roles[li, s, 0] == 0.0:
                    masks[li, t, s] = 1.0
    for t in range(c):
        if not reverse:
            cums[levels * c + t, :t + 1] = 1.0
        else:
            cums[levels * c + t, t:] = 1.0
    masks[levels] = np.eye(c, dtype=np.float32)
    return cums, roles, masks


def _split3(x):
    hi = x.astype(BF16)
    r1 = x - hi.astype(F32)
    mid = r1.astype(BF16)
    lo = (r1 - mid.astype(F32)).astype(BF16)
    return hi, mid, lo


def _gated_chunks(sides, c):
    levels = sides[0][5].shape[0]
    chains = []
    for q_all, kk_all, v_all, logf_all, cum_ref, role_ref, mask_ref, st_ref, o_ref, reverse in sides:
        x_all = _dot(cum_ref[...], jnp.concatenate(_split3(logf_all), axis=0))
        for h in range(HEADS):
            sl = slice(h * HG_DIM, (h + 1) * HG_DIM)
            chains.append((x_all[:, sl], q_all[:, sl], kk_all[:, sl], v_all[:, sl].astype(BF16),
                           role_ref, mask_ref, st_ref, o_ref, reverse, h, sl))
    acc = [_dot_nt(q.astype(BF16), kk.astype(BF16)) * mask_ref[levels]
           for _, q, kk, _, _, mask_ref, *_ in chains]
    for li in range(levels):
        for n, (x, q, kk, _, role_ref, mask_ref, *_) in enumerate(chains):
            z = (jnp.where(role_ref[li] > 0.5, q, kk) * jnp.exp(x[li * c:(li + 1) * c])).astype(BF16)
            acc[n] = acc[n] + _dot_nt(z, z) * mask_ref[li]
    for a, (x, q, kk, vb, _, _, st_ref, o_ref, reverse, h, sl) in zip(acc, chains):
        run = x[levels * c:(levels + 1) * c]
        tot = run[0:1] if reverse else run[c - 1:c]
        st = st_ref[h]
        o_ref[:, sl] = _dot(a.astype(BF16), vb) + _dot_nt((q * jnp.exp(run)).astype(BF16), st.astype(BF16))
        kd = (kk * jnp.exp(tot - run)).astype(BF16)
        st_ref[h] = st * jnp.exp(tot) + _dot_tn(vb, kd)


def _hgrn2_gate(z, log_lb, log1m_lb):
    soft = jnp.log1p(jnp.exp(-jnp.abs(z)))
    b = log1m_lb + (jnp.minimum(z, 0.0) - soft)
    logf = jnp.maximum(log_lb, b) + jnp.log1p(jnp.exp(-jnp.abs(log_lb - b)))
    return logf, jnp.exp(log1m_lb + (jnp.minimum(-z, 0.0) - soft))


def _hgrn2_kernel(qf_ref, zf_ref, vf_ref, qb_ref, zb_ref, vb_ref, llb_ref, l1m_ref,
                  cumf_ref, rolef_ref, maskf_ref, cumb_ref, roleb_ref, maskb_ref,
                  of_ref, ob_ref, sf_ref, sb_ref, *, c):
    @pl.when(pl.program_id(0) == 0)
    def _():
        sf_ref[...] = jnp.zeros_like(sf_ref)
        sb_ref[...] = jnp.zeros_like(sb_ref)

    logf_f, kk_f = _hgrn2_gate(zf_ref[...], llb_ref[0], l1m_ref[0])
    logf_b, kk_b = _hgrn2_gate(zb_ref[...], llb_ref[1], l1m_ref[1])
    _gated_chunks(
        [(_silu(qf_ref[...]), kk_f, vf_ref[...], logf_f, cumf_ref, rolef_ref, maskf_ref, sf_ref, of_ref, False)], c)
    _gated_chunks(
        [(_silu(qb_ref[...]), kk_b, vb_ref[...], logf_b, cumb_ref, roleb_ref, maskb_ref, sb_ref, ob_ref, True)], c)


def _bwd_chunk(i, mc, steps):
    return jnp.where(i < mc, mc - 1 - i, steps - 1 - i + mc)


def _hgrn2(p_main, log_lb, log1m_lb, m_ctx):
    t = p_main.shape[0]
    c = SCAN_CHUNK
    steps, mc = t // c, m_ctx // c
    tabs = [jnp.asarray(a) for rev in (False, True) for a in _hier_tables(c, rev)]
    for k in (0, 3):
        tabs[k] = jnp.tile(tabs[k], (1, 3)).astype(BF16)
    fwd = lambda sec: pl.BlockSpec((c, HG_W), lambda i: (i, sec))
    bwd = lambda sec: pl.BlockSpec((c, HG_W), lambda i: (_bwd_chunk(i, mc, steps), sec))
    full = lambda a: pl.BlockSpec(a.shape, lambda i: (0,) * a.ndim)
    return pl.pallas_call(
        functools.partial(_hgrn2_kernel, c=c),
        grid=(steps,),
        in_specs=[fwd(0), fwd(1), fwd(3), bwd(0), bwd(2), bwd(3), full(log_lb), full(log1m_lb)]
                 + [full(a) for a in tabs],
        out_specs=[pl.BlockSpec((c, HG_W), lambda i: (i, 0)),
                   pl.BlockSpec((c, HG_W), lambda i: (_bwd_chunk(i, mc, steps), 0))],
        out_shape=[jax.ShapeDtypeStruct((t, HG_W), F32)] * 2,
        scratch_shapes=[pltpu.VMEM((HEADS, HG_DIM, HG_DIM), F32)] * 2,
        compiler_params=_cparams(("arbitrary",)),
        name="hgrn2_scan",
    )(p_main, p_main, p_main, p_main, p_main, p_main, log_lb, log1m_lb, *tabs)


def _ret_scores(q, k, lg, c, reverse):
    t = lax.broadcasted_iota(jnp.int32, (c, c), 0)
    s = lax.broadcasted_iota(jnp.int32, (c, c), 1)
    dlt = (s - t) if reverse else (t - s)
    dec = jnp.where(dlt >= 0, jnp.exp(lg[:, :c] * jnp.maximum(dlt, 0).astype(F32)), 0.0)
    return (_dot_nt(q.astype(BF16), k.astype(BF16)) * dec).astype(BF16)


def _ret_finish(a, q, k, v, lg, st_ref, c, reverse):
    r = lax.broadcasted_iota(jnp.int32, (c, RT_K), 0).astype(F32)
    lk = lg[:, :RT_K]
    qdec = jnp.exp(lk * ((c - r) if reverse else (r + 1.0)))
    kdec = jnp.exp(lk * (r if reverse else (c - 1.0 - r)))
    st = st_ref[...]
    vb = v.astype(BF16)
    o = _dot(a, vb) + _dot_nt((q * qdec).astype(BF16), st.astype(BF16))
    st_ref[...] = st * jnp.exp(lk * float(c)) + _dot_tn(vb, (k * kdec).astype(BF16))
    return o


def _ret_kernel(qf_ref, kf_ref, vf_ref, qb_ref, kb_ref, vb_ref, lg_ref,
                of_ref, ob_ref, sf_ref, sb_ref, *, c, sub):
    @pl.when(pl.program_id(0) == 0)
    def _():
        sf_ref[...] = jnp.zeros_like(sf_ref)
        sb_ref[...] = jnp.zeros_like(sb_ref)

    sides = ((qf_ref, kf_ref, vf_ref, of_ref, sf_ref), (qb_ref, kb_ref, vb_ref, ob_ref, sb_ref))
    chains = [(sides[d], d, h, pl.ds((sub - 1 - u if d else u) * c, c))
              for u in range(sub) for h in range(HEADS) for d in range(2)]
    scores = [_ret_scores(q_ref[h, rows], k_ref[h, rows], lg_ref[d, h, 0:1, :], c, d == 1)
              for (q_ref, k_ref, _, _, _), d, h, rows in chains]
    for a, ((q_ref, k_ref, v_ref, o_ref, s_ref), d, h, rows) in zip(scores, chains):
        sl = slice(h * RT_V, (h + 1) * RT_V)
        o_ref[rows, sl] = _ret_finish(a, q_ref[h, rows], k_ref[h, rows], v_ref[rows, sl], lg_ref[d, h, 0:1, :],
                                      s_ref.at[h], c, d == 1)


def _retention(rq, rk, p, v_block, lg, m_ctx):
    t = p.shape[0]
    c, sub = RET_CHUNK, RET_SUB
    rows = c * sub
    steps, mc = t // rows, m_ctx // rows
    assert steps * rows == t and mc * rows == m_ctx
    width = HEADS * RT_V
    fq = pl.BlockSpec((HEADS, rows, RT_K), lambda i: (0, i, 0))
    bq = pl.BlockSpec((HEADS, rows, RT_K), lambda i: (0, _bwd_chunk(i, mc, steps), 0))
    fv = pl.BlockSpec((rows, width), lambda i: (i, v_block))
    bv = pl.BlockSpec((rows, width), lambda i: (_bwd_chunk(i, mc, steps), v_block))
    return pl.pallas_call(
        functools.partial(_ret_kernel, c=c, sub=sub),
        grid=(steps,),
        in_specs=[fq, fq, fv, bq, bq, bv, pl.BlockSpec(lg.shape, lambda i: (0, 0, 0, 0))],
        out_specs=[pl.BlockSpec((rows, width), lambda i: (i, 0)),
                   pl.BlockSpec((rows, width), lambda i: (_bwd_chunk(i, mc, steps), 0))],
        out_shape=[jax.ShapeDtypeStruct((t, width), F32)] * 2,
        scratch_shapes=[pltpu.VMEM((HEADS, RT_V, RT_K), F32)] * 2,
        compiler_params=_cparams(("arbitrary",)),
        name="retention_scan",
    )(rq, rk, p, rq, rk, p, lg)


def _rope128(y, cos, sin_a, sin_b):
    return y * cos + pltpu.roll(y, LANES - 16, 1) * sin_a + pltpu.roll(y, 16, 1) * sin_b


def _rope_tables(n, m_ctx):
    rows = n // GRID_W
    row = jnp.repeat(jnp.arange(rows, dtype=F32), GRID_W)
    col = jnp.tile(jnp.arange(GRID_W, dtype=F32), rows)
    quarter = ROPE_DIM // 4
    inv_freq = ROPE_BASE ** (-jnp.arange(quarter, dtype=F32) / quarter)
    ang_r = row[:, None] * inv_freq
    ang_c = col[:, None] * inv_freq
    ang = jnp.concatenate([ang_r, ang_r, ang_c, ang_c], axis=-1)
    cos, sin = jnp.cos(ang), jnp.sin(ang)
    first = (jnp.arange(ROPE_DIM) % 32) < 16
    sin_a = jnp.where(first, -sin, 0.0)
    sin_b = jnp.where(first, 0.0, sin)
    pad = lambda a, v: jnp.tile(jnp.concatenate([jnp.full((m_ctx, ROPE_DIM), v, F32), a], axis=0), (1, 2))
    return pad(cos, 1.0), pad(sin_a, 0.0), pad(sin_b, 0.0)


def _half_sums(sq):
    low = lax.broadcasted_iota(jnp.int32, sq.shape, 1) < 64
    lo = jnp.sum(jnp.where(low, sq, 0.0), axis=-1, keepdims=True)
    return low, lo, jnp.sum(sq, axis=-1, keepdims=True) - lo


def _mla_up(p, qn_ref, kvn_ref, wuq_ref, wukv_ref):
    def rms(x, w):
        return x * lax.rsqrt(jnp.mean(x * x, axis=-1, keepdims=True) + EPS) * w

    qu = _dot(rms(p[:, :MLA_Q_RANK], qn_ref[...]).astype(BF16), wuq_ref[...])
    kv = _dot(rms(p[:, MLA_Q_RANK:MLA_Q_RANK + MLA_KV_RANK], kvn_ref[...]).astype(BF16),
              wukv_ref[...])
    return qu, kv, p[:, MLA_Q_RANK + MLA_KV_RANK:]


def _mla_heads(qu, kv, kr, cos, sa, sb, gq_ref, gk_ref, qt_ref, k_ref, vt_ref):
    scale = MLA_QK ** -0.5
    ss_kr = jnp.sum(kr * kr, axis=-1, keepdims=True)
    gq, gk = gq_ref[...], gk_ref[...]
    for b in range(HEADS // 2):
        qr = qu[:, HEADS * MLA_NOPE + b * LANES:HEADS * MLA_NOPE + (b + 1) * LANES]
        low, ss_lo, ss_hi = _half_sums(qr * qr)
        inv = []
        for j in range(2):
            h = 2 * b + j
            qn = qu[:, h * MLA_NOPE:(h + 1) * MLA_NOPE]
            ss = jnp.sum(qn * qn, axis=-1, keepdims=True) + (ss_lo, ss_hi)[j]
            inv.append(lax.rsqrt(ss * (1.0 / MLA_QK) + EPS))
            qt_ref[h, 0:MLA_NOPE, :] = (qn * inv[j] * gq[0:1] * scale).T.astype(BF16)
        yrt = (_rope128(qr * jnp.where(low, inv[0], inv[1]) * gq[1:2], cos, sa, sb) * scale).T
        qt_ref[2 * b, MLA_NOPE:MLA_QK, :] = yrt[:MLA_ROPE].astype(BF16)
        qt_ref[2 * b + 1, MLA_NOPE:MLA_QK, :] = yrt[MLA_ROPE:].astype(BF16)
    low = lax.broadcasted_iota(jnp.int32, kr.shape, 1) < 64
    gk_rope = jnp.where(low, gk[1:2], 0.0)
    for h in range(HEADS):
        kn = kv[:, h * MLA_NOPE:(h + 1) * MLA_NOPE]
        inv = lax.rsqrt((jnp.sum(kn * kn, axis=-1, keepdims=True) + ss_kr) * (1.0 / MLA_QK) + EPS)
        k_ref[h, :, 0:MLA_NOPE] = (kn * inv * gk[0:1]).astype(BF16)
        yr = _rope128(kr * inv * gk_rope, cos, sa, sb)
        k_ref[h, :, MLA_NOPE:MLA_QK] = yr[:, :MLA_ROPE].astype(BF16)
        vt_ref[h] = kv[:, HEADS * MLA_NOPE + h * MLA_V:HEADS * MLA_NOPE + (h + 1) * MLA_V].T.astype(BF16)


def _even_front_kernel(x_ref, mod_ref, nw_ref, wm_ref, wa_ref, cos_ref, sa_ref, sb_ref, qn_ref, kvn_ref,
                       wuq_ref, wukv_ref, gq_ref, gk_ref, pm_ref, qt_ref, k_ref, vt_ref, *, m_ctx, rows):
    h = _mixer_input(x_ref, mod_ref, nw_ref, m_ctx, rows)
    qu, kv, kr = _mla_up(_dot(h, wa_ref[...]), qn_ref, kvn_ref, wuq_ref, wukv_ref)
    pm_ref[...] = _dot(h, wm_ref[...])
    _mla_heads(qu, kv, kr, cos_ref[...], sa_ref[...], sb_ref[...], gq_ref, gk_ref, qt_ref, k_ref, vt_ref)


def _even_front(xt, mod, nw, w_main, w_mla, tables, qn, kvn, wuq, wukv, gq, gk, m_ctx):
    t, d = xt.shape
    rows = _row_block(t, ROW_BLOCK // 2)
    rowb = lambda w: pl.BlockSpec((rows, w), lambda i: (i, 0))
    consts = (w_main, w_mla), (qn, kvn, wuq, wukv, gq, gk)
    return pl.pallas_call(
        functools.partial(_even_front_kernel, m_ctx=m_ctx, rows=rows),
        grid=(t // rows,),
        in_specs=[rowb(d), _resident(mod), _resident(nw)] + [_resident(a) for a in consts[0]]
                 + [rowb(LANES)] * 3 + [_resident(a) for a in consts[1]],
        out_specs=[rowb(w_main.shape[1]),
                   pl.BlockSpec((HEADS, MLA_QK, rows), lambda i: (0, 0, i)),
                   pl.BlockSpec((HEADS, rows, MLA_QK), lambda i: (0, i, 0)),
                   pl.BlockSpec((HEADS, ATT_V, rows), lambda i: (0, 0, i))],
        out_shape=[jax.ShapeDtypeStruct((t, w_main.shape[1]), F32),
                   jax.ShapeDtypeStruct((HEADS, MLA_QK, t), BF16),
                   jax.ShapeDtypeStruct((HEADS, t, MLA_QK), BF16),
                   jax.ShapeDtypeStruct((HEADS, ATT_V, t), BF16)],
        compiler_params=_cparams(("parallel",)),
        name="even_front",
    )(xt, mod, nw, w_main, w_mla, *tables, qn, kvn, wuq, wukv, gq, gk)


def _pv_and_sums(vt, p):
    sums = p.reshape(p.shape[0] // SUBLANES, SUBLANES, p.shape[1]).sum(axis=0)
    return jnp.concatenate([_dot(vt, p.astype(BF16)), sums], axis=0)


def _attend(items, first):
    scores = [_dot(k_ref[rows, :], qt) for qt, k_ref, _, _, _, rows in items]
    for (_, _, vt_ref, acc_sc, m_sc, rows), s in zip(items, scores):
        vt = vt_ref[:, rows]
        if m_sc is None:
            pv = _pv_and_sums(vt, jnp.exp(s))
            acc_sc[...] = pv if first else acc_sc[...] + pv
        else:
            m_new = jnp.max(s, axis=0, keepdims=True)
            if not first:
                m_prev = m_sc[...]
                m_new = jnp.maximum(m_prev, m_new)
            pv = _pv_and_sums(vt, jnp.exp(s - m_new))
            acc_sc[...] = pv if first else jnp.exp(m_prev - m_new) * acc_sc[...] + pv
            m_sc[...] = m_new


def _attend_all(qts, k_refs, vt_refs, accs, ms, m_ctx, bq, bk, n_blocks, finish):
    pairs = list(zip(qts, k_refs, vt_refs, accs, ms))
    ctx_items = [p + (slice(0, m_ctx),) for p in pairs]
    split = max(1, 2 // len(pairs))
    sub = bk // split
    is_latent = pl.program_id(1) >= m_ctx // bq

    @pl.when(jnp.logical_not(is_latent))
    def _():
        _attend(ctx_items, True)
        finish()

    @pl.when(is_latent)
    def _():
        _attend(ctx_items, True)
        for j in range(n_blocks):
            _attend([p + (slice(m_ctx + j * bk + u * sub, m_ctx + j * bk + (u + 1) * sub),)
                     for p in pairs for u in range(split)], False)
        finish()


def _normalised(acc_sc):
    acc = acc_sc[...]
    return (acc[:ATT_V] / jnp.sum(acc[ATT_V:], axis=0, keepdims=True)).T


def _mla_attn_kernel(qt_ref, k_ref, vt_ref, o_ref, a0, a1, *m_sc, m_ctx, bq, bk, n_blocks):
    ms = list(m_sc) if m_sc else [None, None]

    def finish():
        o_ref[:, 0:MLA_V] = _normalised(a0)
        o_ref[:, MLA_V:] = _normalised(a1)

    _attend_all([qt_ref[0], qt_ref[1]], [k_ref.at[0], k_ref.at[1]], [vt_ref.at[0], vt_ref.at[1]], [a0, a1], ms,
                m_ctx, bq, bk, n_blocks, finish)


def _attn_blocks(t, m_ctx):
    bq = ATT_BQ
    assert m_ctx % bq == 0 and t % bq == 0
    n_lat = t - m_ctx
    bk = _row_block(n_lat, ATT_BK)
    return bq, bk, n_lat // bk


def _attn_scratch(bq, n_pairs, shifted):
    return [pltpu.VMEM((ACC_ROWS, bq), F32)] * n_pairs + ([pltpu.VMEM((1, bq), F32)] * n_pairs if shifted else [])


def _by_score_bound(bound, attend, *operands):
    return lax.cond(bound <= SCORE_LIMIT,
                    functools.partial(attend, shifted=False),
                    functools.partial(attend, shifted=True), *operands)


def _mla_attn(qt, k, vt, m_ctx, shifted):
    t = k.shape[1]
    bq, bk, n_blocks = _attn_blocks(t, m_ctx)
    return pl.pallas_call(
        functools.partial(_mla_attn_kernel, m_ctx=m_ctx, bq=bq, bk=bk, n_blocks=n_blocks),
        grid=(HEADS // 2, t // bq),
        in_specs=[pl.BlockSpec((2, MLA_QK, bq), lambda h, i: (h, 0, i)),
                  pl.BlockSpec((2, t, MLA_QK), lambda h, i: (h, 0, 0), pipeline_mode=pl.Buffered(1)),
                  pl.BlockSpec((2, ATT_V, t), lambda h, i: (h, 0, 0), pipeline_mode=pl.Buffered(1))],
        out_specs=pl.BlockSpec((bq, 2 * MLA_V), lambda h, i: (i, h)),
        out_shape=jax.ShapeDtypeStruct((t, HEADS * MLA_V), F32),
        scratch_shapes=_attn_scratch(bq, 2, shifted),
        compiler_params=_cparams(("parallel", "arbitrary")),
        name="mla_attn_shifted" if shifted else "mla_attn",
    )(qt, k, vt)


def _diff_attn_kernel(q1_ref, q2_ref, k1_ref, k2_ref, v_ref, lam_ref, sub_ref, o_ref, a1, a2, *m_sc,
                      m_ctx, bq, bk, n_blocks, out_scale):
    ms = list(m_sc) if m_sc else [None, None]

    def finish():
        d = _normalised(a1) - lam_ref[...] * _normalised(a2)
        y = d * lax.rsqrt(jnp.mean(d * d, axis=-1, keepdims=True) + EPS)
        o_ref[...] = y * sub_ref[...] * out_scale

    _attend_all([q1_ref[0], q2_ref[0]], [k1_ref.at[0], k2_ref.at[0]], [v_ref.at[0]] * 2, [a1, a2], ms,
                m_ctx, bq, bk, n_blocks, finish)


def _diff_attn(dqt, dk, dvt, lam, subln, m_ctx, out_scale, shifted):
    t = dk.shape[1]
    bq, bk, n_blocks = _attn_blocks(t, m_ctx)
    qspec = lambda c: pl.BlockSpec((1, DA_DIM, bq), lambda h, i: (2 * h + c, 0, i))
    kspec = lambda c: pl.BlockSpec((1, t, DA_DIM), lambda h, i: (2 * h + c, 0, 0))
    vec = pl.BlockSpec((1, DA_V), lambda h, i: (0, 0))
    return pl.pallas_call(
        functools.partial(_diff_attn_kernel, m_ctx=m_ctx, bq=bq, bk=bk, n_blocks=n_blocks,
                          out_scale=out_scale),
        grid=(HEADS, t // bq),
        in_specs=[qspec(0), qspec(1), kspec(0), kspec(1),
                  pl.BlockSpec((1, ATT_V, t), lambda h, i: (h, 0, 0)), vec, vec],
        out_specs=pl.BlockSpec((bq, DA_V), lambda h, i: (i, h)),
        out_shape=jax.ShapeDtypeStruct((t, HEADS * DA_V), F32),
        scratch_shapes=_attn_scratch(bq, 2, shifted),
        compiler_params=_cparams(("parallel", "arbitrary")),
        name="diff_attn_shifted" if shifted else "diff_attn",
    )(dqt, dqt, dk, dk, dvt, lam, subln)


ODD_ATT_W = 3 * HEADS * 2 * DA_DIM + 2 * HEADS * RT_K


def _odd_prep(p, cos, sa, sb, gq_ref, gk_ref, dqt_ref, dk_ref, dvt_ref, rq_ref, rk_ref):
    da_w = HEADS * 2 * DA_DIM

    def sub_rms(x, g):
        low, ss_lo, ss_hi = _half_sums(x * x)
        inv = jnp.where(low, lax.rsqrt(ss_lo * (1.0 / DA_DIM) + EPS), lax.rsqrt(ss_hi * (1.0 / DA_DIM) + EPS))
        return x * inv * g

    for h in range(HEADS):
        col = h * LANES
        yqt = (_rope128(sub_rms(p[:, col:col + LANES], gq_ref[...]), cos, sa, sb) * (DA_DIM ** -0.5)).T
        dqt_ref[2 * h] = yqt[:DA_DIM].astype(BF16)
        dqt_ref[2 * h + 1] = yqt[DA_DIM:].astype(BF16)
        yk = _rope128(sub_rms(p[:, da_w + col:da_w + col + LANES], gk_ref[...]), cos, sa, sb)
        dk_ref[2 * h] = yk[:, :DA_DIM].astype(BF16)
        dk_ref[2 * h + 1] = yk[:, DA_DIM:].astype(BF16)
        dvt_ref[h] = p[:, 2 * da_w + col:2 * da_w + col + LANES].T.astype(BF16)
    r0 = 2 * da_w + HEADS * DA_V
    for b in range(HEADS // 2):
        col = r0 + b * LANES
        yq = _rope128(p[:, col:col + LANES], cos, sa, sb)
        rq_ref[2 * b] = yq[:, :RT_K]
        rq_ref[2 * b + 1] = yq[:, RT_K:]
        col = r0 + HEADS * RT_K + b * LANES
        yk = _rope128(p[:, col:col + LANES] * (RT_K ** -0.5), cos, sa, sb)
        rk_ref[2 * b] = yk[:, :RT_K]
        rk_ref[2 * b + 1] = yk[:, RT_K:]


def _odd_front_kernel(x_ref, mod_ref, nw_ref, wa_ref, wr_ref, cos_ref, sa_ref, sb_ref, gq_ref, gk_ref,
                      dqt_ref, dk_ref, dvt_ref, rq_ref, rk_ref, rest_ref, *, m_ctx, rows):
    h = _mixer_input(x_ref, mod_ref, nw_ref, m_ctx, rows)
    p_att = _dot(h, wa_ref[...])
    rest_ref[...] = _dot(h, wr_ref[...])
    _odd_prep(p_att, cos_ref[...], sa_ref[...], sb_ref[...], gq_ref, gk_ref,
              dqt_ref, dk_ref, dvt_ref, rq_ref, rk_ref)


def _odd_front(xt, mod, nw, w_in, tables, gq, gk, m_ctx):
    t, d = xt.shape
    rows = _row_block(t, ROW_BLOCK // 2)
    w_att, w_rest = w_in[:, :ODD_ATT_W], w_in[:, ODD_ATT_W:]
    rowb = lambda w: pl.BlockSpec((rows, w), lambda i: (i, 0))
    hm = lambda n, w: pl.BlockSpec((n, rows, w), lambda i: (0, i, 0))
    return pl.pallas_call(
        functools.partial(_odd_front_kernel, m_ctx=m_ctx, rows=rows),
        grid=(t // rows,),
        in_specs=[rowb(d), _resident(mod), _resident(nw), _resident(w_att), _resident(w_rest)]
                 + [rowb(LANES)] * 3 + [_resident(gq), _resident(gk)],
        out_specs=[pl.BlockSpec((2 * HEADS, DA_DIM, rows), lambda i: (0, 0, i)),
                   hm(2 * HEADS, DA_DIM),
                   pl.BlockSpec((HEADS, ATT_V, rows), lambda i: (0, 0, i)), hm(HEADS, RT_K), hm(HEADS, RT_K),
                   rowb(w_rest.shape[1])],
        out_shape=[jax.ShapeDtypeStruct((2 * HEADS, DA_DIM, t), BF16),
                   jax.ShapeDtypeStruct((2 * HEADS, t, DA_DIM), BF16),
                   jax.ShapeDtypeStruct((HEADS, ATT_V, t), BF16),
                   jax.ShapeDtypeStruct((HEADS, t, RT_K), F32),
                   jax.ShapeDtypeStruct((HEADS, t, RT_K), F32),
                   jax.ShapeDtypeStruct((t, w_rest.shape[1]), F32)],
        compiler_params=_cparams(("parallel",)),
        name="odd_front",
    )(xt, mod, nw, w_att, w_rest, *tables, gq, gk)


def _outproj_kernel(x_ref, att_ref, of_ref, ob_ref, gate_ref, nrm_ref, wa_ref, wr_ref, mod_ref, o_ref,
                    *, m_ctx, rows):
    o = of_ref[...] + ob_ref[...]
    gate = gate_ref[...]
    rec = []
    for h in range(HEADS):
        oh = o[:, h * LANES:(h + 1) * LANES]
        inv = lax.rsqrt(jnp.mean(oh * oh, axis=-1, keepdims=True) + EPS)
        rec.append(oh * inv * nrm_ref[...] * _silu(gate[:, h * LANES:(h + 1) * LANES]))
    rec = jnp.concatenate(rec, axis=-1).astype(BF16)
    y = _dot(att_ref[...].astype(BF16), wa_ref[...]) + _dot(rec, wr_ref[...])
    g1 = _row_mod(mod_ref, 2, pl.program_id(0) * rows, rows, m_ctx)
    o_ref[...] = x_ref[...] + g1 * y


def _outproj(xt, att, o_f, o_b, gate_arr, gate_block, nrm, w_att, w_rec, mod, m_ctx):
    t, d = xt.shape
    rows = _row_block(t, ROW_BLOCK // 2)
    w = att.shape[1]
    full = lambda a: pl.BlockSpec(a.shape, lambda i: (0,) * a.ndim)
    rowb = lambda c: pl.BlockSpec((rows, c), lambda i: (i, 0))
    return pl.pallas_call(
        functools.partial(_outproj_kernel, m_ctx=m_ctx, rows=rows),
        grid=(t // rows,),
        in_specs=[rowb(d), rowb(w), rowb(w), rowb(w),
                  pl.BlockSpec((rows, w), lambda i: (i, gate_block)),
                  full(nrm), full(w_att), full(w_rec), full(mod)],
        out_specs=rowb(d),
        out_shape=jax.ShapeDtypeStruct((t, d), F32),
        compiler_params=_cparams(("parallel",)),
        name="out_proj",
    )(xt, att, o_f, o_b, gate_arr, nrm, w_att, w_rec, mod)


def _mlp_kernel(x_ref, mod_ref, nw_ref, w1_ref, w2_ref, o_ref, h_sc, acc_sc, *, m_ctx, rows):
    j = pl.program_id(1)
    row0 = pl.program_id(0) * rows

    @pl.when(j == 0)
    def _():
        h_sc[...] = _modnorm(x_ref[...], nw_ref[...],
                             _row_mod(mod_ref, 4, row0, rows, m_ctx),
                             _row_mod(mod_ref, 3, row0, rows, m_ctx)).astype(BF16)
        acc_sc[...] = jnp.zeros_like(acc_sc)

    u = jnp.maximum(_dot(h_sc[...], w1_ref[...]), 0.0)
    acc_sc[...] += _dot((u * u).astype(BF16), w2_ref[...])

    @pl.when(j == pl.num_programs(1) - 1)
    def _():
        o_ref[...] = x_ref[...] + _row_mod(mod_ref, 5, row0, rows, m_ctx) * acc_sc[...]


def _mlp(xt, mod, nw, w1, w2, m_ctx):
    t, d = xt.shape
    hid = w1.shape[1]
    rows = _row_block(t, ROW_BLOCK)
    th = 512
    full = lambda a: pl.BlockSpec(a.shape, lambda i, j: (0,) * a.ndim)
    return pl.pallas_call(
        functools.partial(_mlp_kernel, m_ctx=m_ctx, rows=rows),
        grid=(t // rows, hid // th),
        in_specs=[pl.BlockSpec((rows, d), lambda i, j: (i, 0)), full(mod), full(nw),
                  pl.BlockSpec((d, th), lambda i, j: (0, j)),
                  pl.BlockSpec((th, d), lambda i, j: (j, 0))],
        out_specs=pl.BlockSpec((rows, d), lambda i, j: (i, 0)),
        out_shape=jax.ShapeDtypeStruct((t, d), F32),
        scratch_shapes=[pltpu.VMEM((rows, d), BF16), pltpu.VMEM((rows, d), F32)],
        compiler_params=_cparams(("parallel", "arbitrary")),
        name="mlp",
    )(xt, mod, nw, w1, w2)


def _head_major(w, parts):
    k = w.shape[0]
    wh = w.reshape(k, HEADS, sum(parts))
    out, off = [], 0
    for width in parts:
        out.append(wh[:, :, off:off + width].reshape(k, HEADS * width))
        off += width
    return jnp.concatenate(out, axis=1)


def kernel(x, c, ctx, c_ctx, ada_w, ada_b, norm_w, w_o, mlp_w1, mlp_w2, a_w_in, hg_lb, hg_norm, mla_q_norm,
           mla_kv_norm, mla_w_uq, mla_w_ukv, mla_qk_q, mla_qk_k, c_w_in, da_lambda, da_qk_q, da_qk_k,
           da_subln, rt_decay, rt_norm):
    assert x.shape[0] == 1 and ctx.shape[0] == 1
    n, d = x.shape[1], x.shape[2]
    m_ctx = ctx.shape[1]
    depth = ada_w.shape[0]
    xt = jnp.concatenate([ctx[0], x[0]], axis=0)

    cvec = jnp.zeros((8, d), F32).at[0].set(c_ctx).at[1].set(c[0])
    mods = _ada_table(cvec, ada_w, ada_b)[:, :2].reshape(depth, 2, 6, d)
    tables = _rope_tables(n, m_ctx)

    lb = jnp.cumsum(jax.nn.softmax(hg_lb.astype(F32), axis=0), axis=0)
    lb = lb - lb[:1]
    log_lb = jnp.log(lb).reshape(-1, 2, 1, HG_W)
    log1m_lb = jnp.log1p(-lb).reshape(-1, 2, 1, HG_W)

    for l in range(depth):
        j = l // 2
        mod = mods[l]
        nw = norm_w[l]
        wo = w_o[l].astype(BF16)
        if l % 2 == 0:
            w_in = a_w_in[j]
            w_main = w_in[:, :5 * HG_W].astype(BF16)
            w_mla = jnp.pad(w_in[:, 5 * HG_W:], ((0, 0), (0, MLA_IN - (w_in.shape[1] - 5 * HG_W)))).astype(BF16)
            gq = mla_qk_q[j]
            gk = mla_qk_k[j]
            p_main, qt, k, vt = _even_front(
                xt, mod, nw[0:1], w_main, w_mla, tables, mla_q_norm[j][None], mla_kv_norm[j][None],
                _head_major(mla_w_uq[j], (MLA_NOPE, MLA_ROPE)).astype(BF16),
                _head_major(mla_w_ukv[j], (MLA_NOPE, MLA_V)).astype(BF16),
                jnp.stack([gq[:MLA_NOPE], jnp.tile(gq[MLA_NOPE:], 2)]),
                jnp.stack([gk[:MLA_NOPE], jnp.tile(gk[MLA_NOPE:], 2)]), m_ctx)
            o_f, o_b = _hgrn2(p_main, log_lb[j], log1m_lb[j], m_ctx)
            bound = 1.02 * MLA_QK ** 0.5 * jnp.max(jnp.abs(gq)) * jnp.max(jnp.abs(gk))
            att = _by_score_bound(bound, functools.partial(_mla_attn, m_ctx=m_ctx), qt, k, vt)
            xt = _outproj(xt, att, o_f, o_b, p_main, 4, hg_norm[j][None], wo[HG_W:], wo[:HG_W], mod, m_ctx)
        else:
            dq, dkt, dv, rq, rk, p = _odd_front(xt, mod, nw[0:1], c_w_in[j].astype(BF16), tables,
                                                 jnp.tile(da_qk_q[j], 2)[None], jnp.tile(da_qk_k[j], 2)[None], m_ctx)
            lam_init = 0.8 - 0.6 * math.exp(-0.3 * l)
            lf = da_lambda[j].astype(F32)
            lam = jnp.exp(jnp.sum(lf[0] * lf[1])) - jnp.exp(jnp.sum(lf[2] * lf[3])) + lam_init
            bound = 1.02 * DA_DIM ** 0.5 * jnp.max(jnp.abs(da_qk_q[j])) * jnp.max(jnp.abs(da_qk_k[j]))
            att = _by_score_bound(
                bound, functools.partial(_diff_attn, m_ctx=m_ctx, out_scale=1.0 - lam_init),
                dq, dkt, dv, jnp.full((1, DA_V), lam, F32), da_subln[j][None])
            lg = jax.nn.log_sigmoid(rt_decay[j].astype(F32))
            lg = jnp.broadcast_to(lg[:, :, None, None], (2, HEADS, 8, LANES))
            r_f, r_b = _retention(rq, rk, p, 0, lg, m_ctx)
            xt = _outproj(xt, att, r_f, r_b, p, 1, rt_norm[j][None], wo[:HG_W], wo[HG_W:], mod, m_ctx)
        xt = _mlp(xt, mod, nw[1:2], mlp_w1[l].astype(BF16), mlp_w2[l].astype(BF16), m_ctx)
    return xt[m_ctx:][None]
```

```python
import functools
import math

import numpy as np
import jax
import jax.numpy as jnp
from jax import lax
from jax.experimental import pallas as pl
from jax.experimental.pallas import tpu as pltpu

F32 = jnp.float32
BF16 = jnp.bfloat16

GRID_W = 64
ROPE_DIM = 64
ROPE_BASE = 10000.0
EPS = 1e-6
HEADS = 4
HG_DIM = 128
HG_W = HEADS * HG_DIM
MLA_NOPE = 128
MLA_ROPE = ROPE_DIM
MLA_V = 128
MLA_QK = MLA_NOPE + MLA_ROPE
MLA_Q_RANK = 384
MLA_KV_RANK = 256
DA_DIM = ROPE_DIM
DA_V = 2 * DA_DIM
RT_K = ROPE_DIM
RT_V = 128
MLA_IN = 768

LANES = 128
VMEM_LIMIT = 56 * 1024 * 1024
ROW_BLOCK = 640
ATT_BQ = 256
ATT_BK = 4096
SCAN_CHUNK = 64
RET_CHUNK = 128
RET_SUB = 2
SUBLANES = 8
ATT_V = 128
ACC_ROWS = ATT_V + SUBLANES
SCORE_LIMIT = 40.0


def _cparams(sem):
    return pltpu.CompilerParams(dimension_semantics=sem, vmem_limit_bytes=VMEM_LIMIT)


def _row_block(t, target):
    best = None
    for r in range(LANES, min(t, target) + 1, LANES):
        if t % r == 0:
            best = r
    assert best is not None, t
    return best


def _dot(a, b):
    return jnp.dot(a, b, preferred_element_type=F32)


def _dot_nt(a, b):
    return lax.dot_general(a, b, (((1,), (1,)), ((), ())), preferred_element_type=F32)


def _dot_tn(a, b):
    return lax.dot_general(a, b, (((0,), (0,)), ((), ())), preferred_element_type=F32)


def _silu(x):
    return x * (1.0 / (1.0 + jnp.exp(-x)))


def _row_mod(mod_ref, k, row0, rows, m_ctx):
    r = row0 + lax.broadcasted_iota(jnp.int32, (rows, 1), 0)
    return jnp.where(r < m_ctx, mod_ref[0, k:k + 1, :], mod_ref[1, k:k + 1, :])


def _modnorm(x, nw, sc, sh):
    y = x * lax.rsqrt(jnp.mean(x * x, axis=-1, keepdims=True) + EPS)
    return y * nw * (1.0 + sc) + sh


def _ada_kernel(c_ref, w_ref, b_ref, o_ref):
    cv = c_ref[...]
    o_ref[0] = jnp.dot(_silu(cv), w_ref[0], precision=lax.Precision.HIGHEST,
                       preferred_element_type=F32) + b_ref[0]


def _ada_table(cvec, ada_w, ada_b):
    depth, d, d6 = ada_w.shape
    tn = d6 // 4
    return pl.pallas_call(
        _ada_kernel,
        grid=(depth, d6 // tn),
        in_specs=[pl.BlockSpec((8, d), lambda l, j: (0, 0)),
                  pl.BlockSpec((1, d, tn), lambda l, j: (l, 0, j)),
                  pl.BlockSpec((1, 1, tn), lambda l, j: (l, 0, j))],
        out_specs=pl.BlockSpec((1, 8, tn), lambda l, j: (l, 0, j)),
        out_shape=jax.ShapeDtypeStruct((depth, 8, d6), F32),
        compiler_params=_cparams(("parallel", "parallel")),
        name="ada_table",
    )(cvec, ada_w, ada_b.reshape(depth, 1, d6))


def _mixer_input(x_ref, mod_ref, nw_ref, m_ctx, rows):
    row0 = pl.program_id(0) * rows
    return _modnorm(x_ref[...], nw_ref[...],
                    _row_mod(mod_ref, 1, row0, rows, m_ctx),
                    _row_mod(mod_ref, 0, row0, rows, m_ctx)).astype(BF16)


def _resident(a):
    return pl.BlockSpec(a.shape, lambda i: (0,) * a.ndim, pipeline_mode=pl.Buffered(1))


def _hier_tables(c, reverse):
    levels = int(math.log2(c))
    assert 1 << levels == c
    cums = np.zeros(((levels + 1) * c, c), np.float32)
    roles = np.zeros((levels, c, LANES), np.float32)
    masks = np.zeros((levels + 1, c, c), np.float32)
    for li in range(levels):
        h = c >> (li + 1)
        for t in range(c):
            base = (t // (2 * h)) * 2 * h
            late = (t - base) >= h
            if not reverse:
                if late:
                    cums[li * c + t, base + h:t + 1] = 1.0
                else:
                    cums[li * c + t, t + 1:base + h] = 1.0
            else:
                if late:
                    cums[li * c + t, base + h:t] = 1.0
                else:
                    cums[li * c + t, t:base + h] = 1.0
            is_query = late != reverse
            roles[li, t, :] = 1.0 if is_query else 0.0
        for t in range(c):
            for s in range(c):
                same = (t // (2 * h)) == (s // (2 * h))
                if same and roles[li, t, 0] == 1.0 and roles[li, s, 0] == 0.0:
                    masks[li, t, s] = 1.0
    for t in range(c):
        if not reverse:
            cums[levels * c + t, :t + 1] = 1.0
        else:
            cums[levels * c + t, t:] = 1.0
    masks[levels] = np.eye(c, dtype=np.float32)
    return cums, roles, masks


def _split3(x):
    hi = x.astype(BF16)
    r1 = x - hi.astype(F32)
    mid = r1.astype(BF16)
    lo = (r1 - mid.astype(F32)).astype(BF16)
    return hi, mid, lo


def _gated_chunks(sides, c):
    levels = sides[0][5].shape[0]
    chains = []
    for q_all, kk_all, v_all, logf_all, cum_ref, role_ref, mask_ref, st_ref, o_ref, reverse in sides:
        x_all = _dot(cum_ref[...], jnp.concatenate(_split3(logf_all), axis=0))
        for h in range(HEADS):
            sl = slice(h * HG_DIM, (h + 1) * HG_DIM)
            chains.append((x_all[:, sl], q_all[:, sl], kk_all[:, sl], v_all[:, sl].astype(BF16),
                           role_ref, mask_ref, st_ref, o_ref, reverse, h, sl))
    acc = [_dot_nt(q.astype(BF16), kk.astype(BF16)) * mask_ref[levels]
           for _, q, kk, _, _, mask_ref, *_ in chains]
    for li in range(levels):
        for n, (x, q, kk, _, role_ref, mask_ref, *_) in enumerate(chains):
            z = (jnp.where(role_ref[li] > 0.5, q, kk) * jnp.exp(x[li * c:(li + 1) * c])).astype(BF16)
            acc[n] = acc[n] + _dot_nt(z, z) * mask_ref[li]
    for a, (x, q, kk, vb, _, _, st_ref, o_ref, reverse, h, sl) in zip(acc, chains):
        run = x[levels * c:(levels + 1) * c]
        tot = run[0:1] if reverse else run[c - 1:c]
        st = st_ref[h]
        o_ref[:, sl] = _dot(a.astype(BF16), vb) + _dot_nt((q * jnp.exp(run)).astype(BF16), st.astype(BF16))
        kd = (kk * jnp.exp(tot - run)).astype(BF16)
        st_ref[h] = st * jnp.exp(tot) + _dot_tn(vb, kd)


def _hgrn2_gate(z, log_lb, log1m_lb):
    soft = jnp.log1p(jnp.exp(-jnp.abs(z)))
    b = log1m_lb + (jnp.minimum(z, 0.0) - soft)
    logf = jnp.maximum(log_lb, b) + jnp.log1p(jnp.exp(-jnp.abs(log_lb - b)))
    return logf, jnp.exp(log1m_lb + (jnp.minimum(-z, 0.0) - soft))


def _hgrn2_kernel(qf_ref, zf_ref, vf_ref, qb_ref, zb_ref, vb_ref, llb_ref, l1m_ref,
                  cumf_ref, rolef_ref, maskf_ref, cumb_ref, roleb_ref, maskb_ref,
                  of_ref, ob_ref, sf_ref, sb_ref, *, c):
    @pl.when(pl.program_id(0) == 0)
    def _():
        sf_ref[...] = jnp.zeros_like(sf_ref)
        sb_ref[...] = jnp.zeros_like(sb_ref)

    logf_f, kk_f = _hgrn2_gate(zf_ref[...], llb_ref[0], l1m_ref[0])
    logf_b, kk_b = _hgrn2_gate(zb_ref[...], llb_ref[1], l1m_ref[1])
    _gated_chunks(
        [(_silu(qf_ref[...]), kk_f, vf_ref[...], logf_f, cumf_ref, rolef_ref, maskf_ref, sf_ref, of_ref, False)], c)
    _gated_chunks(
        [(_silu(qb_ref[...]), kk_b, vb_ref[...], logf_b, cumb_ref, roleb_ref, maskb_ref, sb_ref, ob_ref, True)], c)


def _bwd_chunk(i, mc, steps):
    return jnp.where(i < mc, mc - 1 - i, steps - 1 - i + mc)


def _hgrn2(p_main, log_lb, log1m_lb, m_ctx):
    t = p_main.shape[0]
    c = SCAN_CHUNK
    steps, mc = t // c, m_ctx // c
    tabs = [jnp.asarray(a) for rev in (False, True) for a in _hier_tables(c, rev)]
    for k in (0, 3):
        tabs[k] = jnp.tile(tabs[k], (1, 3)).astype(BF16)
    fwd = lambda sec: pl.BlockSpec((c, HG_W), lambda i: (i, sec))
    bwd = lambda sec: pl.BlockSpec((c, HG_W), lambda i: (_bwd_chunk(i, mc, steps), sec))
    full = lambda a: pl.BlockSpec(a.shape, lambda i: (0,) * a.ndim)
    return pl.pallas_call(
        functools.partial(_hgrn2_kernel, c=c),
        grid=(steps,),
        in_specs=[fwd(0), fwd(1), fwd(3), bwd(0), bwd(2), bwd(3), full(log_lb), full(log1m_lb)]
                 + [full(a) for a in tabs],
        out_specs=[pl.BlockSpec((c, HG_W), lambda i: (i, 0)),
                   pl.BlockSpec((c, HG_W), lambda i: (_bwd_chunk(i, mc, steps), 0))],
        out_shape=[jax.ShapeDtypeStruct((t, HG_W), F32)] * 2,
        scratch_shapes=[pltpu.VMEM((HEADS, HG_DIM, HG_DIM), F32)] * 2,
        compiler_params=_cparams(("arbitrary",)),
        name="hgrn2_scan",
    )(p_main, p_main, p_main, p_main, p_main, p_main, log_lb, log1m_lb, *tabs)


def _ret_scores(q, k, lg, c, reverse):
    t = lax.broadcasted_iota(jnp.int32, (c, c), 0)
    s = lax.broadcasted_iota(jnp.int32, (c, c), 1)
    dlt = (s - t) if reverse else (t - s)
    dec = jnp.where(dlt >= 0, jnp.exp(lg[:, :c] * jnp.maximum(dlt, 0).astype(F32)), 0.0)
    return (_dot_nt(q.astype(BF16), k.astype(BF16)) * dec).astype(BF16)


def _ret_finish(a, q, k, v, lg, st_ref, c, reverse):
    r = lax.broadcasted_iota(jnp.int32, (c, RT_K), 0).astype(F32)
    lk = lg[:, :RT_K]
    qdec = jnp.exp(lk * ((c - r) if reverse else (r + 1.0)))
    kdec = jnp.exp(lk * (r if reverse else (c - 1.0 - r)))
    st = st_ref[...]
    vb = v.astype(BF16)
    o = _dot(a, vb) + _dot_nt((q * qdec).astype(BF16), st.astype(BF16))
    st_ref[...] = st * jnp.exp(lk * float(c)) + _dot_tn(vb, (k * kdec).astype(BF16))
    return o


def _ret_kernel(qf_ref, kf_ref, vf_ref, qb_ref, kb_ref, vb_ref, lg_ref,
                of_ref, ob_ref, sf_ref, sb_ref, *, c, sub):
    @pl.when(pl.program_id(0) == 0)
    def _():
        sf_ref[...] = jnp.zeros_like(sf_ref)
        sb_ref[...] = jnp.zeros_like(sb_ref)

    sides = ((qf_ref, kf_ref, vf_ref, of_ref, sf_ref), (qb_ref, kb_ref, vb_ref, ob_ref, sb_ref))
    chains = [(sides[d], d, h, pl.ds((sub - 1 - u if d else u) * c, c))
              for u in range(sub) for h in range(HEADS) for d in range(2)]
    scores = [_ret_scores(q_ref[h, rows], k_ref[h, rows], lg_ref[d, h, 0:1, :], c, d == 1)
              for (q_ref, k_ref, _, _, _), d, h, rows in chains]
    for a, ((q_ref, k_ref, v_ref, o_ref, s_ref), d, h, rows) in zip(scores, chains):
        sl = slice(h * RT_V, (h + 1) * RT_V)
        o_ref[rows, sl] = _ret_finish(a, q_ref[h, rows], k_ref[h, rows], v_ref[rows, sl], lg_ref[d, h, 0:1, :],
                                      s_ref.at[h], c, d == 1)


def _retention(rq, rk, p, v_block, lg, m_ctx):
    t = p.shape[0]
    c, sub = RET_CHUNK, RET_SUB
    rows = c * sub
    steps, mc = t // rows, m_ctx // rows
    assert steps * rows == t and mc * rows == m_ctx
    width = HEADS * RT_V
    fq = pl.BlockSpec((HEADS, rows, RT_K), lambda i: (0, i, 0))
    bq = pl.BlockSpec((HEADS, rows, RT_K), lambda i: (0, _bwd_chunk(i, mc, steps), 0))
    fv = pl.BlockSpec((rows, width), lambda i: (i, v_block))
    bv = pl.BlockSpec((rows, width), lambda i: (_bwd_chunk(i, mc, steps), v_block))
    return pl.pallas_call(
        functools.partial(_ret_kernel, c=c, sub=sub),
        grid=(steps,),
        in_specs=[fq, fq, fv, bq, bq, bv, pl.BlockSpec(lg.shape, lambda i: (0, 0, 0, 0))],
        out_specs=[pl.BlockSpec((rows, width), lambda i: (i, 0)),
                   pl.BlockSpec((rows, width), lambda i: (_bwd_chunk(i, mc, steps), 0))],
        out_shape=[jax.ShapeDtypeStruct((t, width), F32)] * 2,
        scratch_shapes=[pltpu.VMEM((HEADS, RT_V, RT_K), F32)] * 2,
        compiler_params=_cparams(("arbitrary",)),
        name="retention_scan",
    )(rq, rk, p, rq, rk, p, lg)


def _rope128(y, cos, sin_a, sin_b):
    return y * cos + pltpu.roll(y, LANES - 16, 1) * sin_a + pltpu.roll(y, 16, 1) * sin_b


def _rope_tables(n, m_ctx):
    rows = n // GRID_W
    row = jnp.repeat(jnp.arange(rows, dtype=F32), GRID_W)
    col = jnp.tile(jnp.arange(GRID_W, dtype=F32), rows)
    quarter = ROPE_DIM // 4
    inv_freq = ROPE_BASE ** (-jnp.arange(quarter, dtype=F32) / quarter)
    ang_r = row[:, None] * inv_freq
    ang_c = col[:, None] * inv_freq
    ang = jnp.concatenate([ang_r, ang_r, ang_c, ang_c], axis=-1)
    cos, sin = jnp.cos(ang), jnp.sin(ang)
    first = (jnp.arange(ROPE_DIM) % 32) < 16
    sin_a = jnp.where(first, -sin, 0.0)
    sin_b = jnp.where(first, 0.0, sin)
    pad = lambda a, v: jnp.tile(jnp.concatenate([jnp.full((m_ctx, ROPE_DIM), v, F32), a], axis=0), (1, 2))
    return pad(cos, 1.0), pad(sin_a, 0.0), pad(sin_b, 0.0)


def _half_sums(sq):
    low = lax.broadcasted_iota(jnp.int32, sq.shape, 1) < 64
    lo = jnp.sum(jnp.where(low, sq, 0.0), axis=-1, keepdims=True)
    return low, lo, jnp.sum(sq, axis=-1, keepdims=True) - lo


def _mla_up(p, qn_ref, kvn_ref, wuq_ref, wukv_ref):
    def rms(x, w):
        return x * lax.rsqrt(jnp.mean(x * x, axis=-1, keepdims=True) + EPS) * w

    qu = _dot(rms(p[:, :MLA_Q_RANK], qn_ref[...]).astype(BF16), wuq_ref[...])
    kv = _dot(rms(p[:, MLA_Q_RANK:MLA_Q_RANK + MLA_KV_RANK], kvn_ref[...]).astype(BF16),
              wukv_ref[...])
    return qu, kv, p[:, MLA_Q_RANK + MLA_KV_RANK:]


def _mla_heads(qu, kv, kr, cos, sa, sb, gq_ref, gk_ref, qt_ref, k_ref, vt_ref):
    scale = MLA_QK ** -0.5
    ss_kr = jnp.sum(kr * kr, axis=-1, keepdims=True)
    gq, gk = gq_ref[...], gk_ref[...]
    for b in range(HEADS // 2):
        qr = qu[:, HEADS * MLA_NOPE + b * LANES:HEADS * MLA_NOPE + (b + 1) * LANES]
        low, ss_lo, ss_hi = _half_sums(qr * qr)
        inv = []
        for j in range(2):
            h = 2 * b + j
            qn = qu[:, h * MLA_NOPE:(h + 1) * MLA_NOPE]
            ss = jnp.sum(qn * qn, axis=-1, keepdims=True) + (ss_lo, ss_hi)[j]
            inv.append(lax.rsqrt(ss * (1.0 / MLA_QK) + EPS))
            qt_ref[h, 0:MLA_NOPE, :] = (qn * inv[j] * gq[0:1] * scale).T.astype(BF16)
        yrt = (_rope128(qr * jnp.where(low, inv[0], inv[1]) * gq[1:2], cos, sa, sb) * scale).T
        qt_ref[2 * b, MLA_NOPE:MLA_QK, :] = yrt[:MLA_ROPE].astype(BF16)
        qt_ref[2 * b + 1, MLA_NOPE:MLA_QK, :] = yrt[MLA_ROPE:].astype(BF16)
    low = lax.broadcasted_iota(jnp.int32, kr.shape, 1) < 64
    gk_rope = jnp.where(low, gk[1:2], 0.0)
    for h in range(HEADS):
        kn = kv[:, h * MLA_NOPE:(h + 1) * MLA_NOPE]
        inv = lax.rsqrt((jnp.sum(kn * kn, axis=-1, keepdims=True) + ss_kr) * (1.0 / MLA_QK) + EPS)
        k_ref[h, :, 0:MLA_NOPE] = (kn * inv * gk[0:1]).astype(BF16)
        yr = _rope128(kr * inv * gk_rope, cos, sa, sb)
        k_ref[h, :, MLA_NOPE:MLA_QK] = yr[:, :MLA_ROPE].astype(BF16)
        vt_ref[h] = kv[:, HEADS * MLA_NOPE + h * MLA_V:HEADS * MLA_NOPE + (h + 1) * MLA_V].T.astype(BF16)


def _even_front_kernel(x_ref, mod_ref, nw_ref, wm_ref, wa_ref, cos_ref, sa_ref, sb_ref, qn_ref, kvn_ref,
                       wuq_ref, wukv_ref, gq_ref, gk_ref, pm_ref, qt_ref, k_ref, vt_ref, *, m_ctx, rows):
    h = _mixer_input(x_ref, mod_ref, nw_ref, m_ctx, rows)
    qu, kv, kr = _mla_up(_dot(h, wa_ref[...]), qn_ref, kvn_ref, wuq_ref, wukv_ref)
    pm_ref[...] = _dot(h, wm_ref[...])
    _mla_heads(qu, kv, kr, cos_ref[...], sa_ref[...], sb_ref[...], gq_ref, gk_ref, qt_ref, k_ref, vt_ref)


def _even_front(xt, mod, nw, w_main, w_mla, tables, qn, kvn, wuq, wukv, gq, gk, m_ctx):
    t, d = xt.shape
    rows = _row_block(t, ROW_BLOCK)
    rowb = lambda w: pl.BlockSpec((rows, w), lambda i: (i, 0))
    consts = (w_main, w_mla), (qn, kvn, wuq, wukv, gq, gk)
    return pl.pallas_call(
        functools.partial(_even_front_kernel, m_ctx=m_ctx, rows=rows),
        grid=(t // rows,),
        in_specs=[rowb(d), _resident(mod), _resident(nw)] + [_resident(a) for a in consts[0]]
                 + [rowb(LANES)] * 3 + [_resident(a) for a in consts[1]],
        out_specs=[rowb(w_main.shape[1]),
                   pl.BlockSpec((HEADS, MLA_QK, rows), lambda i: (0, 0, i)),
                   pl.BlockSpec((HEADS, rows, MLA_QK), lambda i: (0, i, 0)),
                   pl.BlockSpec((HEADS, ATT_V, rows), lambda i: (0, 0, i))],
        out_shape=[jax.ShapeDtypeStruct((t, w_main.shape[1]), F32),
                   jax.ShapeDtypeStruct((HEADS, MLA_QK, t), BF16),
                   jax.ShapeDtypeStruct((HEADS, t, MLA_QK), BF16),
                   jax.ShapeDtypeStruct((HEADS, ATT_V, t), BF16)],
        compiler_params=_cparams(("parallel",)),
        name="even_front",
    )(xt, mod, nw, w_main, w_mla, *tables, qn, kvn, wuq, wukv, gq, gk)


def _pv_and_sums(vt, p):
    sums = p.reshape(p.shape[0] // SUBLANES, SUBLANES, p.shape[1]).sum(axis=0)
    return jnp.concatenate([_dot(vt, p.astype(BF16)), sums], axis=0)


def _attend(items, first):
    scores = [_dot(k_ref[rows, :], qt) for qt, k_ref, _, _, _, rows in items]
    for (_, _, vt_ref, acc_sc, m_sc, rows), s in zip(items, scores):
        vt = vt_ref[:, rows]
        if m_sc is None:
            pv = _pv_and_sums(vt, jnp.exp(s))
            acc_sc[...] = pv if first else acc_sc[...] + pv
        else:
            m_new = jnp.max(s, axis=0, keepdims=True)
            if not first:
                m_prev = m_sc[...]
                m_new = jnp.maximum(m_prev, m_new)
            pv = _pv_and_sums(vt, jnp.exp(s - m_new))
            acc_sc[...] = pv if first else jnp.exp(m_prev - m_new) * acc_sc[...] + pv
            m_sc[...] = m_new


def _attend_all(qts, k_refs, vt_refs, accs, ms, m_ctx, bq, bk, n_blocks, finish):
    pairs = list(zip(qts, k_refs, vt_refs, accs, ms))
    ctx_items = [p + (slice(0, m_ctx),) for p in pairs]
    split = max(1, 2 // len(pairs))
    sub = bk // split
    is_latent = pl.program_id(1) >= m_ctx // bq

    @pl.when(jnp.logical_not(is_latent))
    def _():
        _attend(ctx_items, True)
        finish()

    @pl.when(is_latent)
    def _():
        _attend(ctx_items, True)
        for j in range(n_blocks):
            _attend([p + (slice(m_ctx + j * bk + u * sub, m_ctx + j * bk + (u + 1) * sub),)
                     for p in pairs for u in range(split)], False)
        finish()


def _normalised(acc_sc):
    acc = acc_sc[...]
    return (acc[:ATT_V] / jnp.sum(acc[ATT_V:], axis=0, keepdims=True)).T


def _mla_attn_kernel(qt_ref, k_ref, vt_ref, o_ref, a0, a1, *m_sc, m_ctx, bq, bk, n_blocks):
    ms = list(m_sc) if m_sc else [None, None]

    def finish():
        o_ref[:, 0:MLA_V] = _normalised(a0)
        o_ref[:, MLA_V:] = _normalised(a1)

    _attend_all([qt_ref[0], qt_ref[1]], [k_ref.at[0], k_ref.at[1]], [vt_ref.at[0], vt_ref.at[1]], [a0, a1], ms,
                m_ctx, bq, bk, n_blocks, finish)


def _attn_blocks(t, m_ctx):
    bq = ATT_BQ
    assert m_ctx % bq == 0 and t % bq == 0
    n_lat = t - m_ctx
    bk = _row_block(n_lat, ATT_BK)
    return bq, bk, n_lat // bk


def _attn_scratch(bq, n_pairs, shifted):
    return [pltpu.VMEM((ACC_ROWS, bq), F32)] * n_pairs + ([pltpu.VMEM((1, bq), F32)] * n_pairs if shifted else [])


def _by_score_bound(bound, attend, *operands):
    return lax.cond(bound <= SCORE_LIMIT,
                    functools.partial(attend, shifted=False),
                    functools.partial(attend, shifted=True), *operands)


def _mla_attn(qt, k, vt, m_ctx, shifted):
    t = k.shape[1]
    bq, bk, n_blocks = _attn_blocks(t, m_ctx)
    return pl.pallas_call(
        functools.partial(_mla_attn_kernel, m_ctx=m_ctx, bq=bq, bk=bk, n_blocks=n_blocks),
        grid=(HEADS // 2, t // bq),
        in_specs=[pl.BlockSpec((2, MLA_QK, bq), lambda h, i: (h, 0, i)),
                  pl.BlockSpec((2, t, MLA_QK), lambda h, i: (h, 0, 0), pipeline_mode=pl.Buffered(1)),
                  pl.BlockSpec((2, ATT_V, t), lambda h, i: (h, 0, 0), pipeline_mode=pl.Buffered(1))],
        out_specs=pl.BlockSpec((bq, 2 * MLA_V), lambda h, i: (i, h)),
        out_shape=jax.ShapeDtypeStruct((t, HEADS * MLA_V), F32),
        scratch_shapes=_attn_scratch(bq, 2, shifted),
        compiler_params=_cparams(("parallel", "arbitrary")),
        name="mla_attn_shifted" if shifted else "mla_attn",
    )(qt, k, vt)


def _diff_attn_kernel(q1_ref, q2_ref, k1_ref, k2_ref, v_ref, lam_ref, sub_ref, o_ref, a1, a2, *m_sc,
                      m_ctx, bq, bk, n_blocks, out_scale):
    ms = list(m_sc) if m_sc else [None, None]

    def finish():
        d = _normalised(a1) - lam_ref[...] * _normalised(a2)
        y = d * lax.rsqrt(jnp.mean(d * d, axis=-1, keepdims=True) + EPS)
        o_ref[...] = y * sub_ref[...] * out_scale

    _attend_all([q1_ref[0], q2_ref[0]], [k1_ref.at[0], k2_ref.at[0]], [v_ref.at[0]] * 2, [a1, a2], ms,
                m_ctx, bq, bk, n_blocks, finish)


def _diff_attn(dqt, dk, dvt, lam, subln, m_ctx, out_scale, shifted):
    t = dk.shape[1]
    bq, bk, n_blocks = _attn_blocks(t, m_ctx)
    qspec = lambda c: pl.BlockSpec((1, DA_DIM, bq), lambda h, i: (2 * h + c, 0, i))
    kspec = lambda c: pl.BlockSpec((1, t, DA_DIM), lambda h, i: (2 * h + c, 0, 0))
    vec = pl.BlockSpec((1, DA_V), lambda h, i: (0, 0))
    return pl.pallas_call(
        functools.partial(_diff_attn_kernel, m_ctx=m_ctx, bq=bq, bk=bk, n_blocks=n_blocks,
                          out_scale=out_scale),
        grid=(HEADS, t // bq),
        in_specs=[qspec(0), qspec(1), kspec(0), kspec(1),
                  pl.BlockSpec((1, ATT_V, t), lambda h, i: (h, 0, 0)), vec, vec],
        out_specs=pl.BlockSpec((bq, DA_V), lambda h, i: (i, h)),
        out_shape=jax.ShapeDtypeStruct((t, HEADS * DA_V), F32),
        scratch_shapes=_attn_scratch(bq, 2, shifted),
        compiler_params=_cparams(("parallel", "arbitrary")),
        name="diff_attn_shifted" if shifted else "diff_attn",
    )(dqt, dqt, dk, dk, dvt, lam, subln)


ODD_ATT_W = 3 * HEADS * 2 * DA_DIM + 2 * HEADS * RT_K


def _odd_prep(p, cos, sa, sb, gq_ref, gk_ref, dqt_ref, dk_ref, dvt_ref, rq_ref, rk_ref):
    da_w = HEADS * 2 * DA_DIM

    def sub_rms(x, g):
        low, ss_lo, ss_hi = _half_sums(x * x)
        inv = jnp.where(low, lax.rsqrt(ss_lo * (1.0 / DA_DIM) + EPS), lax.rsqrt(ss_hi * (1.0 / DA_DIM) + EPS))
        return x * inv * g

    for h in range(HEADS):
        col = h * LANES
        yqt = (_rope128(sub_rms(p[:, col:col + LANES], gq_ref[...]), cos, sa, sb) * (DA_DIM ** -0.5)).T
        dqt_ref[2 * h] = yqt[:DA_DIM].astype(BF16)
        dqt_ref[2 * h + 1] = yqt[DA_DIM:].astype(BF16)
        yk = _rope128(sub_rms(p[:, da_w + col:da_w + col + LANES], gk_ref[...]), cos, sa, sb)
        dk_ref[2 * h] = yk[:, :DA_DIM].astype(BF16)
        dk_ref[2 * h + 1] = yk[:, DA_DIM:].astype(BF16)
        dvt_ref[h] = p[:, 2 * da_w + col:2 * da_w + col + LANES].T.astype(BF16)
    r0 = 2 * da_w + HEADS * DA_V
    for b in range(HEADS // 2):
        col = r0 + b * LANES
        yq = _rope128(p[:, col:col + LANES], cos, sa, sb)
        rq_ref[2 * b] = yq[:, :RT_K]
        rq_ref[2 * b + 1] = yq[:, RT_K:]
        col = r0 + HEADS * RT_K + b * LANES
        yk = _rope128(p[:, col:col + LANES] * (RT_K ** -0.5), cos, sa, sb)
        rk_ref[2 * b] = yk[:, :RT_K]
        rk_ref[2 * b + 1] = yk[:, RT_K:]


def _odd_front_kernel(x_ref, mod_ref, nw_ref, wa_ref, wr_ref, cos_ref, sa_ref, sb_ref, gq_ref, gk_ref,
                      dqt_ref, dk_ref, dvt_ref, rq_ref, rk_ref, rest_ref, *, m_ctx, rows):
    h = _mixer_input(x_ref, mod_ref, nw_ref, m_ctx, rows)
    p_att = _dot(h, wa_ref[...])
    rest_ref[...] = _dot(h, wr_ref[...])
    _odd_prep(p_att, cos_ref[...], sa_ref[...], sb_ref[...], gq_ref, gk_ref,
              dqt_ref, dk_ref, dvt_ref, rq_ref, rk_ref)


def _odd_front(xt, mod, nw, w_in, tables, gq, gk, m_ctx):
    t, d = xt.shape
    rows = _row_block(t, ROW_BLOCK)
    w_att, w_rest = w_in[:, :ODD_ATT_W], w_in[:, ODD_ATT_W:]
    rowb = lambda w: pl.BlockSpec((rows, w), lambda i: (i, 0))
    hm = lambda n, w: pl.BlockSpec((n, rows, w), lambda i: (0, i, 0))
    return pl.pallas_call(
        functools.partial(_odd_front_kernel, m_ctx=m_ctx, rows=rows),
        grid=(t // rows,),
        in_specs=[rowb(d), _resident(mod), _resident(nw), _resident(w_att), _resident(w_rest)]
                 + [rowb(LANES)] * 3 + [_resident(gq), _resident(gk)],
        out_specs=[pl.BlockSpec((2 * HEADS, DA_DIM, rows), lambda i: (0, 0, i)),
                   hm(2 * HEADS, DA_DIM),
                   pl.BlockSpec((HEADS, ATT_V, rows), lambda i: (0, 0, i)), hm(HEADS, RT_K), hm(HEADS, RT_K),
                   rowb(w_rest.shape[1])],
        out_shape=[jax.ShapeDtypeStruct((2 * HEADS, DA_DIM, t), BF16),
                   jax.ShapeDtypeStruct((2 * HEADS, t, DA_DIM), BF16),
                   jax.ShapeDtypeStruct((HEADS, ATT_V, t), BF16),
                   jax.ShapeDtypeStruct((HEADS, t, RT_K), F32),
                   jax.ShapeDtypeStruct((HEADS, t, RT_K), F32),
                   jax.ShapeDtypeStruct((t, w_rest.shape[1]), F32)],
        compiler_params=_cparams(("parallel",)),
        name="odd_front",
    )(xt, mod, nw, w_att, w_rest, *tables, gq, gk)


def _back_kernel(x_ref, att_ref, of_ref, ob_ref, gate_ref, nrm_ref, wa_ref, wr_ref, mod_ref, nw_ref, w1_ref, w2_ref,
                 o_ref, *, m_ctx, rows):
    row0 = pl.program_id(0) * rows
    o = of_ref[...] + ob_ref[...]
    gate = gate_ref[...]
    rec = []
    for h in range(HEADS):
        oh = o[:, h * LANES:(h + 1) * LANES]
        inv = lax.rsqrt(jnp.mean(oh * oh, axis=-1, keepdims=True) + EPS)
        rec.append(oh * inv * nrm_ref[...] * _silu(gate[:, h * LANES:(h + 1) * LANES]))
    rec = jnp.concatenate(rec, axis=-1).astype(BF16)
    y = _dot(att_ref[...].astype(BF16), wa_ref[...]) + _dot(rec, wr_ref[...])
    x1 = x_ref[...] + _row_mod(mod_ref, 2, row0, rows, m_ctx) * y
    h = _modnorm(x1, nw_ref[...], _row_mod(mod_ref, 4, row0, rows, m_ctx),
                 _row_mod(mod_ref, 3, row0, rows, m_ctx)).astype(BF16)
    u = jnp.maximum(_dot(h, w1_ref[...]), 0.0)
    o_ref[...] = x1 + _row_mod(mod_ref, 5, row0, rows, m_ctx) * _dot((u * u).astype(BF16), w2_ref[...])


def _back(xt, att, o_f, o_b, gate_arr, gate_block, nrm, w_att, w_rec, mod, nw, w1, w2, m_ctx):
    t, d = xt.shape
    rows = _row_block(t, ROW_BLOCK)
    w = att.shape[1]
    rowb = lambda c: pl.BlockSpec((rows, c), lambda i: (i, 0))
    return pl.pallas_call(
        functools.partial(_back_kernel, m_ctx=m_ctx, rows=rows),
        grid=(t // rows,),
        in_specs=[rowb(d), rowb(w), rowb(w), rowb(w),
                  pl.BlockSpec((rows, w), lambda i: (i, gate_block))]
                 + [_resident(a) for a in (nrm, w_att, w_rec, mod, nw, w1, w2)],
        out_specs=rowb(d),
        out_shape=jax.ShapeDtypeStruct((t, d), F32),
        compiler_params=_cparams(("parallel",)),
        name="layer_back",
    )(xt, att, o_f, o_b, gate_arr, nrm, w_att, w_rec, mod, nw, w1, w2)


def _head_major(w, parts):
    k = w.shape[0]
    wh = w.reshape(k, HEADS, sum(parts))
    out, off = [], 0
    for width in parts:
        out.append(wh[:, :, off:off + width].reshape(k, HEADS * width))
        off += width
    return jnp.concatenate(out, axis=1)


def kernel(x, c, ctx, c_ctx, ada_w, ada_b, norm_w, w_o, mlp_w1, mlp_w2, a_w_in, hg_lb, hg_norm, mla_q_norm,
           mla_kv_norm, mla_w_uq, mla_w_ukv, mla_qk_q, mla_qk_k, c_w_in, da_lambda, da_qk_q, da_qk_k,
           da_subln, rt_decay, rt_norm):
    assert x.shape[0] == 1 and ctx.shape[0] == 1
    n, d = x.shape[1], x.shape[2]
    m_ctx = ctx.shape[1]
    depth = ada_w.shape[0]
    xt = jnp.concatenate([ctx[0], x[0]], axis=0)

    cvec = jnp.zeros((8, d), F32).at[0].set(c_ctx).at[1].set(c[0])
    mods = _ada_table(cvec, ada_w, ada_b)[:, :2].reshape(depth, 2, 6, d)
    tables = _rope_tables(n, m_ctx)

    lb = jnp.cumsum(jax.nn.softmax(hg_lb.astype(F32), axis=0), axis=0)
    lb = lb - lb[:1]
    log_lb = jnp.log(lb).reshape(-1, 2, 1, HG_W)
    log1m_lb = jnp.log1p(-lb).reshape(-1, 2, 1, HG_W)

    for l in range(depth):
        j = l // 2
        mod = mods[l]
        nw = norm_w[l]
        wo = w_o[l].astype(BF16)
        if l % 2 == 0:
            w_in = a_w_in[j]
            w_main = w_in[:, :5 * HG_W].astype(BF16)
            w_mla = jnp.pad(w_in[:, 5 * HG_W:], ((0, 0), (0, MLA_IN - (w_in.shape[1] - 5 * HG_W)))).astype(BF16)
            gq = mla_qk_q[j]
            gk = mla_qk_k[j]
            p_main, qt, k, vt = _even_front(
                xt, mod, nw[0:1], w_main, w_mla, tables, mla_q_norm[j][None], mla_kv_norm[j][None],
                _head_major(mla_w_uq[j], (MLA_NOPE, MLA_ROPE)).astype(BF16),
                _head_major(mla_w_ukv[j], (MLA_NOPE, MLA_V)).astype(BF16),
                jnp.stack([gq[:MLA_NOPE], jnp.tile(gq[MLA_NOPE:], 2)]),
                jnp.stack([gk[:MLA_NOPE], jnp.tile(gk[MLA_NOPE:], 2)]), m_ctx)
            o_f, o_b = _hgrn2(p_main, log_lb[j], log1m_lb[j], m_ctx)
            bound = 1.02 * MLA_QK ** 0.5 * jnp.max(jnp.abs(gq)) * jnp.max(jnp.abs(gk))
            att = _by_score_bound(bound, functools.partial(_mla_attn, m_ctx=m_ctx), qt, k, vt)
            back = (att, o_f, o_b, p_main, 4, hg_norm[j][None], wo[HG_W:], wo[:HG_W])
        else:
            dq, dkt, dv, rq, rk, p = _odd_front(xt, mod, nw[0:1], c_w_in[j].astype(BF16), tables,
                                                 jnp.tile(da_qk_q[j], 2)[None], jnp.tile(da_qk_k[j], 2)[None], m_ctx)
            lam_init = 0.8 - 0.6 * math.exp(-0.3 * l)
            lf = da_lambda[j].astype(F32)
            lam = jnp.exp(jnp.sum(lf[0] * lf[1])) - jnp.exp(jnp.sum(lf[2] * lf[3])) + lam_init
            bound = 1.02 * DA_DIM ** 0.5 * jnp.max(jnp.abs(da_qk_q[j])) * jnp.max(jnp.abs(da_qk_k[j]))
            att = _by_score_bound(
                bound, functools.partial(_diff_attn, m_ctx=m_ctx, out_scale=1.0 - lam_init),
                dq, dkt, dv, jnp.full((1, DA_V), lam, F32), da_subln[j][None])
            lg = jax.nn.log_sigmoid(rt_decay[j].astype(F32))
            lg = jnp.broadcast_to(lg[:, :, None, None], (2, HEADS, 8, LANES))
            r_f, r_b = _retention(rq, rk, p, 0, lg, m_ctx)
            back = (att, r_f, r_b, p, 1, rt_norm[j][None], wo[:HG_W], wo[HG_W:])
        xt = _back(xt, *back, mod, nw[1:2], mlp_w1[l].astype(BF16), mlp_w2[l].astype(BF16), m_ctx)
    return xt[m_ctx:][None]
```

```python
import functools
import math

import numpy as np
import jax
import jax.numpy as jnp
from jax import lax
from jax.experimental import pallas as pl
from jax.experimental.pallas import tpu as pltpu

F32 = jnp.float32
BF16 = jnp.bfloat16

GRID_W = 64
ROPE_DIM = 64
ROPE_BASE = 10000.0
EPS = 1e-6
HEADS = 4
HG_DIM = 128
HG_W = HEADS * HG_DIM
MLA_NOPE = 128
MLA_ROPE = ROPE_DIM
MLA_V = 128
MLA_QK = MLA_NOPE + MLA_ROPE
MLA_Q_RANK = 384
MLA_KV_RANK = 256
DA_DIM = ROPE_DIM
DA_V = 2 * DA_DIM
RT_K = ROPE_DIM
RT_V = 128
MLA_IN = 768

LANES = 128
VMEM_LIMIT = 56 * 1024 * 1024
ROW_BLOCK = 640
ATT_BQ = 256
ATT_BK = 4096
SCAN_CHUNK = 64
SCAN_SUB = 4
RET_CHUNK = 128
RET_SUB = 2
SUBLANES = 8
ATT_V = 128
ACC_ROWS = ATT_V + SUBLANES
SCORE_LIMIT = 40.0


def _cparams(sem):
    return pltpu.CompilerParams(dimension_semantics=sem, vmem_limit_bytes=VMEM_LIMIT)


def _row_block(t, target):
    best = None
    for r in range(LANES, min(t, target) + 1, LANES):
        if t % r == 0:
            best = r
    assert best is not None, t
    return best


def _dot(a, b):
    return jnp.dot(a, b, preferred_element_type=F32)


def _dot_nt(a, b):
    return lax.dot_general(a, b, (((1,), (1,)), ((), ())), preferred_element_type=F32)


def _dot_tn(a, b):
    return lax.dot_general(a, b, (((0,), (0,)), ((), ())), preferred_element_type=F32)


def _silu(x):
    return x * (1.0 / (1.0 + jnp.exp(-x)))


def _row_mod(mod_ref, k, row0, rows, m_ctx):
    r = row0 + lax.broadcasted_iota(jnp.int32, (rows, 1), 0)
    return jnp.where(r < m_ctx, mod_ref[0, k:k + 1, :], mod_ref[1, k:k + 1, :])


def _modnorm(x, nw, sc, sh):
    y = x * lax.rsqrt(jnp.mean(x * x, axis=-1, keepdims=True) + EPS)
    return y * nw * (1.0 + sc) + sh


def _ada_kernel(c_ref, w_ref, b_ref, o_ref):
    cv = c_ref[...]
    o_ref[0] = jnp.dot(_silu(cv), w_ref[0], precision=lax.Precision.HIGHEST,
                       preferred_element_type=F32) + b_ref[0]


def _ada_table(cvec, ada_w, ada_b):
    depth, d, d6 = ada_w.shape
    tn = d6 // 4
    return pl.pallas_call(
        _ada_kernel,
        grid=(depth, d6 // tn),
        in_specs=[pl.BlockSpec((8, d), lambda l, j: (0, 0)),
                  pl.BlockSpec((1, d, tn), lambda l, j: (l, 0, j)),
                  pl.BlockSpec((1, 1, tn), lambda l, j: (l, 0, j))],
        out_specs=pl.BlockSpec((1, 8, tn), lambda l, j: (l, 0, j)),
        out_shape=jax.ShapeDtypeStruct((depth, 8, d6), F32),
        compiler_params=_cparams(("parallel", "parallel")),
        name="ada_table",
    )(cvec, ada_w, ada_b.reshape(depth, 1, d6))


def _mixer_input(x_ref, mod_ref, nw_ref, m_ctx, rows):
    row0 = pl.program_id(0) * rows
    return _modnorm(x_ref[...], nw_ref[...],
                    _row_mod(mod_ref, 1, row0, rows, m_ctx),
                    _row_mod(mod_ref, 0, row0, rows, m_ctx)).astype(BF16)


def _resident(a):
    return pl.BlockSpec(a.shape, lambda i: (0,) * a.ndim, pipeline_mode=pl.Buffered(1))


def _hier_tables(c, reverse):
    levels = int(math.log2(c))
    assert 1 << levels == c
    cums = np.zeros(((levels + 1) * c, c), np.float32)
    roles = np.zeros((levels, c, LANES), np.float32)
    masks = np.zeros((levels + 1, c, c), np.float32)
    for li in range(levels):
        h = c >> (li + 1)
        for t in range(c):
            base = (t // (2 * h)) * 2 * h
            late = (t - base) >= h
            if not reverse:
                if late:
                    cums[li * c + t, base + h:t + 1] = 1.0
                else:
                    cums[li * c + t, t + 1:base + h] = 1.0
            else:
                if late:
                    cums[li * c + t, base + h:t] = 1.0
                else:
                    cums[li * c + t, t:base + h] = 1.0
            is_query = late != reverse
            roles[li, t, :] = 1.0 if is_query else 0.0
        for t in range(c):
            for s in range(c):
                same = (t // (2 * h)) == (s // (2 * h))
                if same and roles[li, t, 0] == 1.0 and roles[li, s, 0] == 0.0:
                    masks[li, t, s] = 1.0
    for t in range(c):
        if not reverse:
            cums[levels * c + t, :t + 1] = 1.0
        else:
            cums[levels * c + t, t:] = 1.0
    masks[levels] = np.eye(c, dtype=np.float32)
    return cums, roles, masks


def _split3(x):
    hi = x.astype(BF16)
    r1 = x - hi.astype(F32)
    mid = r1.astype(BF16)
    lo = (r1 - mid.astype(F32)).astype(BF16)
    return hi, mid, lo


def _gated_chunks(sides, c):
    levels = sides[0][5].shape[0]
    chains = []
    for q_all, kk_all, v_all, logf_all, cum_ref, role_ref, mask_ref, st_ref, o_ref, reverse in sides:
        x_all = _dot(cum_ref[...], jnp.concatenate(_split3(logf_all), axis=0))
        for h in range(HEADS):
            sl = slice(h * HG_DIM, (h + 1) * HG_DIM)
            chains.append((x_all[:, sl], q_all[:, sl], kk_all[:, sl], v_all[:, sl].astype(BF16),
                           role_ref, mask_ref, st_ref, o_ref, reverse, h, sl))
    acc = [_dot_nt(q.astype(BF16), kk.astype(BF16)) * mask_ref[levels]
           for _, q, kk, _, _, mask_ref, *_ in chains]
    for li in range(levels):
        for n, (x, q, kk, _, role_ref, mask_ref, *_) in enumerate(chains):
            z = (jnp.where(role_ref[li] > 0.5, q, kk) * jnp.exp(x[li * c:(li + 1) * c])).astype(BF16)
            acc[n] = acc[n] + _dot_nt(z, z) * mask_ref[li]
    for a, (x, q, kk, vb, _, _, st_ref, o_ref, reverse, h, sl) in zip(acc, chains):
        run = x[levels * c:(levels + 1) * c]
        tot = run[0:1] if reverse else run[c - 1:c]
        st = st_ref[h]
        o_ref[:, sl] = _dot(a.astype(BF16), vb) + _dot_nt((q * jnp.exp(run)).astype(BF16), st.astype(BF16))
        kd = (kk * jnp.exp(tot - run)).astype(BF16)
        st_ref[h] = st * jnp.exp(tot) + _dot_tn(vb, kd)


def _hgrn2_gate(z, log_lb, log1m_lb):
    soft = jnp.log1p(jnp.exp(-jnp.abs(z)))
    b = log1m_lb + (jnp.minimum(z, 0.0) - soft)
    logf = jnp.maximum(log_lb, b) + jnp.log1p(jnp.exp(-jnp.abs(log_lb - b)))
    return logf, jnp.exp(log1m_lb + (jnp.minimum(-z, 0.0) - soft))


def _hgrn2_kernel(qf_ref, zf_ref, vf_ref, qb_ref, zb_ref, vb_ref, llb_ref, l1m_ref,
                  cumf_ref, rolef_ref, maskf_ref, cumb_ref, roleb_ref, maskb_ref,
                  of_ref, ob_ref, sf_ref, sb_ref, *, c, sub):
    @pl.when(pl.program_id(0) == 0)
    def _():
        sf_ref[...] = jnp.zeros_like(sf_ref)
        sb_ref[...] = jnp.zeros_like(sb_ref)

    sides = []
    for u in range(sub):
        rf, rb = pl.ds(u * c, c), pl.ds((sub - 1 - u) * c, c)
        logf, kk = _hgrn2_gate(zf_ref[rf, :], llb_ref[0], l1m_ref[0])
        sides.append((_silu(qf_ref[rf, :]), kk, vf_ref[rf, :], logf, cumf_ref, rolef_ref, maskf_ref, sf_ref,
                      of_ref.at[rf], False))
        logf, kk = _hgrn2_gate(zb_ref[rb, :], llb_ref[1], l1m_ref[1])
        sides.append((_silu(qb_ref[rb, :]), kk, vb_ref[rb, :], logf, cumb_ref, roleb_ref, maskb_ref, sb_ref,
                      ob_ref.at[rb], True))
    for side in sides:
        _gated_chunks([side], c)


def _bwd_chunk(i, mc, steps):
    return jnp.where(i < mc, mc - 1 - i, steps - 1 - i + mc)


def _hgrn2(p_main, log_lb, log1m_lb, m_ctx):
    t = p_main.shape[0]
    c, sub = SCAN_CHUNK, SCAN_SUB
    rows = c * sub
    steps, mc = t // rows, m_ctx // rows
    assert steps * rows == t and mc * rows == m_ctx
    tabs = [jnp.asarray(a) for rev in (False, True) for a in _hier_tables(c, rev)]
    for k in (0, 3):
        tabs[k] = jnp.tile(tabs[k], (1, 3)).astype(BF16)
    fwd = lambda sec: pl.BlockSpec((rows, HG_W), lambda i: (i, sec))
    bwd = lambda sec: pl.BlockSpec((rows, HG_W), lambda i: (_bwd_chunk(i, mc, steps), sec))
    full = lambda a: pl.BlockSpec(a.shape, lambda i: (0,) * a.ndim)
    return pl.pallas_call(
        functools.partial(_hgrn2_kernel, c=c, sub=sub),
        grid=(steps,),
        in_specs=[fwd(0), fwd(1), fwd(3), bwd(0), bwd(2), bwd(3), full(log_lb), full(log1m_lb)]
                 + [full(a) for a in tabs],
        out_specs=[pl.BlockSpec((rows, HG_W), lambda i: (i, 0)),
                   pl.BlockSpec((rows, HG_W), lambda i: (_bwd_chunk(i, mc, steps), 0))],
        out_shape=[jax.ShapeDtypeStruct((t, HG_W), F32)] * 2,
        scratch_shapes=[pltpu.VMEM((HEADS, HG_DIM, HG_DIM), F32)] * 2,
        compiler_params=_cparams(("arbitrary",)),
        name="hgrn2_scan",
    )(p_main, p_main, p_main, p_main, p_main, p_main, log_lb, log1m_lb, *tabs)


def _ret_scores(q, k, lg, c, reverse):
    t = lax.broadcasted_iota(jnp.int32, (c, c), 0)
    s = lax.broadcasted_iota(jnp.int32, (c, c), 1)
    dlt = (s - t) if reverse else (t - s)
    dec = jnp.where(dlt >= 0, jnp.exp(lg[:, :c] * jnp.maximum(dlt, 0).astype(F32)), 0.0)
    return (_dot_nt(q.astype(BF16), k.astype(BF16)) * dec).astype(BF16)


def _ret_finish(a, q, k, v, lg, st_ref, c, reverse):
    r = lax.broadcasted_iota(jnp.int32, (c, RT_K), 0).astype(F32)
    lk = lg[:, :RT_K]
    qdec = jnp.exp(lk * ((c - r) if reverse else (r + 1.0)))
    kdec = jnp.exp(lk * (r if reverse else (c - 1.0 - r)))
    st = st_ref[...]
    vb = v.astype(BF16)
    o = _dot(a, vb) + _dot_nt((q * qdec).astype(BF16), st.astype(BF16))
    st_ref[...] = st * jnp.exp(lk * float(c)) + _dot_tn(vb, (k * kdec).astype(BF16))
    return o


def _ret_kernel(qf_ref, kf_ref, vf_ref, qb_ref, kb_ref, vb_ref, lg_ref,
                of_ref, ob_ref, sf_ref, sb_ref, *, c, sub):
    @pl.when(pl.program_id(0) == 0)
    def _():
        sf_ref[...] = jnp.zeros_like(sf_ref)
        sb_ref[...] = jnp.zeros_like(sb_ref)

    sides = ((qf_ref, kf_ref, vf_ref, of_ref, sf_ref), (qb_ref, kb_ref, vb_ref, ob_ref, sb_ref))
    chains = [(sides[d], d, h, pl.ds((sub - 1 - u if d else u) * c, c))
              for u in range(sub) for h in range(HEADS) for d in range(2)]
    scores = [_ret_scores(q_ref[h, rows], k_ref[h, rows], lg_ref[d, h, 0:1, :], c, d == 1)
              for (q_ref, k_ref, _, _, _), d, h, rows in chains]
    for a, ((q_ref, k_ref, v_ref, o_ref, s_ref), d, h, rows) in zip(scores, chains):
        sl = slice(h * RT_V, (h + 1) * RT_V)
        o_ref[rows, sl] = _ret_finish(a, q_ref[h, rows], k_ref[h, rows], v_ref[rows, sl], lg_ref[d, h, 0:1, :],
                                      s_ref.at[h], c, d == 1)


def _retention(rq, rk, p, v_block, lg, m_ctx):
    t = p.shape[0]
    c, sub = RET_CHUNK, RET_SUB
    rows = c * sub
    steps, mc = t // rows, m_ctx // rows
    assert steps * rows == t and mc * rows == m_ctx
    width = HEADS * RT_V
    fq = pl.BlockSpec((HEADS, rows, RT_K), lambda i: (0, i, 0))
    bq = pl.BlockSpec((HEADS, rows, RT_K), lambda i: (0, _bwd_chunk(i, mc, steps), 0))
    fv = pl.BlockSpec((rows, width), lambda i: (i, v_block))
    bv = pl.BlockSpec((rows, width), lambda i: (_bwd_chunk(i, mc, steps), v_block))
    return pl.pallas_call(
        functools.partial(_ret_kernel, c=c, sub=sub),
        grid=(steps,),
        in_specs=[fq, fq, fv, bq, bq, bv, pl.BlockSpec(lg.shape, lambda i: (0, 0, 0, 0))],
        out_specs=[pl.BlockSpec((rows, width), lambda i: (i, 0)),
                   pl.BlockSpec((rows, width), lambda i: (_bwd_chunk(i, mc, steps), 0))],
        out_shape=[jax.ShapeDtypeStruct((t, width), F32)] * 2,
        scratch_shapes=[pltpu.VMEM((HEADS, RT_V, RT_K), F32)] * 2,
        compiler_params=_cparams(("arbitrary",)),
        name="retention_scan",
    )(rq, rk, p, rq, rk, p, lg)


def _rope128(y, cos, sin_a, sin_b):
    return y * cos + pltpu.roll(y, LANES - 16, 1) * sin_a + pltpu.roll(y, 16, 1) * sin_b


def _rope_tables(n, m_ctx):
    rows = n // GRID_W
    row = jnp.repeat(jnp.arange(rows, dtype=F32), GRID_W)
    col = jnp.tile(jnp.arange(GRID_W, dtype=F32), rows)
    quarter = ROPE_DIM // 4
    inv_freq = ROPE_BASE ** (-jnp.arange(quarter, dtype=F32) / quarter)
    ang_r = row[:, None] * inv_freq
    ang_c = col[:, None] * inv_freq
    ang = jnp.concatenate([ang_r, ang_r, ang_c, ang_c], axis=-1)
    cos, sin = jnp.cos(ang), jnp.sin(ang)
    first = (jnp.arange(ROPE_DIM) % 32) < 16
    sin_a = jnp.where(first, -sin, 0.0)
    sin_b = jnp.where(first, 0.0, sin)
    pad = lambda a, v: jnp.tile(jnp.concatenate([jnp.full((m_ctx, ROPE_DIM), v, F32), a], axis=0), (1, 2))
    return pad(cos, 1.0), pad(sin_a, 0.0), pad(sin_b, 0.0)


def _half_sums(sq):
    low = lax.broadcasted_iota(jnp.int32, sq.shape, 1) < 64
    lo = jnp.sum(jnp.where(low, sq, 0.0), axis=-1, keepdims=True)
    return low, lo, jnp.sum(sq, axis=-1, keepdims=True) - lo


def _mla_up(p, qn_ref, kvn_ref, wuq_ref, wukv_ref):
    def rms(x, w):
        return x * lax.rsqrt(jnp.mean(x * x, axis=-1, keepdims=True) + EPS) * w

    qu = _dot(rms(p[:, :MLA_Q_RANK], qn_ref[...]).astype(BF16), wuq_ref[...])
    kv = _dot(rms(p[:, MLA_Q_RANK:MLA_Q_RANK + MLA_KV_RANK], kvn_ref[...]).astype(BF16),
              wukv_ref[...])
    return qu, kv, p[:, MLA_Q_RANK + MLA_KV_RANK:]


def _mla_heads(qu, kv, kr, cos, sa, sb, gq_ref, gk_ref, qt_ref, k_ref, vt_ref):
    scale = MLA_QK ** -0.5
    ss_kr = jnp.sum(kr * kr, axis=-1, keepdims=True)
    gq, gk = gq_ref[...], gk_ref[...]
    for b in range(HEADS // 2):
        qr = qu[:, HEADS * MLA_NOPE + b * LANES:HEADS * MLA_NOPE + (b + 1) * LANES]
        low, ss_lo, ss_hi = _half_sums(qr * qr)
        inv = []
        for j in range(2):
            h = 2 * b + j
            qn = qu[:, h * MLA_NOPE:(h + 1) * MLA_NOPE]
            ss = jnp.sum(qn * qn, axis=-1, keepdims=True) + (ss_lo, ss_hi)[j]
            inv.append(lax.rsqrt(ss * (1.0 / MLA_QK) + EPS))
            qt_ref[h, 0:MLA_NOPE, :] = (qn * inv[j] * gq[0:1] * scale).T.astype(BF16)
        yrt = (_rope128(qr * jnp.where(low, inv[0], inv[1]) * gq[1:2], cos, sa, sb) * scale).T
        qt_ref[2 * b, MLA_NOPE:MLA_QK, :] = yrt[:MLA_ROPE].astype(BF16)
        qt_ref[2 * b + 1, MLA_NOPE:MLA_QK, :] = yrt[MLA_ROPE:].astype(BF16)
    low = lax.broadcasted_iota(jnp.int32, kr.shape, 1) < 64
    gk_rope = jnp.where(low, gk[1:2], 0.0)
    for h in range(HEADS):
        kn = kv[:, h * MLA_NOPE:(h + 1) * MLA_NOPE]
        inv = lax.rsqrt((jnp.sum(kn * kn, axis=-1, keepdims=True) + ss_kr) * (1.0 / MLA_QK) + EPS)
        k_ref[h, :, 0:MLA_NOPE] = (kn * inv * gk[0:1]).astype(BF16)
        yr = _rope128(kr * inv * gk_rope, cos, sa, sb)
        k_ref[h, :, MLA_NOPE:MLA_QK] = yr[:, :MLA_ROPE].astype(BF16)
        vt_ref[h] = kv[:, HEADS * MLA_NOPE + h * MLA_V:HEADS * MLA_NOPE + (h + 1) * MLA_V].T.astype(BF16)


def _even_front_kernel(x_ref, mod_ref, nw_ref, wm_ref, wa_ref, cos_ref, sa_ref, sb_ref, qn_ref, kvn_ref,
                       wuq_ref, wukv_ref, gq_ref, gk_ref, pm_ref, qt_ref, k_ref, vt_ref, *, m_ctx, rows):
    h = _mixer_input(x_ref, mod_ref, nw_ref, m_ctx, rows)
    qu, kv, kr = _mla_up(_dot(h, wa_ref[...]), qn_ref, kvn_ref, wuq_ref, wukv_ref)
    pm_ref[...] = _dot(h, wm_ref[...])
    _mla_heads(qu, kv, kr, cos_ref[...], sa_ref[...], sb_ref[...], gq_ref, gk_ref, qt_ref, k_ref, vt_ref)


def _even_front(xt, mod, nw, w_main, w_mla, tables, qn, kvn, wuq, wukv, gq, gk, m_ctx):
    t, d = xt.shape
    rows = _row_block(t, ROW_BLOCK)
    rowb = lambda w: pl.BlockSpec((rows, w), lambda i: (i, 0))
    consts = (w_main, w_mla), (qn, kvn, wuq, wukv, gq, gk)
    return pl.pallas_call(
        functools.partial(_even_front_kernel, m_ctx=m_ctx, rows=rows),
        grid=(t // rows,),
        in_specs=[rowb(d), _resident(mod), _resident(nw)] + [_resident(a) for a in consts[0]]
                 + [rowb(LANES)] * 3 + [_resident(a) for a in consts[1]],
        out_specs=[rowb(w_main.shape[1]),
                   pl.BlockSpec((HEADS, MLA_QK, rows), lambda i: (0, 0, i)),
                   pl.BlockSpec((HEADS, rows, MLA_QK), lambda i: (0, i, 0)),
                   pl.BlockSpec((HEADS, ATT_V, rows), lambda i: (0, 0, i))],
        out_shape=[jax.ShapeDtypeStruct((t, w_main.shape[1]), F32),
                   jax.ShapeDtypeStruct((HEADS, MLA_QK, t), BF16),
                   jax.ShapeDtypeStruct((HEADS, t, MLA_QK), BF16),
                   jax.ShapeDtypeStruct((HEADS, ATT_V, t), BF16)],
        compiler_params=_cparams(("parallel",)),
        name="even_front",
    )(xt, mod, nw, w_main, w_mla, *tables, qn, kvn, wuq, wukv, gq, gk)


def _pv_and_sums(vt, p):
    sums = p.reshape(p.shape[0] // SUBLANES, SUBLANES, p.shape[1]).sum(axis=0)
    return jnp.concatenate([_dot(vt, p.astype(BF16)), sums], axis=0)


def _attend(items, first):
    scores = [_dot(k_ref[rows, :], qt) for qt, k_ref, _, _, _, rows in items]
    for (_, _, vt_ref, acc_sc, m_sc, rows), s in zip(items, scores):
        vt = vt_ref[:, rows]
        if m_sc is None:
            pv = _pv_and_sums(vt, jnp.exp(s))
            acc_sc[...] = pv if first else acc_sc[...] + pv
        else:
            m_new = jnp.max(s, axis=0, keepdims=True)
            if not first:
                m_prev = m_sc[...]
                m_new = jnp.maximum(m_prev, m_new)
            pv = _pv_and_sums(vt, jnp.exp(s - m_new))
            acc_sc[...] = pv if first else jnp.exp(m_prev - m_new) * acc_sc[...] + pv
            m_sc[...] = m_new


def _attend_all(qts, k_refs, vt_refs, accs, ms, m_ctx, bq, bk, n_blocks, finish):
    pairs = list(zip(qts, k_refs, vt_refs, accs, ms))
    ctx_items = [p + (slice(0, m_ctx),) for p in pairs]
    split = max(1, 2 // len(pairs))
    sub = bk // split
    is_latent = pl.program_id(1) >= m_ctx // bq

    @pl.when(jnp.logical_not(is_latent))
    def _():
        _attend(ctx_items, True)
        finish()

    @pl.when(is_latent)
    def _():
        _attend(ctx_items, True)
        for j in range(n_blocks):
            _attend([p + (slice(m_ctx + j * bk + u * sub, m_ctx + j * bk + (u + 1) * sub),)
                     for p in pairs for u in range(split)], False)
        finish()


def _normalised(acc_sc):
    acc = acc_sc[...]
    return (acc[:ATT_V] / jnp.sum(acc[ATT_V:], axis=0, keepdims=True)).T


def _mla_attn_kernel(qt_ref, k_ref, vt_ref, o_ref, a0, a1, *m_sc, m_ctx, bq, bk, n_blocks):
    ms = list(m_sc) if m_sc else [None, None]

    def finish():
        o_ref[:, 0:MLA_V] = _normalised(a0)
        o_ref[:, MLA_V:] = _normalised(a1)

    _attend_all([qt_ref[0], qt_ref[1]], [k_ref.at[0], k_ref.at[1]], [vt_ref.at[0], vt_ref.at[1]], [a0, a1], ms,
                m_ctx, bq, bk, n_blocks, finish)


def _attn_blocks(t, m_ctx):
    bq = ATT_BQ
    assert m_ctx % bq == 0 and t % bq == 0
    n_lat = t - m_ctx
    bk = _row_block(n_lat, ATT_BK)
    return bq, bk, n_lat // bk


def _attn_scratch(bq, n_pairs, shifted):
    return [pltpu.VMEM((ACC_ROWS, bq), F32)] * n_pairs + ([pltpu.VMEM((1, bq), F32)] * n_pairs if shifted else [])


def _by_score_bound(bound, attend, *operands):
    return lax.cond(bound <= SCORE_LIMIT,
                    functools.partial(attend, shifted=False),
                    functools.partial(attend, shifted=True), *operands)


def _mla_attn(qt, k, vt, m_ctx, shifted):
    t = k.shape[1]
    bq, bk, n_blocks = _attn_blocks(t, m_ctx)
    return pl.pallas_call(
        functools.partial(_mla_attn_kernel, m_ctx=m_ctx, bq=bq, bk=bk, n_blocks=n_blocks),
        grid=(HEADS // 2, t // bq),
        in_specs=[pl.BlockSpec((2, MLA_QK, bq), lambda h, i: (h, 0, i)),
                  pl.BlockSpec((2, t, MLA_QK), lambda h, i: (h, 0, 0), pipeline_mode=pl.Buffered(1)),
                  pl.BlockSpec((2, ATT_V, t), lambda h, i: (h, 0, 0), pipeline_mode=pl.Buffered(1))],
        out_specs=pl.BlockSpec((bq, 2 * MLA_V), lambda h, i: (i, h)),
        out_shape=jax.ShapeDtypeStruct((t, HEADS * MLA_V), F32),
        scratch_shapes=_attn_scratch(bq, 2, shifted),
        compiler_params=_cparams(("parallel", "arbitrary")),
        name="mla_attn_shifted" if shifted else "mla_attn",
    )(qt, k, vt)


def _diff_attn_kernel(q1_ref, q2_ref, k1_ref, k2_ref, v_ref, lam_ref, sub_ref, o_ref, a1, a2, *m_sc,
                      m_ctx, bq, bk, n_blocks, out_scale):
    ms = list(m_sc) if m_sc else [None, None]

    def finish():
        d = _normalised(a1) - lam_ref[...] * _normalised(a2)
        y = d * lax.rsqrt(jnp.mean(d * d, axis=-1, keepdims=True) + EPS)
        o_ref[...] = y * sub_ref[...] * out_scale

    _attend_all([q1_ref[0], q2_ref[0]], [k1_ref.at[0], k2_ref.at[0]], [v_ref.at[0]] * 2, [a1, a2], ms,
                m_ctx, bq, bk, n_blocks, finish)


def _diff_attn(dqt, dk, dvt, lam, subln, m_ctx, out_scale, shifted):
    t = dk.shape[1]
    bq, bk, n_blocks = _attn_blocks(t, m_ctx)
    qspec = lambda c: pl.BlockSpec((1, DA_DIM, bq), lambda h, i: (2 * h + c, 0, i))
    kspec = lambda c: pl.BlockSpec((1, t, DA_DIM), lambda h, i: (2 * h + c, 0, 0))
    vec = pl.BlockSpec((1, DA_V), lambda h, i: (0, 0))
    return pl.pallas_call(
        functools.partial(_diff_attn_kernel, m_ctx=m_ctx, bq=bq, bk=bk, n_blocks=n_blocks,
                          out_scale=out_scale),
        grid=(HEADS, t // bq),
        in_specs=[qspec(0), qspec(1), kspec(0), kspec(1),
                  pl.BlockSpec((1, ATT_V, t), lambda h, i: (h, 0, 0)), vec, vec],
        out_specs=pl.BlockSpec((bq, DA_V), lambda h, i: (i, h)),
        out_shape=jax.ShapeDtypeStruct((t, HEADS * DA_V), F32),
        scratch_shapes=_attn_scratch(bq, 2, shifted),
        compiler_params=_cparams(("parallel", "arbitrary")),
        name="diff_attn_shifted" if shifted else "diff_attn",
    )(dqt, dqt, dk, dk, dvt, lam, subln)


ODD_ATT_W = 3 * HEADS * 2 * DA_DIM + 2 * HEADS * RT_K


def _odd_prep(p, cos, sa, sb, gq_ref, gk_ref, dqt_ref, dk_ref, dvt_ref, rq_ref, rk_ref):
    da_w = HEADS * 2 * DA_DIM

    def sub_rms(x, g):
        low, ss_lo, ss_hi = _half_sums(x * x)
        inv = jnp.where(low, lax.rsqrt(ss_lo * (1.0 / DA_DIM) + EPS), lax.rsqrt(ss_hi * (1.0 / DA_DIM) + EPS))
        return x * inv * g

    for h in range(HEADS):
        col = h * LANES
        yqt = (_rope128(sub_rms(p[:, col:col + LANES], gq_ref[...]), cos, sa, sb) * (DA_DIM ** -0.5)).T
        dqt_ref[2 * h] = yqt[:DA_DIM].astype(BF16)
        dqt_ref[2 * h + 1] = yqt[DA_DIM:].astype(BF16)
        yk = _rope128(sub_rms(p[:, da_w + col:da_w + col + LANES], gk_ref[...]), cos, sa, sb)
        dk_ref[2 * h] = yk[:, :DA_DIM].astype(BF16)
        dk_ref[2 * h + 1] = yk[:, DA_DIM:].astype(BF16)
        dvt_ref[h] = p[:, 2 * da_w + col:2 * da_w + col + LANES].T.astype(BF16)
    r0 = 2 * da_w + HEADS * DA_V
    for b in range(HEADS // 2):
        col = r0 + b * LANES
        yq = _rope128(p[:, col:col + LANES], cos, sa, sb)
        rq_ref[2 * b] = yq[:, :RT_K]
        rq_ref[2 * b + 1] = yq[:, RT_K:]
        col = r0 + HEADS * RT_K + b * LANES
        yk = _rope128(p[:, col:col + LANES] * (RT_K ** -0.5), cos, sa, sb)
        rk_ref[2 * b] = yk[:, :RT_K]
        rk_ref[2 * b + 1] = yk[:, RT_K:]


def _odd_front_kernel(x_ref, mod_ref, nw_ref, wa_ref, wr_ref, cos_ref, sa_ref, sb_ref, gq_ref, gk_ref,
                      dqt_ref, dk_ref, dvt_ref, rq_ref, rk_ref, rest_ref, *, m_ctx, rows):
    h = _mixer_input(x_ref, mod_ref, nw_ref, m_ctx, rows)
    p_att = _dot(h, wa_ref[...])
    rest_ref[...] = _dot(h, wr_ref[...])
    _odd_prep(p_att, cos_ref[...], sa_ref[...], sb_ref[...], gq_ref, gk_ref,
              dqt_ref, dk_ref, dvt_ref, rq_ref, rk_ref)


def _odd_front(xt, mod, nw, w_in, tables, gq, gk, m_ctx):
    t, d = xt.shape
    rows = _row_block(t, ROW_BLOCK)
    w_att, w_rest = w_in[:, :ODD_ATT_W], w_in[:, ODD_ATT_W:]
    rowb = lambda w: pl.BlockSpec((rows, w), lambda i: (i, 0))
    hm = lambda n, w: pl.BlockSpec((n, rows, w), lambda i: (0, i, 0))
    return pl.pallas_call(
        functools.partial(_odd_front_kernel, m_ctx=m_ctx, rows=rows),
        grid=(t // rows,),
        in_specs=[rowb(d), _resident(mod), _resident(nw), _resident(w_att), _resident(w_rest)]
                 + [rowb(LANES)] * 3 + [_resident(gq), _resident(gk)],
        out_specs=[pl.BlockSpec((2 * HEADS, DA_DIM, rows), lambda i: (0, 0, i)),
                   hm(2 * HEADS, DA_DIM),
                   pl.BlockSpec((HEADS, ATT_V, rows), lambda i: (0, 0, i)), hm(HEADS, RT_K), hm(HEADS, RT_K),
                   rowb(w_rest.shape[1])],
        out_shape=[jax.ShapeDtypeStruct((2 * HEADS, DA_DIM, t), BF16),
                   jax.ShapeDtypeStruct((2 * HEADS, t, DA_DIM), BF16),
                   jax.ShapeDtypeStruct((HEADS, ATT_V, t), BF16),
                   jax.ShapeDtypeStruct((HEADS, t, RT_K), F32),
                   jax.ShapeDtypeStruct((HEADS, t, RT_K), F32),
                   jax.ShapeDtypeStruct((t, w_rest.shape[1]), F32)],
        compiler_params=_cparams(("parallel",)),
        name="odd_front",
    )(xt, mod, nw, w_att, w_rest, *tables, gq, gk)


def _back_kernel(x_ref, att_ref, of_ref, ob_ref, gate_ref, nrm_ref, wa_ref, wr_ref, mod_ref, nw_ref, w1_ref, w2_ref,
                 o_ref, *, m_ctx, rows):
    row0 = pl.program_id(0) * rows
    o = of_ref[...] + ob_ref[...]
    gate = gate_ref[...]
    rec = []
    for h in range(HEADS):
        oh = o[:, h * LANES:(h + 1) * LANES]
        inv = lax.rsqrt(jnp.mean(oh * oh, axis=-1, keepdims=True) + EPS)
        rec.append(oh * inv * nrm_ref[...] * _silu(gate[:, h * LANES:(h + 1) * LANES]))
    rec = jnp.concatenate(rec, axis=-1).astype(BF16)
    y = _dot(att_ref[...].astype(BF16), wa_ref[...]) + _dot(rec, wr_ref[...])
    x1 = x_ref[...] + _row_mod(mod_ref, 2, row0, rows, m_ctx) * y
    h = _modnorm(x1, nw_ref[...], _row_mod(mod_ref, 4, row0, rows, m_ctx),
                 _row_mod(mod_ref, 3, row0, rows, m_ctx)).astype(BF16)
    u = jnp.maximum(_dot(h, w1_ref[...]), 0.0)
    o_ref[...] = x1 + _row_mod(mod_ref, 5, row0, rows, m_ctx) * _dot((u * u).astype(BF16), w2_ref[...])


def _back(xt, att, o_f, o_b, gate_arr, gate_block, nrm, w_att, w_rec, mod, nw, w1, w2, m_ctx):
    t, d = xt.shape
    rows = _row_block(t, ROW_BLOCK)
    w = att.shape[1]
    rowb = lambda c: pl.BlockSpec((rows, c), lambda i: (i, 0))
    return pl.pallas_call(
        functools.partial(_back_kernel, m_ctx=m_ctx, rows=rows),
        grid=(t // rows,),
        in_specs=[rowb(d), rowb(w), rowb(w), rowb(w),
                  pl.BlockSpec((rows, w), lambda i: (i, gate_block))]
                 + [_resident(a) for a in (nrm, w_att, w_rec, mod, nw, w1, w2)],
        out_specs=rowb(d),
        out_shape=jax.ShapeDtypeStruct((t, d), F32),
        compiler_params=_cparams(("parallel",)),
        name="layer_back",
    )(xt, att, o_f, o_b, gate_arr, nrm, w_att, w_rec, mod, nw, w1, w2)


def _head_major(w, parts):
    k = w.shape[0]
    wh = w.reshape(k, HEADS, sum(parts))
    out, off = [], 0
    for width in parts:
        out.append(wh[:, :, off:off + width].reshape(k, HEADS * width))
        off += width
    return jnp.concatenate(out, axis=1)


def kernel(x, c, ctx, c_ctx, ada_w, ada_b, norm_w, w_o, mlp_w1, mlp_w2, a_w_in, hg_lb, hg_norm, mla_q_norm,
           mla_kv_norm, mla_w_uq, mla_w_ukv, mla_qk_q, mla_qk_k, c_w_in, da_lambda, da_qk_q, da_qk_k,
           da_subln, rt_decay, rt_norm):
    assert x.shape[0] == 1 and ctx.shape[0] == 1
    n, d = x.shape[1], x.shape[2]
    m_ctx = ctx.shape[1]
    depth = ada_w.shape[0]
    xt = jnp.concatenate([ctx[0], x[0]], axis=0)

    cvec = jnp.zeros((8, d), F32).at[0].set(c_ctx).at[1].set(c[0])
    mods = _ada_table(cvec, ada_w, ada_b)[:, :2].reshape(depth, 2, 6, d)
    tables = _rope_tables(n, m_ctx)

    lb = jnp.cumsum(jax.nn.softmax(hg_lb.astype(F32), axis=0), axis=0)
    lb = lb - lb[:1]
    log_lb = jnp.log(lb).reshape(-1, 2, 1, HG_W)
    log1m_lb = jnp.log1p(-lb).reshape(-1, 2, 1, HG_W)

    for l in range(depth):
        j = l // 2
        mod = mods[l]
        nw = norm_w[l]
        wo = w_o[l].astype(BF16)
        if l % 2 == 0:
            w_in = a_w_in[j]
            w_main = w_in[:, :5 * HG_W].astype(BF16)
            w_mla = jnp.pad(w_in[:, 5 * HG_W:], ((0, 0), (0, MLA_IN - (w_in.shape[1] - 5 * HG_W)))).astype(BF16)
            gq = mla_qk_q[j]
            gk = mla_qk_k[j]
            p_main, qt, k, vt = _even_front(
                xt, mod, nw[0:1], w_main, w_mla, tables, mla_q_norm[j][None], mla_kv_norm[j][None],
                _head_major(mla_w_uq[j], (MLA_NOPE, MLA_ROPE)).astype(BF16),
                _head_major(mla_w_ukv[j], (MLA_NOPE, MLA_V)).astype(BF16),
                jnp.stack([gq[:MLA_NOPE], jnp.tile(gq[MLA_NOPE:], 2)]),
                jnp.stack([gk[:MLA_NOPE], jnp.tile(gk[MLA_NOPE:], 2)]), m_ctx)
            o_f, o_b = _hgrn2(p_main, log_lb[j], log1m_lb[j], m_ctx)
            bound = 1.02 * MLA_QK ** 0.5 * jnp.max(jnp.abs(gq)) * jnp.max(jnp.abs(gk))
            att = _by_score_bound(bound, functools.partial(_mla_attn, m_ctx=m_ctx), qt, k, vt)
            back = (att, o_f, o_b, p_main, 4, hg_norm[j][None], wo[HG_W:], wo[:HG_W])
        else:
            dq, dkt, dv, rq, rk, p = _odd_front(xt, mod, nw[0:1], c_w_in[j].astype(BF16), tables,
                                                 jnp.tile(da_qk_q[j], 2)[None], jnp.tile(da_qk_k[j], 2)[None], m_ctx)
            lam_init = 0.8 - 0.6 * math.exp(-0.3 * l)
            lf = da_lambda[j].astype(F32)
            lam = jnp.exp(jnp.sum(lf[0] * lf[1])) - jnp.exp(jnp.sum(lf[2] * lf[3])) + lam_init
            bound = 1.02 * DA_DIM ** 0.5 * jnp.max(jnp.abs(da_qk_q[j])) * jnp.max(jnp.abs(da_qk_k[j]))
            att = _by_score_bound(
                bound, functools.partial(_diff_attn, m_ctx=m_ctx, out_scale=1.0 - lam_init),
                dq, dkt, dv, jnp.full((1, DA_V), lam, F32), da_subln[j][None])
            lg = jax.nn.log_sigmoid(rt_decay[j].astype(F32))
            lg = jnp.broadcast_to(lg[:, :, None, None], (2, HEADS, 8, LANES))
            r_f, r_b = _retention(rq, rk, p, 0, lg, m_ctx)
            back = (att, r_f, r_b, p, 1, rt_norm[j][None], wo[:HG_W], wo[HG_W:])
        xt = _back(xt, *back, mod, nw[1:2], mlp_w1[l].astype(BF16), mlp_w2[l].astype(BF16), m_ctx)
    return xt[m_ctx:][None]
```

```python
import functools
import math

import numpy as np
import jax
import jax.numpy as jnp
from jax import lax
from jax.experimental import pallas as pl
from jax.experimental.pallas import tpu as pltpu

F32 = jnp.float32
BF16 = jnp.bfloat16

GRID_W = 64
ROPE_DIM = 64
ROPE_BASE = 10000.0
EPS = 1e-6
HEADS = 4
HG_DIM = 128
HG_W = HEADS * HG_DIM
MLA_NOPE = 128
MLA_ROPE = ROPE_DIM
MLA_V = 128
MLA_QK = MLA_NOPE + MLA_ROPE
MLA_Q_RANK = 384
MLA_KV_RANK = 256
DA_DIM = ROPE_DIM
DA_V = 2 * DA_DIM
RT_K = ROPE_DIM
RT_V = 128
MLA_IN = 768

LANES = 128
VMEM_LIMIT = 56 * 1024 * 1024
ROW_BLOCK = 640
ATT_BQ = 256
ATT_BK = 4096
SCAN_CHUNK = 64
SCAN_SUB = 4
RET_CHUNK = 128
RET_SUB = 2
SUBLANES = 8
ATT_V = 128
ACC_ROWS = ATT_V + SUBLANES
SCORE_LIMIT = 40.0


def _cparams(sem):
    return pltpu.CompilerParams(dimension_semantics=sem, vmem_limit_bytes=VMEM_LIMIT)


def _row_block(t, target):
    best = None
    for r in range(LANES, min(t, target) + 1, LANES):
        if t % r == 0:
            best = r
    assert best is not None, t
    return best


def _dot(a, b):
    return jnp.dot(a, b, preferred_element_type=F32)


def _dot_nt(a, b):
    return lax.dot_general(a, b, (((1,), (1,)), ((), ())), preferred_element_type=F32)


def _dot_tn(a, b):
    return lax.dot_general(a, b, (((0,), (0,)), ((), ())), preferred_element_type=F32)


def _silu(x):
    return x * (1.0 / (1.0 + jnp.exp(-x)))


def _row_mod(mod_ref, k, row0, rows, m_ctx):
    r = row0 + lax.broadcasted_iota(jnp.int32, (rows, 1), 0)
    return jnp.where(r < m_ctx, mod_ref[0, k:k + 1, :], mod_ref[1, k:k + 1, :])


def _modnorm(x, nw, sc, sh):
    y = x * lax.rsqrt(jnp.mean(x * x, axis=-1, keepdims=True) + EPS)
    return y * nw * (1.0 + sc) + sh


def _ada_kernel(c_ref, w_ref, b_ref, o_ref):
    cv = c_ref[...]
    o_ref[0] = jnp.dot(_silu(cv), w_ref[0], precision=lax.Precision.HIGHEST,
                       preferred_element_type=F32) + b_ref[0]


def _ada_table(cvec, ada_w, ada_b):
    depth, d, d6 = ada_w.shape
    tn = d6 // 4
    return pl.pallas_call(
        _ada_kernel,
        grid=(depth, d6 // tn),
        in_specs=[pl.BlockSpec((8, d), lambda l, j: (0, 0)),
                  pl.BlockSpec((1, d, tn), lambda l, j: (l, 0, j)),
                  pl.BlockSpec((1, 1, tn), lambda l, j: (l, 0, j))],
        out_specs=pl.BlockSpec((1, 8, tn), lambda l, j: (l, 0, j)),
        out_shape=jax.ShapeDtypeStruct((depth, 8, d6), F32),
        compiler_params=_cparams(("parallel", "parallel")),
        name="ada_table",
    )(cvec, ada_w, ada_b.reshape(depth, 1, d6))


def _mixer_input(x_ref, mod_ref, nw_ref, m_ctx, rows):
    row0 = pl.program_id(0) * rows
    return _modnorm(x_ref[...], nw_ref[...],
                    _row_mod(mod_ref, 1, row0, rows, m_ctx),
                    _row_mod(mod_ref, 0, row0, rows, m_ctx)).astype(BF16)


def _resident(a):
    return pl.BlockSpec(a.shape, lambda i: (0,) * a.ndim, pipeline_mode=pl.Buffered(1))


def _hier_tables(c, reverse):
    levels = int(math.log2(c))
    assert 1 << levels == c
    cums = np.zeros(((levels + 1) * c, c), np.float32)
    roles = np.zeros((levels, c, LANES), np.float32)
    masks = np.zeros((levels + 1, c, c), np.float32)
    for li in range(levels):
        h = c >> (li + 1)
        for t in range(c):
            base = (t // (2 * h)) * 2 * h
            late = (t - base) >= h
            if not reverse:
                if late:
                    cums[li * c + t, base + h:t + 1] = 1.0
                else:
                    cums[li * c + t, t + 1:base + h] = 1.0
            else:
                if late:
                    cums[li * c + t, base + h:t] = 1.0
                else:
                    cums[li * c + t, t:base + h] = 1.0
            is_query = late != reverse
            roles[li, t, :] = 1.0 if is_query else 0.0
        for t in range(c):
            for s in range(c):
                same = (t // (2 * h)) == (s // (2 * h))
                if same and roles[li, t, 0] == 1.0 and roles[li, s, 0] == 0.0:
                    masks[li, t, s] = 1.0
    for t in range(c):
        if not reverse:
            cums[levels * c + t, :t + 1] = 1.0
        else:
            cums[levels * c + t, t:] = 1.0
    masks[levels] = np.eye(c, dtype=np.float32)
    return cums, roles, masks


def _split3(x):
    hi = x.astype(BF16)
    r1 = x - hi.astype(F32)
    mid = r1.astype(BF16)
    lo = (r1 - mid.astype(F32)).astype(BF16)
    return hi, mid, lo


def _gated_chunks(sides, c):
    levels = sides[0][5].shape[0]
    chains = []
    for q_all, kk_all, v_all, logf_all, cum_ref, role_ref, mask_ref, st_ref, o_ref, reverse in sides:
        x_all = _dot(cum_ref[...], jnp.concatenate(_split3(logf_all), axis=0))
        for h in range(HEADS):
            sl = slice(h * HG_DIM, (h + 1) * HG_DIM)
            chains.append((x_all[:, sl], q_all[:, sl], kk_all[:, sl], v_all[:, sl].astype(BF16),
                           role_ref, mask_ref, st_ref, o_ref, reverse, h, sl))
    acc = [_dot_nt(q.astype(BF16), kk.astype(BF16)) * mask_ref[levels]
           for _, q, kk, _, _, mask_ref, *_ in chains]
    for li in range(levels):
        for n, (x, q, kk, _, role_ref, mask_ref, *_) in enumerate(chains):
            z = (jnp.where(role_ref[li] > 0.5, q, kk) * jnp.exp(x[li * c:(li + 1) * c])).astype(BF16)
            acc[n] = acc[n] + _dot_nt(z, z) * mask_ref[li]
    for a, (x, q, kk, vb, _, _, st_ref, o_ref, reverse, h, sl) in zip(acc, chains):
        run = x[levels * c:(levels + 1) * c]
        tot = run[0:1] if reverse else run[c - 1:c]
        st = st_ref[h]
        o_ref[:, sl] = _dot(a.astype(BF16), vb) + _dot_nt((q * jnp.exp(run)).astype(BF16), st.astype(BF16))
        kd = (kk * jnp.exp(tot - run)).astype(BF16)
        st_ref[h] = st * jnp.exp(tot) + _dot_tn(vb, kd)


def _hgrn2_gate(z, log_lb, log1m_lb):
    soft = jnp.log(1.0 + jnp.exp(-jnp.abs(z)))
    b = log1m_lb + (jnp.minimum(z, 0.0) - soft)
    logf = jnp.maximum(log_lb, b) + jnp.log(1.0 + jnp.exp(-jnp.abs(log_lb - b)))
    return logf, jnp.exp(log1m_lb + (jnp.minimum(-z, 0.0) - soft))


def _hgrn2_kernel(qf_ref, zf_ref, vf_ref, qb_ref, zb_ref, vb_ref, llb_ref, l1m_ref,
                  cumf_ref, rolef_ref, maskf_ref, cumb_ref, roleb_ref, maskb_ref,
                  of_ref, ob_ref, sf_ref, sb_ref, *, c, sub):
    @pl.when(pl.program_id(0) == 0)
    def _():
        sf_ref[...] = jnp.zeros_like(sf_ref)
        sb_ref[...] = jnp.zeros_like(sb_ref)

    sides = []
    for u in range(sub):
        rf, rb = pl.ds(u * c, c), pl.ds((sub - 1 - u) * c, c)
        logf, kk = _hgrn2_gate(zf_ref[rf, :], llb_ref[0], l1m_ref[0])
        sides.append((_silu(qf_ref[rf, :]), kk, vf_ref[rf, :], logf, cumf_ref, rolef_ref, maskf_ref, sf_ref,
                      of_ref.at[rf], False))
        logf, kk = _hgrn2_gate(zb_ref[rb, :], llb_ref[1], l1m_ref[1])
        sides.append((_silu(qb_ref[rb, :]), kk, vb_ref[rb, :], logf, cumb_ref, roleb_ref, maskb_ref, sb_ref,
                      ob_ref.at[rb], True))
    for side in sides:
        _gated_chunks([side], c)


def _bwd_chunk(i, mc, steps):
    return jnp.where(i < mc, mc - 1 - i, steps - 1 - i + mc)


def _hgrn2(p_main, log_lb, log1m_lb, m_ctx):
    t = p_main.shape[0]
    c, sub = SCAN_CHUNK, SCAN_SUB
    rows = c * sub
    steps, mc = t // rows, m_ctx // rows
    assert steps * rows == t and mc * rows == m_ctx
    tabs = [jnp.asarray(a) for rev in (False, True) for a in _hier_tables(c, rev)]
    for k in (0, 3):
        tabs[k] = jnp.tile(tabs[k], (1, 3)).astype(BF16)
    fwd = lambda sec: pl.BlockSpec((rows, HG_W), lambda i: (i, sec))
    bwd = lambda sec: pl.BlockSpec((rows, HG_W), lambda i: (_bwd_chunk(i, mc, steps), sec))
    full = lambda a: pl.BlockSpec(a.shape, lambda i: (0,) * a.ndim)
    return pl.pallas_call(
        functools.partial(_hgrn2_kernel, c=c, sub=sub),
        grid=(steps,),
        in_specs=[fwd(0), fwd(1), fwd(3), bwd(0), bwd(2), bwd(3), full(log_lb), full(log1m_lb)]
                 + [full(a) for a in tabs],
        out_specs=[pl.BlockSpec((rows, HG_W), lambda i: (i, 0)),
                   pl.BlockSpec((rows, HG_W), lambda i: (_bwd_chunk(i, mc, steps), 0))],
        out_shape=[jax.ShapeDtypeStruct((t, HG_W), F32)] * 2,
        scratch_shapes=[pltpu.VMEM((HEADS, HG_DIM, HG_DIM), F32)] * 2,
        compiler_params=_cparams(("arbitrary",)),
        name="hgrn2_scan",
    )(p_main, p_main, p_main, p_main, p_main, p_main, log_lb, log1m_lb, *tabs)


def _ret_scores(q, k, lg, c, reverse):
    t = lax.broadcasted_iota(jnp.int32, (c, c), 0)
    s = lax.broadcasted_iota(jnp.int32, (c, c), 1)
    dlt = (s - t) if reverse else (t - s)
    dec = jnp.where(dlt >= 0, jnp.exp(lg[:, :c] * jnp.maximum(dlt, 0).astype(F32)), 0.0)
    return (_dot_nt(q.astype(BF16), k.astype(BF16)) * dec).astype(BF16)


def _ret_finish(a, q, k, v, lg, st_ref, c, reverse):
    r = lax.broadcasted_iota(jnp.int32, (c, RT_K), 0).astype(F32)
    lk = lg[:, :RT_K]
    qdec = jnp.exp(lk * ((c - r) if reverse else (r + 1.0)))
    kdec = jnp.exp(lk * (r if reverse else (c - 1.0 - r)))
    st = st_ref[...]
    vb = v.astype(BF16)
    o = _dot(a, vb) + _dot_nt((q * qdec).astype(BF16), st.astype(BF16))
    st_ref[...] = st * jnp.exp(lk * float(c)) + _dot_tn(vb, (k * kdec).astype(BF16))
    return o


def _ret_kernel(qf_ref, kf_ref, vf_ref, qb_ref, kb_ref, vb_ref, lg_ref,
                of_ref, ob_ref, sf_ref, sb_ref, *, c, sub):
    @pl.when(pl.program_id(0) == 0)
    def _():
        sf_ref[...] = jnp.zeros_like(sf_ref)
        sb_ref[...] = jnp.zeros_like(sb_ref)

    sides = ((qf_ref, kf_ref, vf_ref, of_ref, sf_ref), (qb_ref, kb_ref, vb_ref, ob_ref, sb_ref))
    chains = [(sides[d], d, h, pl.ds((sub - 1 - u if d else u) * c, c))
              for u in range(sub) for h in range(HEADS) for d in range(2)]
    scores = [_ret_scores(q_ref[h, rows], k_ref[h, rows], lg_ref[d, h, 0:1, :], c, d == 1)
              for (q_ref, k_ref, _, _, _), d, h, rows in chains]
    for a, ((q_ref, k_ref, v_ref, o_ref, s_ref), d, h, rows) in zip(scores, chains):
        sl = slice(h * RT_V, (h + 1) * RT_V)
        o_ref[rows, sl] = _ret_finish(a, q_ref[h, rows], k_ref[h, rows], v_ref[rows, sl], lg_ref[d, h, 0:1, :],
                                      s_ref.at[h], c, d == 1)


def _retention(rq, rk, p, v_block, lg, m_ctx):
    t = p.shape[0]
    c, sub = RET_CHUNK, RET_SUB
    rows = c * sub
    steps, mc = t // rows, m_ctx // rows
    assert steps * rows == t and mc * rows == m_ctx
    width = HEADS * RT_V
    fq = pl.BlockSpec((HEADS, rows, RT_K), lambda i: (0, i, 0))
    bq = pl.BlockSpec((HEADS, rows, RT_K), lambda i: (0, _bwd_chunk(i, mc, steps), 0))
    fv = pl.BlockSpec((rows, width), lambda i: (i, v_block))
    bv = pl.BlockSpec((rows, width), lambda i: (_bwd_chunk(i, mc, steps), v_block))
    return pl.pallas_call(
        functools.partial(_ret_kernel, c=c, sub=sub),
        grid=(steps,),
        in_specs=[fq, fq, fv, bq, bq, bv, pl.BlockSpec(lg.shape, lambda i: (0, 0, 0, 0))],
        out_specs=[pl.BlockSpec((rows, width), lambda i: (i, 0)),
                   pl.BlockSpec((rows, width), lambda i: (_bwd_chunk(i, mc, steps), 0))],
        out_shape=[jax.ShapeDtypeStruct((t, width), F32)] * 2,
        scratch_shapes=[pltpu.VMEM((HEADS, RT_V, RT_K), F32)] * 2,
        compiler_params=_cparams(("arbitrary",)),
        name="retention_scan",
    )(rq, rk, p, rq, rk, p, lg)


def _rope128(y, cos, sin_a, sin_b):
    return y * cos + pltpu.roll(y, LANES - 16, 1) * sin_a + pltpu.roll(y, 16, 1) * sin_b


def _rope_tables(n, m_ctx):
    rows = n // GRID_W
    row = jnp.repeat(jnp.arange(rows, dtype=F32), GRID_W)
    col = jnp.tile(jnp.arange(GRID_W, dtype=F32), rows)
    quarter = ROPE_DIM // 4
    inv_freq = ROPE_BASE ** (-jnp.arange(quarter, dtype=F32) / quarter)
    ang_r = row[:, None] * inv_freq
    ang_c = col[:, None] * inv_freq
    ang = jnp.concatenate([ang_r, ang_r, ang_c, ang_c], axis=-1)
    cos, sin = jnp.cos(ang), jnp.sin(ang)
    first = (jnp.arange(ROPE_DIM) % 32) < 16
    sin_a = jnp.where(first, -sin, 0.0)
    sin_b = jnp.where(first, 0.0, sin)
    pad = lambda a, v: jnp.tile(jnp.concatenate([jnp.full((m_ctx, ROPE_DIM), v, F32), a], axis=0), (1, 2))
    return pad(cos, 1.0), pad(sin_a, 0.0), pad(sin_b, 0.0)


def _half_sums(sq):
    low = lax.broadcasted_iota(jnp.int32, sq.shape, 1) < 64
    lo = jnp.sum(jnp.where(low, sq, 0.0), axis=-1, keepdims=True)
    return low, lo, jnp.sum(sq, axis=-1, keepdims=True) - lo


def _mla_up(p, qn_ref, kvn_ref, wuq_ref, wukv_ref):
    def rms(x, w):
        return x * lax.rsqrt(jnp.mean(x * x, axis=-1, keepdims=True) + EPS) * w

    qu = _dot(rms(p[:, :MLA_Q_RANK], qn_ref[...]).astype(BF16), wuq_ref[...])
    kv = _dot(rms(p[:, MLA_Q_RANK:MLA_Q_RANK + MLA_KV_RANK], kvn_ref[...]).astype(BF16),
              wukv_ref[...])
    return qu, kv, p[:, MLA_Q_RANK + MLA_KV_RANK:]


def _mla_heads(qu, kv, kr, cos, sa, sb, gq_ref, gk_ref, qt_ref, k_ref, vt_ref):
    scale = MLA_QK ** -0.5
    ss_kr = jnp.sum(kr * kr, axis=-1, keepdims=True)
    gq, gk = gq_ref[...], gk_ref[...]
    for b in range(HEADS // 2):
        qr = qu[:, HEADS * MLA_NOPE + b * LANES:HEADS * MLA_NOPE + (b + 1) * LANES]
        low, ss_lo, ss_hi = _half_sums(qr * qr)
        inv = []
        for j in range(2):
            h = 2 * b + j
            qn = qu[:, h * MLA_NOPE:(h + 1) * MLA_NOPE]
            ss = jnp.sum(qn * qn, axis=-1, keepdims=True) + (ss_lo, ss_hi)[j]
            inv.append(lax.rsqrt(ss * (1.0 / MLA_QK) + EPS))
            qt_ref[h, 0:MLA_NOPE, :] = (qn * inv[j] * gq[0:1] * scale).T.astype(BF16)
        yrt = (_rope128(qr * jnp.where(low, inv[0], inv[1]) * gq[1:2], cos, sa, sb) * scale).T
        qt_ref[2 * b, MLA_NOPE:MLA_QK, :] = yrt[:MLA_ROPE].astype(BF16)
        qt_ref[2 * b + 1, MLA_NOPE:MLA_QK, :] = yrt[MLA_ROPE:].astype(BF16)
    low = lax.broadcasted_iota(jnp.int32, kr.shape, 1) < 64
    gk_rope = jnp.where(low, gk[1:2], 0.0)
    for h in range(HEADS):
        kn = kv[:, h * MLA_NOPE:(h + 1) * MLA_NOPE]
        inv = lax.rsqrt((jnp.sum(kn * kn, axis=-1, keepdims=True) + ss_kr) * (1.0 / MLA_QK) + EPS)
        k_ref[h, :, 0:MLA_NOPE] = (kn * inv * gk[0:1]).astype(BF16)
        yr = _rope128(kr * inv * gk_rope, cos, sa, sb)
        k_ref[h, :, MLA_NOPE:MLA_QK] = yr[:, :MLA_ROPE].astype(BF16)
        vt_ref[h] = kv[:, HEADS * MLA_NOPE + h * MLA_V:HEADS * MLA_NOPE + (h + 1) * MLA_V].T.astype(BF16)


def _even_front_kernel(x_ref, mod_ref, nw_ref, wm_ref, wa_ref, cos_ref, sa_ref, sb_ref, qn_ref, kvn_ref,
                       wuq_ref, wukv_ref, gq_ref, gk_ref, pm_ref, qt_ref, k_ref, vt_ref, *, m_ctx, rows):
    h = _mixer_input(x_ref, mod_ref, nw_ref, m_ctx, rows)
    qu, kv, kr = _mla_up(_dot(h, wa_ref[...]), qn_ref, kvn_ref, wuq_ref, wukv_ref)
    pm_ref[...] = _dot(h, wm_ref[...])
    _mla_heads(qu, kv, kr, cos_ref[...], sa_ref[...], sb_ref[...], gq_ref, gk_ref, qt_ref, k_ref, vt_ref)


def _even_front(xt, mod, nw, w_main, w_mla, tables, qn, kvn, wuq, wukv, gq, gk, m_ctx):
    t, d = xt.shape
    rows = _row_block(t, ROW_BLOCK)
    rowb = lambda w: pl.BlockSpec((rows, w), lambda i: (i, 0))
    consts = (w_main, w_mla), (qn, kvn, wuq, wukv, gq, gk)
    return pl.pallas_call(
        functools.partial(_even_front_kernel, m_ctx=m_ctx, rows=rows),
        grid=(t // rows,),
        in_specs=[rowb(d), _resident(mod), _resident(nw)] + [_resident(a) for a in consts[0]]
                 + [rowb(LANES)] * 3 + [_resident(a) for a in consts[1]],
        out_specs=[rowb(w_main.shape[1]),
                   pl.BlockSpec((HEADS, MLA_QK, rows), lambda i: (0, 0, i)),
                   pl.BlockSpec((HEADS, rows, MLA_QK), lambda i: (0, i, 0)),
                   pl.BlockSpec((HEADS, ATT_V, rows), lambda i: (0, 0, i))],
        out_shape=[jax.ShapeDtypeStruct((t, w_main.shape[1]), F32),
                   jax.ShapeDtypeStruct((HEADS, MLA_QK, t), BF16),
                   jax.ShapeDtypeStruct((HEADS, t, MLA_QK), BF16),
                   jax.ShapeDtypeStruct((HEADS, ATT_V, t), BF16)],
        compiler_params=_cparams(("parallel",)),
        name="even_front",
    )(xt, mod, nw, w_main, w_mla, *tables, qn, kvn, wuq, wukv, gq, gk)


def _pv_and_sums(vt, p):
    sums = p.reshape(p.shape[0] // SUBLANES, SUBLANES, p.shape[1]).sum(axis=0)
    return jnp.concatenate([_dot(vt, p.astype(BF16)), sums], axis=0)


def _attend(items, first):
    scores = [_dot(k_ref[rows, :], qt) for qt, k_ref, _, _, _, rows in items]
    for (_, _, vt_ref, acc_sc, m_sc, rows), s in zip(items, scores):
        vt = vt_ref[:, rows]
        if m_sc is None:
            pv = _pv_and_sums(vt, jnp.exp(s))
            acc_sc[...] = pv if first else acc_sc[...] + pv
        else:
            m_new = jnp.max(s, axis=0, keepdims=True)
            if not first:
                m_prev = m_sc[...]
                m_new = jnp.maximum(m_prev, m_new)
            pv = _pv_and_sums(vt, jnp.exp(s - m_new))
            acc_sc[...] = pv if first else jnp.exp(m_prev - m_new) * acc_sc[...] + pv
            m_sc[...] = m_new


def _attend_all(qts, k_refs, vt_refs, accs, ms, m_ctx, bq, bk, n_blocks, finish):
    pairs = list(zip(qts, k_refs, vt_refs, accs, ms))
    ctx_items = [p + (slice(0, m_ctx),) for p in pairs]
    split = max(1, 2 // len(pairs))
    sub = bk // split
    is_latent = pl.program_id(1) >= m_ctx // bq

    @pl.when(jnp.logical_not(is_latent))
    def _():
        _attend(ctx_items, True)
        finish()

    @pl.when(is_latent)
    def _():
        _attend(ctx_items, True)
        for j in range(n_blocks):
            _attend([p + (slice(m_ctx + j * bk + u * sub, m_ctx + j * bk + (u + 1) * sub),)
                     for p in pairs for u in range(split)], False)
        finish()


def _normalised(acc_sc):
    acc = acc_sc[...]
    return (acc[:ATT_V] / jnp.sum(acc[ATT_V:], axis=0, keepdims=True)).T


def _mla_attn_kernel(qt_ref, k_ref, vt_ref, o_ref, a0, a1, *m_sc, m_ctx, bq, bk, n_blocks):
    ms = list(m_sc) if m_sc else [None, None]

    def finish():
        o_ref[:, 0:MLA_V] = _normalised(a0)
        o_ref[:, MLA_V:] = _normalised(a1)

    _attend_all([qt_ref[0], qt_ref[1]], [k_ref.at[0], k_ref.at[1]], [vt_ref.at[0], vt_ref.at[1]], [a0, a1], ms,
                m_ctx, bq, bk, n_blocks, finish)


def _attn_blocks(t, m_ctx):
    bq = ATT_BQ
    assert m_ctx % bq == 0 and t % bq == 0
    n_lat = t - m_ctx
    bk = _row_block(n_lat, ATT_BK)
    return bq, bk, n_lat // bk


def _attn_scratch(bq, n_pairs, shifted):
    return [pltpu.VMEM((ACC_ROWS, bq), F32)] * n_pairs + ([pltpu.VMEM((1, bq), F32)] * n_pairs if shifted else [])


def _by_score_bound(bound, attend, *operands):
    return lax.cond(bound <= SCORE_LIMIT,
                    functools.partial(attend, shifted=False),
                    functools.partial(attend, shifted=True), *operands)


def _mla_attn(qt, k, vt, m_ctx, shifted):
    t = k.shape[1]
    bq, bk, n_blocks = _attn_blocks(t, m_ctx)
    return pl.pallas_call(
        functools.partial(_mla_attn_kernel, m_ctx=m_ctx, bq=bq, bk=bk, n_blocks=n_blocks),
        grid=(HEADS // 2, t // bq),
        in_specs=[pl.BlockSpec((2, MLA_QK, bq), lambda h, i: (h, 0, i)),
                  pl.BlockSpec((2, t, MLA_QK), lambda h, i: (h, 0, 0), pipeline_mode=pl.Buffered(1)),
                  pl.BlockSpec((2, ATT_V, t), lambda h, i: (h, 0, 0), pipeline_mode=pl.Buffered(1))],
        out_specs=pl.BlockSpec((bq, 2 * MLA_V), lambda h, i: (i, h)),
        out_shape=jax.ShapeDtypeStruct((t, HEADS * MLA_V), F32),
        scratch_shapes=_attn_scratch(bq, 2, shifted),
        compiler_params=_cparams(("parallel", "arbitrary")),
        name="mla_attn_shifted" if shifted else "mla_attn",
    )(qt, k, vt)


def _diff_attn_kernel(q1_ref, q2_ref, k1_ref, k2_ref, v_ref, lam_ref, sub_ref, o_ref, a1, a2, *m_sc,
                      m_ctx, bq, bk, n_blocks, out_scale):
    ms = list(m_sc) if m_sc else [None, None]

    def finish():
        d = _normalised(a1) - lam_ref[...] * _normalised(a2)
        y = d * lax.rsqrt(jnp.mean(d * d, axis=-1, keepdims=True) + EPS)
        o_ref[...] = y * sub_ref[...] * out_scale

    _attend_all([q1_ref[0], q2_ref[0]], [k1_ref.at[0], k2_ref.at[0]], [v_ref.at[0]] * 2, [a1, a2], ms,
                m_ctx, bq, bk, n_blocks, finish)


def _diff_attn(dqt, dk, dvt, lam, subln, m_ctx, out_scale, shifted):
    t = dk.shape[1]
    bq, bk, n_blocks = _attn_blocks(t, m_ctx)
    qspec = lambda c: pl.BlockSpec((1, DA_DIM, bq), lambda h, i: (2 * h + c, 0, i))
    kspec = lambda c: pl.BlockSpec((1, t, DA_DIM), lambda h, i: (2 * h + c, 0, 0))
    vec = pl.BlockSpec((1, DA_V), lambda h, i: (0, 0))
    return pl.pallas_call(
        functools.partial(_diff_attn_kernel, m_ctx=m_ctx, bq=bq, bk=bk, n_blocks=n_blocks,
                          out_scale=out_scale),
        grid=(HEADS, t // bq),
        in_specs=[qspec(0), qspec(1), kspec(0), kspec(1),
                  pl.BlockSpec((1, ATT_V, t), lambda h, i: (h, 0, 0)), vec, vec],
        out_specs=pl.BlockSpec((bq, DA_V), lambda h, i: (i, h)),
        out_shape=jax.ShapeDtypeStruct((t, HEADS * DA_V), F32),
        scratch_shapes=_attn_scratch(bq, 2, shifted),
        compiler_params=_cparams(("parallel", "arbitrary")),
        name="diff_attn_shifted" if shifted else "diff_attn",
    )(dqt, dqt, dk, dk, dvt, lam, subln)


ODD_ATT_W = 3 * HEADS * 2 * DA_DIM + 2 * HEADS * RT_K


def _odd_prep(p, cos, sa, sb, gq_ref, gk_ref, dqt_ref, dk_ref, dvt_ref, rq_ref, rk_ref):
    da_w = HEADS * 2 * DA_DIM

    def sub_rms(x, g):
        low, ss_lo, ss_hi = _half_sums(x * x)
        inv = jnp.where(low, lax.rsqrt(ss_lo * (1.0 / DA_DIM) + EPS), lax.rsqrt(ss_hi * (1.0 / DA_DIM) + EPS))
        return x * inv * g

    for h in range(HEADS):
        col = h * LANES
        yqt = (_rope128(sub_rms(p[:, col:col + LANES], gq_ref[...]), cos, sa, sb) * (DA_DIM ** -0.5)).T
        dqt_ref[2 * h] = yqt[:DA_DIM].astype(BF16)
        dqt_ref[2 * h + 1] = yqt[DA_DIM:].astype(BF16)
        yk = _rope128(sub_rms(p[:, da_w + col:da_w + col + LANES], gk_ref[...]), cos, sa, sb)
        dk_ref[2 * h] = yk[:, :DA_DIM].astype(BF16)
        dk_ref[2 * h + 1] = yk[:, DA_DIM:].astype(BF16)
        dvt_ref[h] = p[:, 2 * da_w + col:2 * da_w + col + LANES].T.astype(BF16)
    r0 = 2 * da_w + HEADS * DA_V
    for b in range(HEADS // 2):
        col = r0 + b * LANES
        yq = _rope128(p[:, col:col + LANES], cos, sa, sb)
        rq_ref[2 * b] = yq[:, :RT_K]
        rq_ref[2 * b + 1] = yq[:, RT_K:]
        col = r0 + HEADS * RT_K + b * LANES
        yk = _rope128(p[:, col:col + LANES] * (RT_K ** -0.5), cos, sa, sb)
        rk_ref[2 * b] = yk[:, :RT_K]
        rk_ref[2 * b + 1] = yk[:, RT_K:]


def _odd_front_kernel(x_ref, mod_ref, nw_ref, wa_ref, wr_ref, cos_ref, sa_ref, sb_ref, gq_ref, gk_ref,
                      dqt_ref, dk_ref, dvt_ref, rq_ref, rk_ref, rest_ref, *, m_ctx, rows):
    h = _mixer_input(x_ref, mod_ref, nw_ref, m_ctx, rows)
    p_att = _dot(h, wa_ref[...])
    rest_ref[...] = _dot(h, wr_ref[...])
    _odd_prep(p_att, cos_ref[...], sa_ref[...], sb_ref[...], gq_ref, gk_ref,
              dqt_ref, dk_ref, dvt_ref, rq_ref, rk_ref)


def _odd_front(xt, mod, nw, w_in, tables, gq, gk, m_ctx):
    t, d = xt.shape
    rows = _row_block(t, ROW_BLOCK)
    w_att, w_rest = w_in[:, :ODD_ATT_W], w_in[:, ODD_ATT_W:]
    rowb = lambda w: pl.BlockSpec((rows, w), lambda i: (i, 0))
    hm = lambda n, w: pl.BlockSpec((n, rows, w), lambda i: (0, i, 0))
    return pl.pallas_call(
        functools.partial(_odd_front_kernel, m_ctx=m_ctx, rows=rows),
        grid=(t // rows,),
        in_specs=[rowb(d), _resident(mod), _resident(nw), _resident(w_att), _resident(w_rest)]
                 + [rowb(LANES)] * 3 + [_resident(gq), _resident(gk)],
        out_specs=[pl.BlockSpec((2 * HEADS, DA_DIM, rows), lambda i: (0, 0, i)),
                   hm(2 * HEADS, DA_DIM),
                   pl.BlockSpec((HEADS, ATT_V, rows), lambda i: (0, 0, i)), hm(HEADS, RT_K), hm(HEADS, RT_K),
                   rowb(w_rest.shape[1])],
        out_shape=[jax.ShapeDtypeStruct((2 * HEADS, DA_DIM, t), BF16),
                   jax.ShapeDtypeStruct((2 * HEADS, t, DA_DIM), BF16),
                   jax.ShapeDtypeStruct((HEADS, ATT_V, t), BF16),
                   jax.ShapeDtypeStruct((HEADS, t, RT_K), F32),
                   jax.ShapeDtypeStruct((HEADS, t, RT_K), F32),
                   jax.ShapeDtypeStruct((t, w_rest.shape[1]), F32)],
        compiler_params=_cparams(("parallel",)),
        name="odd_front",
    )(xt, mod, nw, w_att, w_rest, *tables, gq, gk)


def _back_kernel(x_ref, att_ref, of_ref, ob_ref, gate_ref, nrm_ref, wa_ref, wr_ref, mod_ref, nw_ref, w1_ref, w2_ref,
                 o_ref, *, m_ctx, rows):
    row0 = pl.program_id(0) * rows
    o = of_ref[...] + ob_ref[...]
    gate = gate_ref[...]
    rec = []
    for h in range(HEADS):
        oh = o[:, h * LANES:(h + 1) * LANES]
        inv = lax.rsqrt(jnp.mean(oh * oh, axis=-1, keepdims=True) + EPS)
        rec.append(oh * inv * nrm_ref[...] * _silu(gate[:, h * LANES:(h + 1) * LANES]))
    rec = jnp.concatenate(rec, axis=-1).astype(BF16)
    y = _dot(att_ref[...].astype(BF16), wa_ref[...]) + _dot(rec, wr_ref[...])
    x1 = x_ref[...] + _row_mod(mod_ref, 2, row0, rows, m_ctx) * y
    h = _modnorm(x1, nw_ref[...], _row_mod(mod_ref, 4, row0, rows, m_ctx),
                 _row_mod(mod_ref, 3, row0, rows, m_ctx)).astype(BF16)
    u = jnp.maximum(_dot(h, w1_ref[...]), 0.0)
    o_ref[...] = x1 + _row_mod(mod_ref, 5, row0, rows, m_ctx) * _dot((u * u).astype(BF16), w2_ref[...])


def _back(xt, att, o_f, o_b, gate_arr, gate_block, nrm, w_att, w_rec, mod, nw, w1, w2, m_ctx):
    t, d = xt.shape
    rows = _row_block(t, ROW_BLOCK)
    w = att.shape[1]
    rowb = lambda c: pl.BlockSpec((rows, c), lambda i: (i, 0))
    return pl.pallas_call(
        functools.partial(_back_kernel, m_ctx=m_ctx, rows=rows),
        grid=(t // rows,),
        in_specs=[rowb(d), rowb(w), rowb(w), rowb(w),
                  pl.BlockSpec((rows, w), lambda i: (i, gate_block))]
                 + [_resident(a) for a in (nrm, w_att, w_rec, mod, nw, w1, w2)],
        out_specs=rowb(d),
        out_shape=jax.ShapeDtypeStruct((t, d), F32),
        compiler_params=_cparams(("parallel",)),
        name="layer_back",
    )(xt, att, o_f, o_b, gate_arr, nrm, w_att, w_rec, mod, nw, w1, w2)


def _head_major(w, parts):
    k = w.shape[0]
    wh = w.reshape(k, HEADS, sum(parts))
    out, off = [], 0
    for width in parts:
        out.append(wh[:, :, off:off + width].reshape(k, HEADS * width))
        off += width
    return jnp.concatenate(out, axis=1)


def kernel(x, c, ctx, c_ctx, ada_w, ada_b, norm_w, w_o, mlp_w1, mlp_w2, a_w_in, hg_lb, hg_norm, mla_q_norm,
           mla_kv_norm, mla_w_uq, mla_w_ukv, mla_qk_q, mla_qk_k, c_w_in, da_lambda, da_qk_q, da_qk_k,
           da_subln, rt_decay, rt_norm):
    assert x.shape[0] == 1 and ctx.shape[0] == 1
    n, d = x.shape[1], x.shape[2]
    m_ctx = ctx.shape[1]
    depth = ada_w.shape[0]
    xt = jnp.concatenate([ctx[0], x[0]], axis=0)

    cvec = jnp.zeros((8, d), F32).at[0].set(c_ctx).at[1].set(c[0])
    mods = _ada_table(cvec, ada_w, ada_b)[:, :2].reshape(depth, 2, 6, d)
    tables = _rope_tables(n, m_ctx)

    lb = jnp.cumsum(jax.nn.softmax(hg_lb.astype(F32), axis=0), axis=0)
    lb = lb - lb[:1]
    log_lb = jnp.log(lb).reshape(-1, 2, 1, HG_W)
    log1m_lb = jnp.log1p(-lb).reshape(-1, 2, 1, HG_W)

    for l in range(depth):
        j = l // 2
        mod = mods[l]
        nw = norm_w[l]
        wo = w_o[l].astype(BF16)
        if l % 2 == 0:
            w_in = a_w_in[j]
            w_main = w_in[:, :5 * HG_W].astype(BF16)
            w_mla = jnp.pad(w_in[:, 5 * HG_W:], ((0, 0), (0, MLA_IN - (w_in.shape[1] - 5 * HG_W)))).astype(BF16)
            gq = mla_qk_q[j]
            gk = mla_qk_k[j]
            p_main, qt, k, vt = _even_front(
                xt, mod, nw[0:1], w_main, w_mla, tables, mla_q_norm[j][None], mla_kv_norm[j][None],
                _head_major(mla_w_uq[j], (MLA_NOPE, MLA_ROPE)).astype(BF16),
                _head_major(mla_w_ukv[j], (MLA_NOPE, MLA_V)).astype(BF16),
                jnp.stack([gq[:MLA_NOPE], jnp.tile(gq[MLA_NOPE:], 2)]),
                jnp.stack([gk[:MLA_NOPE], jnp.tile(gk[MLA_NOPE:], 2)]), m_ctx)
            o_f, o_b = _hgrn2(p_main, log_lb[j], log1m_lb[j], m_ctx)
            bound = 1.02 * MLA_QK ** 0.5 * jnp.max(jnp.abs(gq)) * jnp.max(jnp.abs(gk))
            att = _by_score_bound(bound, functools.partial(_mla_attn, m_ctx=m_ctx), qt, k, vt)
            back = (att, o_f, o_b, p_main, 4, hg_norm[j][None], wo[HG_W:], wo[:HG_W])
        else:
            dq, dkt, dv, rq, rk, p = _odd_front(xt, mod, nw[0:1], c_w_in[j].astype(BF16), tables,
                                                 jnp.tile(da_qk_q[j], 2)[None], jnp.tile(da_qk_k[j], 2)[None], m_ctx)
            lam_init = 0.8 - 0.6 * math.exp(-0.3 * l)
            lf = da_lambda[j].astype(F32)
            lam = jnp.exp(jnp.sum(lf[0] * lf[1])) - jnp.exp(jnp.sum(lf[2] * lf[3])) + lam_init
            bound = 1.02 * DA_DIM ** 0.5 * jnp.max(jnp.abs(da_qk_q[j])) * jnp.max(jnp.abs(da_qk_k[j]))
            att = _by_score_bound(
                bound, functools.partial(_diff_attn, m_ctx=m_ctx, out_scale=1.0 - lam_init),
                dq, dkt, dv, jnp.full((1, DA_V), lam, F32), da_subln[j][None])
            lg = jax.nn.log_sigmoid(rt_decay[j].astype(F32))
            lg = jnp.broadcast_to(lg[:, :, None, None], (2, HEADS, 8, LANES))
            r_f, r_b = _retention(rq, rk, p, 0, lg, m_ctx)
            back = (att, r_f, r_b, p, 1, rt_norm[j][None], wo[:HG_W], wo[HG_W:])
        xt = _back(xt, *back, mod, nw[1:2], mlp_w1[l].astype(BF16), mlp_w2[l].astype(BF16), m_ctx)
    return xt[m_ctx:][None]
```

```python
import functools
import math

import numpy as np
import jax
import jax.numpy as jnp
from jax import lax
from jax.experimental import pallas as pl
from jax.experimental.pallas import tpu as pltpu

F32 = jnp.float32
BF16 = jnp.bfloat16

GRID_W = 64
ROPE_DIM = 64
ROPE_BASE = 10000.0
EPS = 1e-6
HEADS = 4
HG_DIM = 128
HG_W = HEADS * HG_DIM
MLA_NOPE = 128
MLA_ROPE = ROPE_DIM
MLA_V = 128
MLA_QK = MLA_NOPE + MLA_ROPE
MLA_Q_RANK = 384
MLA_KV_RANK = 256
DA_DIM = ROPE_DIM
DA_V = 2 * DA_DIM
RT_K = ROPE_DIM
RT_V = 128
MLA_IN = 768

LANES = 128
VMEM_LIMIT = 56 * 1024 * 1024
ROW_BLOCK = 640
ATT_BQ = 256
ATT_BK = 4096
SCAN_CHUNK = 64
SCAN_SUB = 4
RET_CHUNK = 128
RET_SUB = 2
SUBLANES = 8
ATT_V = 128
ACC_ROWS = ATT_V + SUBLANES
SCORE_LIMIT = 40.0


def _cparams(sem):
    return pltpu.CompilerParams(dimension_semantics=sem, vmem_limit_bytes=VMEM_LIMIT)


def _row_block(t, target):
    best = None
    for r in range(LANES, min(t, target) + 1, LANES):
        if t % r == 0:
            best = r
    assert best is not None, t
    return best


def _dot(a, b):
    return jnp.dot(a, b, preferred_element_type=F32)


def _dot_nt(a, b):
    return lax.dot_general(a, b, (((1,), (1,)), ((), ())), preferred_element_type=F32)


def _dot_tn(a, b):
    return lax.dot_general(a, b, (((0,), (0,)), ((), ())), preferred_element_type=F32)


def _silu(x):
    return x * (1.0 / (1.0 + jnp.exp(-x)))


def _row_mod(mod_ref, k, row0, rows, m_ctx):
    r = row0 + lax.broadcasted_iota(jnp.int32, (rows, 1), 0)
    return jnp.where(r < m_ctx, mod_ref[0, k:k + 1, :], mod_ref[1, k:k + 1, :])


def _modnorm(x, nw, sc, sh):
    y = x * lax.rsqrt(jnp.mean(x * x, axis=-1, keepdims=True) + EPS)
    return y * nw * (1.0 + sc) + sh


def _ada_kernel(c_ref, w_ref, b_ref, o_ref):
    cv = c_ref[...]
    o_ref[0] = jnp.dot(_silu(cv), w_ref[0], precision=lax.Precision.HIGHEST,
                       preferred_element_type=F32) + b_ref[0]


def _ada_table(cvec, ada_w, ada_b):
    depth, d, d6 = ada_w.shape
    tn = d6 // 4
    return pl.pallas_call(
        _ada_kernel,
        grid=(depth, d6 // tn),
        in_specs=[pl.BlockSpec((8, d), lambda l, j: (0, 0)),
                  pl.BlockSpec((1, d, tn), lambda l, j: (l, 0, j)),
                  pl.BlockSpec((1, 1, tn), lambda l, j: (l, 0, j))],
        out_specs=pl.BlockSpec((1, 8, tn), lambda l, j: (l, 0, j)),
        out_shape=jax.ShapeDtypeStruct((depth, 8, d6), F32),
        compiler_params=_cparams(("parallel", "parallel")),
        name="ada_table",
    )(cvec, ada_w, ada_b.reshape(depth, 1, d6))


def _mixer_input(x_ref, mod_ref, nw_ref, m_ctx, rows):
    row0 = pl.program_id(0) * rows
    return _modnorm(x_ref[...], nw_ref[...],
                    _row_mod(mod_ref, 1, row0, rows, m_ctx),
                    _row_mod(mod_ref, 0, row0, rows, m_ctx)).astype(BF16)


def _resident(a):
    return pl.BlockSpec(a.shape, lambda i: (0,) * a.ndim, pipeline_mode=pl.Buffered(1))


def _hier_tables(c, reverse):
    levels = int(math.log2(c))
    assert 1 << levels == c
    cums = np.zeros(((levels + 1) * c, c), np.float32)
    roles = np.zeros((levels, c, LANES), np.float32)
    masks = np.zeros((levels + 1, c, c), np.float32)
    for li in range(levels):
        h = c >> (li + 1)
        for t in range(c):
            base = (t // (2 * h)) * 2 * h
            late = (t - base) >= h
            if not reverse:
                if late:
                    cums[li * c + t, base + h:t + 1] = 1.0
                else:
                    cums[li * c + t, t + 1:base + h] = 1.0
            else:
                if late:
                    cums[li * c + t, base + h:t] = 1.0
                else:
                    cums[li * c + t, t:base + h] = 1.0
            is_query = late != reverse
            roles[li, t, :] = 1.0 if is_query else 0.0
        for t in range(c):
            for s in range(c):
                same = (t // (2 * h)) == (s // (2 * h))
                if same and roles[li, t, 0] == 1.0 and roles[li, s, 0] == 0.0:
                    masks[li, t, s] = 1.0
    for t in range(c):
        if not reverse:
            cums[levels * c + t, :t + 1] = 1.0
        else:
            cums[levels * c + t, t:] = 1.0
    masks[levels] = np.eye(c, dtype=np.float32)
    return cums, roles, masks


def _split3(x):
    hi = x.astype(BF16)
    r1 = x - hi.astype(F32)
    mid = r1.astype(BF16)
    lo = (r1 - mid.astype(F32)).astype(BF16)
    return hi, mid, lo


def _gated_chunks(sides, c):
    levels = sides[0][5].shape[0]
    chains = []
    for q_all, kk_all, v_all, logf_all, cum_ref, role_ref, mask_ref, st_ref, o_ref, reverse in sides:
        x_all = _dot(cum_ref[...], jnp.concatenate(_split3(logf_all), axis=0))
        for h in range(HEADS):
            sl = slice(h * HG_DIM, (h + 1) * HG_DIM)
            chains.append((x_all[:, sl], q_all[:, sl], kk_all[:, sl], v_all[:, sl].astype(BF16),
                           role_ref, mask_ref, st_ref, o_ref, reverse, h, sl))
    acc = [_dot_nt(q.astype(BF16), kk.astype(BF16)) * mask_ref[levels]
           for _, q, kk, _, _, mask_ref, *_ in chains]
    for li in range(levels):
        for n, (x, q, kk, _, role_ref, mask_ref, *_) in enumerate(chains):
            z = (jnp.where(role_ref[li] > 0.5, q, kk) * jnp.exp(x[li * c:(li + 1) * c])).astype(BF16)
            acc[n] = acc[n] + _dot_nt(z, z) * mask_ref[li]
    for a, (x, q, kk, vb, _, _, st_ref, o_ref, reverse, h, sl) in zip(acc, chains):
        run = x[levels * c:(levels + 1) * c]
        tot = run[0:1] if reverse else run[c - 1:c]
        st = st_ref[h]
        o_ref[:, sl] = _dot(a.astype(BF16), vb) + _dot_nt((q * jnp.exp(run)).astype(BF16), st.astype(BF16))
        kd = (kk * jnp.exp(tot - run)).astype(BF16)
        st_ref[h] = st * jnp.exp(tot) + _dot_tn(vb, kd)


def _hgrn2_gate(z, log_lb, log1m_lb):
    soft = jnp.log(1.0 + jnp.exp(-jnp.abs(z)))
    b = log1m_lb + (jnp.minimum(z, 0.0) - soft)
    logf = jnp.maximum(log_lb, b) + jnp.log(1.0 + jnp.exp(-jnp.abs(log_lb - b)))
    return logf, jnp.exp(log1m_lb + (jnp.minimum(-z, 0.0) - soft))


def _hgrn2_kernel(qf_ref, zf_ref, vf_ref, qb_ref, zb_ref, vb_ref, llb_ref, l1m_ref,
                  cumf_ref, rolef_ref, maskf_ref, cumb_ref, roleb_ref, maskb_ref,
                  of_ref, ob_ref, sf_ref, sb_ref, *, c, sub):
    @pl.when(pl.program_id(0) == 0)
    def _():
        sf_ref[...] = jnp.zeros_like(sf_ref)
        sb_ref[...] = jnp.zeros_like(sb_ref)

    sides = []
    for u in range(sub):
        rf, rb = pl.ds(u * c, c), pl.ds((sub - 1 - u) * c, c)
        logf, kk = _hgrn2_gate(zf_ref[rf, :], llb_ref[0], l1m_ref[0])
        sides.append((_silu(qf_ref[rf, :]), kk, vf_ref[rf, :], logf, cumf_ref, rolef_ref, maskf_ref, sf_ref,
                      of_ref.at[rf], False))
        logf, kk = _hgrn2_gate(zb_ref[rb, :], llb_ref[1], l1m_ref[1])
        sides.append((_silu(qb_ref[rb, :]), kk, vb_ref[rb, :], logf, cumb_ref, roleb_ref, maskb_ref, sb_ref,
                      ob_ref.at[rb], True))
    for side in sides:
        _gated_chunks([side], c)


def _bwd_chunk(i, mc, steps):
    return jnp.where(i < mc, mc - 1 - i, steps - 1 - i + mc)


def _hgrn2(p_main, log_lb, log1m_lb, m_ctx):
    t = p_main.shape[0]
    c, sub = SCAN_CHUNK, SCAN_SUB
    rows = c * sub
    steps, mc = t // rows, m_ctx // rows
    assert steps * rows == t and mc * rows == m_ctx
    tabs = [jnp.asarray(a) for rev in (False, True) for a in _hier_tables(c, rev)]
    for k in (0, 3):
        tabs[k] = jnp.tile(tabs[k], (1, 3)).astype(BF16)
    fwd = lambda sec: pl.BlockSpec((rows, HG_W), lambda i: (i, sec))
    bwd = lambda sec: pl.BlockSpec((rows, HG_W), lambda i: (_bwd_chunk(i, mc, steps), sec))
    full = lambda a: pl.BlockSpec(a.shape, lambda i: (0,) * a.ndim)
    return pl.pallas_call(
        functools.partial(_hgrn2_kernel, c=c, sub=sub),
        grid=(steps,),
        in_specs=[fwd(0), fwd(1), fwd(3), bwd(0), bwd(2), bwd(3), full(log_lb), full(log1m_lb)]
                 + [full(a) for a in tabs],
        out_specs=[pl.BlockSpec((rows, HG_W), lambda i: (i, 0)),
                   pl.BlockSpec((rows, HG_W), lambda i: (_bwd_chunk(i, mc, steps), 0))],
        out_shape=[jax.ShapeDtypeStruct((t, HG_W), F32)] * 2,
        scratch_shapes=[pltpu.VMEM((HEADS, HG_DIM, HG_DIM), F32)] * 2,
        compiler_params=_cparams(("arbitrary",)),
        name="hgrn2_scan",
    )(p_main, p_main, p_main, p_main, p_main, p_main, log_lb, log1m_lb, *tabs)


def _ret_head(x, h):
    low = lax.broadcasted_iota(jnp.int32, x.shape, 1) < RT_K
    return jnp.where(low if h % 2 == 0 else jnp.logical_not(low), x, 0.0)


def _ret_scores(q, k, lg, c, reverse):
    t = lax.broadcasted_iota(jnp.int32, (c, c), 0)
    s = lax.broadcasted_iota(jnp.int32, (c, c), 1)
    dlt = (s - t) if reverse else (t - s)
    dec = jnp.where(dlt >= 0, jnp.exp(lg[:, :c] * jnp.maximum(dlt, 0).astype(F32)), 0.0)
    return (_dot_nt(q.astype(BF16), k.astype(BF16)) * dec).astype(BF16)


def _ret_finish(a, q, k, v, lg, st_ref, c, reverse):
    r = lax.broadcasted_iota(jnp.int32, (c, LANES), 0).astype(F32)
    lk = lg
    qdec = jnp.exp(lk * ((c - r) if reverse else (r + 1.0)))
    kdec = jnp.exp(lk * (r if reverse else (c - 1.0 - r)))
    st = st_ref[...]
    vb = v.astype(BF16)
    o = _dot(a, vb) + _dot_nt((q * qdec).astype(BF16), st.astype(BF16))
    st_ref[...] = st * jnp.exp(lk * float(c)) + _dot_tn(vb, (k * kdec).astype(BF16))
    return o


def _ret_kernel(qf_ref, kf_ref, vf_ref, qb_ref, kb_ref, vb_ref, lg_ref,
                of_ref, ob_ref, sf_ref, sb_ref, *, c, sub):
    @pl.when(pl.program_id(0) == 0)
    def _():
        sf_ref[...] = jnp.zeros_like(sf_ref)
        sb_ref[...] = jnp.zeros_like(sb_ref)

    sides = ((qf_ref, kf_ref, vf_ref, of_ref, sf_ref), (qb_ref, kb_ref, vb_ref, ob_ref, sb_ref))
    chains = [(sides[d], d, h, pl.ds((sub - 1 - u if d else u) * c, c))
              for u in range(sub) for h in range(HEADS) for d in range(2)]
    blk = lambda h: slice(h // 2 * LANES, (h // 2 + 1) * LANES)
    scores = [_ret_scores(_ret_head(q_ref[rows, blk(h)], h), k_ref[rows, blk(h)], lg_ref[d, h, 0:1, :], c, d == 1)
              for (q_ref, k_ref, _, _, _), d, h, rows in chains]
    for a, ((q_ref, k_ref, v_ref, o_ref, s_ref), d, h, rows) in zip(scores, chains):
        sl = slice(h * RT_V, (h + 1) * RT_V)
        o_ref[rows, sl] = _ret_finish(a, _ret_head(q_ref[rows, blk(h)], h), k_ref[rows, blk(h)], v_ref[rows, sl],
                                      lg_ref[d, h, 0:1, :], s_ref.at[h], c, d == 1)


def _retention(rq, rk, p, v_block, lg, m_ctx):
    t = p.shape[0]
    c, sub = RET_CHUNK, RET_SUB
    rows = c * sub
    steps, mc = t // rows, m_ctx // rows
    assert steps * rows == t and mc * rows == m_ctx
    width = HEADS * RT_V
    fq = pl.BlockSpec((rows, HEADS * RT_K), lambda i: (i, 0))
    bq = pl.BlockSpec((rows, HEADS * RT_K), lambda i: (_bwd_chunk(i, mc, steps), 0))
    fv = pl.BlockSpec((rows, width), lambda i: (i, v_block))
    bv = pl.BlockSpec((rows, width), lambda i: (_bwd_chunk(i, mc, steps), v_block))
    return pl.pallas_call(
        functools.partial(_ret_kernel, c=c, sub=sub),
        grid=(steps,),
        in_specs=[fq, fq, fv, bq, bq, bv, pl.BlockSpec(lg.shape, lambda i: (0, 0, 0, 0))],
        out_specs=[pl.BlockSpec((rows, width), lambda i: (i, 0)),
                   pl.BlockSpec((rows, width), lambda i: (_bwd_chunk(i, mc, steps), 0))],
        out_shape=[jax.ShapeDtypeStruct((t, width), F32)] * 2,
        scratch_shapes=[pltpu.VMEM((HEADS, RT_V, LANES), F32)] * 2,
        compiler_params=_cparams(("arbitrary",)),
        name="retention_scan",
    )(rq, rk, p, rq, rk, p, lg)


def _rope128(y, cos, sin_a, sin_b):
    return y * cos + pltpu.roll(y, LANES - 16, 1) * sin_a + pltpu.roll(y, 16, 1) * sin_b


def _rope_tables(n, m_ctx):
    rows = n // GRID_W
    row = jnp.repeat(jnp.arange(rows, dtype=F32), GRID_W)
    col = jnp.tile(jnp.arange(GRID_W, dtype=F32), rows)
    quarter = ROPE_DIM // 4
    inv_freq = ROPE_BASE ** (-jnp.arange(quarter, dtype=F32) / quarter)
    ang_r = row[:, None] * inv_freq
    ang_c = col[:, None] * inv_freq
    ang = jnp.concatenate([ang_r, ang_r, ang_c, ang_c], axis=-1)
    cos, sin = jnp.cos(ang), jnp.sin(ang)
    first = (jnp.arange(ROPE_DIM) % 32) < 16
    sin_a = jnp.where(first, -sin, 0.0)
    sin_b = jnp.where(first, 0.0, sin)
    pad = lambda a, v: jnp.tile(jnp.concatenate([jnp.full((m_ctx, ROPE_DIM), v, F32), a], axis=0), (1, 2))
    return pad(cos, 1.0), pad(sin_a, 0.0), pad(sin_b, 0.0)


def _half_sums(sq):
    low = lax.broadcasted_iota(jnp.int32, sq.shape, 1) < 64
    lo = jnp.sum(jnp.where(low, sq, 0.0), axis=-1, keepdims=True)
    return low, lo, jnp.sum(sq, axis=-1, keepdims=True) - lo


def _mla_up(p, qn_ref, kvn_ref, wuq_ref, wukv_ref):
    def rms(x, w):
        return x * lax.rsqrt(jnp.mean(x * x, axis=-1, keepdims=True) + EPS) * w

    qu = _dot(rms(p[:, :MLA_Q_RANK], qn_ref[...]).astype(BF16), wuq_ref[...])
    kv = _dot(rms(p[:, MLA_Q_RANK:MLA_Q_RANK + MLA_KV_RANK], kvn_ref[...]).astype(BF16),
              wukv_ref[...])
    return qu, kv, p[:, MLA_Q_RANK + MLA_KV_RANK:]


def _mla_heads(qu, kv, kr, cos, sa, sb, gq_ref, gk_ref, qt_ref, k_ref, vt_ref):
    scale = MLA_QK ** -0.5
    ss_kr = jnp.sum(kr * kr, axis=-1, keepdims=True)
    gq, gk = gq_ref[...], gk_ref[...]
    for b in range(HEADS // 2):
        qr = qu[:, HEADS * MLA_NOPE + b * LANES:HEADS * MLA_NOPE + (b + 1) * LANES]
        low, ss_lo, ss_hi = _half_sums(qr * qr)
        inv = []
        for j in range(2):
            h = 2 * b + j
            qn = qu[:, h * MLA_NOPE:(h + 1) * MLA_NOPE]
            ss = jnp.sum(qn * qn, axis=-1, keepdims=True) + (ss_lo, ss_hi)[j]
            inv.append(lax.rsqrt(ss * (1.0 / MLA_QK) + EPS))
            qt_ref[h, 0:MLA_NOPE, :] = (qn * inv[j] * gq[0:1] * scale).T.astype(BF16)
        yrt = (_rope128(qr * jnp.where(low, inv[0], inv[1]) * gq[1:2], cos, sa, sb) * scale).T
        qt_ref[2 * b, MLA_NOPE:MLA_QK, :] = yrt[:MLA_ROPE].astype(BF16)
        qt_ref[2 * b + 1, MLA_NOPE:MLA_QK, :] = yrt[MLA_ROPE:].astype(BF16)
    low = lax.broadcasted_iota(jnp.int32, kr.shape, 1) < 64
    gk_rope = jnp.where(low, gk[1:2], 0.0)
    for h in range(HEADS):
        kn = kv[:, h * MLA_NOPE:(h + 1) * MLA_NOPE]
        inv = lax.rsqrt((jnp.sum(kn * kn, axis=-1, keepdims=True) + ss_kr) * (1.0 / MLA_QK) + EPS)
        k_ref[h, :, 0:MLA_NOPE] = (kn * inv * gk[0:1]).astype(BF16)
        yr = _rope128(kr * inv * gk_rope, cos, sa, sb)
        k_ref[h, :, MLA_NOPE:MLA_QK] = yr[:, :MLA_ROPE].astype(BF16)
        vt_ref[h] = kv[:, HEADS * MLA_NOPE + h * MLA_V:HEADS * MLA_NOPE + (h + 1) * MLA_V].T.astype(BF16)


def _even_front_kernel(x_ref, mod_ref, nw_ref, wm_ref, wa_ref, cos_ref, sa_ref, sb_ref, qn_ref, kvn_ref,
                       wuq_ref, wukv_ref, gq_ref, gk_ref, pm_ref, qt_ref, k_ref, vt_ref, *, m_ctx, rows):
    h = _mixer_input(x_ref, mod_ref, nw_ref, m_ctx, rows)
    qu, kv, kr = _mla_up(_dot(h, wa_ref[...]), qn_ref, kvn_ref, wuq_ref, wukv_ref)
    pm_ref[...] = _dot(h, wm_ref[...])
    _mla_heads(qu, kv, kr, cos_ref[...], sa_ref[...], sb_ref[...], gq_ref, gk_ref, qt_ref, k_ref, vt_ref)


def _even_front(xt, mod, nw, w_main, w_mla, tables, qn, kvn, wuq, wukv, gq, gk, m_ctx):
    t, d = xt.shape
    rows = _row_block(t, ROW_BLOCK)
    rowb = lambda w: pl.BlockSpec((rows, w), lambda i: (i, 0))
    consts = (w_main, w_mla), (qn, kvn, wuq, wukv, gq, gk)
    return pl.pallas_call(
        functools.partial(_even_front_kernel, m_ctx=m_ctx, rows=rows),
        grid=(t // rows,),
        in_specs=[rowb(d), _resident(mod), _resident(nw)] + [_resident(a) for a in consts[0]]
                 + [rowb(LANES)] * 3 + [_resident(a) for a in consts[1]],
        out_specs=[rowb(w_main.shape[1]),
                   pl.BlockSpec((HEADS, MLA_QK, rows), lambda i: (0, 0, i)),
                   pl.BlockSpec((HEADS, rows, MLA_QK), lambda i: (0, i, 0)),
                   pl.BlockSpec((HEADS, ATT_V, rows), lambda i: (0, 0, i))],
        out_shape=[jax.ShapeDtypeStruct((t, w_main.shape[1]), F32),
                   jax.ShapeDtypeStruct((HEADS, MLA_QK, t), BF16),
                   jax.ShapeDtypeStruct((HEADS, t, MLA_QK), BF16),
                   jax.ShapeDtypeStruct((HEADS, ATT_V, t), BF16)],
        compiler_params=_cparams(("parallel",)),
        name="even_front",
    )(xt, mod, nw, w_main, w_mla, *tables, qn, kvn, wuq, wukv, gq, gk)


def _pv_and_sums(vt, p):
    sums = p.reshape(p.shape[0] // SUBLANES, SUBLANES, p.shape[1]).sum(axis=0)
    return jnp.concatenate([_dot(vt, p.astype(BF16)), sums], axis=0)


def _attend(items, first):
    scores = [_dot(k_ref[rows, :], qt) for qt, k_ref, _, _, _, rows in items]
    for (_, _, vt_ref, acc_sc, m_sc, rows), s in zip(items, scores):
        vt = vt_ref[:, rows]
        if m_sc is None:
            pv = _pv_and_sums(vt, jnp.exp(s))
            acc_sc[...] = pv if first else acc_sc[...] + pv
        else:
            m_new = jnp.max(s, axis=0, keepdims=True)
            if not first:
                m_prev = m_sc[...]
                m_new = jnp.maximum(m_prev, m_new)
            pv = _pv_and_sums(vt, jnp.exp(s - m_new))
            acc_sc[...] = pv if first else jnp.exp(m_prev - m_new) * acc_sc[...] + pv
            m_sc[...] = m_new


def _attend_all(qts, k_refs, vt_refs, accs, ms, m_ctx, bq, bk, n_blocks, finish):
    pairs = list(zip(qts, k_refs, vt_refs, accs, ms))
    ctx_items = [p + (slice(0, m_ctx),) for p in pairs]
    split = max(1, 2 // len(pairs))
    sub = bk // split
    is_latent = pl.program_id(1) >= m_ctx // bq

    @pl.when(jnp.logical_not(is_latent))
    def _():
        _attend(ctx_items, True)
        finish()

    @pl.when(is_latent)
    def _():
        _attend(ctx_items, True)
        for j in range(n_blocks):
            _attend([p + (slice(m_ctx + j * bk + u * sub, m_ctx + j * bk + (u + 1) * sub),)
                     for p in pairs for u in range(split)], False)
        finish()


def _normalised(acc_sc):
    acc = acc_sc[...]
    return (acc[:ATT_V] / jnp.sum(acc[ATT_V:], axis=0, keepdims=True)).T


def _mla_attn_kernel(qt_ref, k_ref, vt_ref, o_ref, a0, a1, *m_sc, m_ctx, bq, bk, n_blocks):
    ms = list(m_sc) if m_sc else [None, None]

    def finish():
        o_ref[:, 0:MLA_V] = _normalised(a0)
        o_ref[:, MLA_V:] = _normalised(a1)

    _attend_all([qt_ref[0], qt_ref[1]], [k_ref.at[0], k_ref.at[1]], [vt_ref.at[0], vt_ref.at[1]], [a0, a1], ms,
                m_ctx, bq, bk, n_blocks, finish)


def _attn_blocks(t, m_ctx):
    bq = ATT_BQ
    assert m_ctx % bq == 0 and t % bq == 0
    n_lat = t - m_ctx
    bk = _row_block(n_lat, ATT_BK)
    return bq, bk, n_lat // bk


def _attn_scratch(bq, n_pairs, shifted):
    return [pltpu.VMEM((ACC_ROWS, bq), F32)] * n_pairs + ([pltpu.VMEM((1, bq), F32)] * n_pairs if shifted else [])


def _by_score_bound(bound, attend, *operands):
    return lax.cond(bound <= SCORE_LIMIT,
                    functools.partial(attend, shifted=False),
                    functools.partial(attend, shifted=True), *operands)


def _mla_attn(qt, k, vt, m_ctx, shifted):
    t = k.shape[1]
    bq, bk, n_blocks = _attn_blocks(t, m_ctx)
    return pl.pallas_call(
        functools.partial(_mla_attn_kernel, m_ctx=m_ctx, bq=bq, bk=bk, n_blocks=n_blocks),
        grid=(HEADS // 2, t // bq),
        in_specs=[pl.BlockSpec((2, MLA_QK, bq), lambda h, i: (h, 0, i)),
                  pl.BlockSpec((2, t, MLA_QK), lambda h, i: (h, 0, 0), pipeline_mode=pl.Buffered(1)),
                  pl.BlockSpec((2, ATT_V, t), lambda h, i: (h, 0, 0), pipeline_mode=pl.Buffered(1))],
        out_specs=pl.BlockSpec((bq, 2 * MLA_V), lambda h, i: (i, h)),
        out_shape=jax.ShapeDtypeStruct((t, HEADS * MLA_V), F32),
        scratch_shapes=_attn_scratch(bq, 2, shifted),
        compiler_params=_cparams(("parallel", "arbitrary")),
        name="mla_attn_shifted" if shifted else "mla_attn",
    )(qt, k, vt)


def _diff_attn_kernel(q1_ref, q2_ref, k1_ref, k2_ref, v_ref, lam_ref, sub_ref, o_ref, a1, a2, *m_sc,
                      m_ctx, bq, bk, n_blocks, out_scale):
    ms = list(m_sc) if m_sc else [None, None]

    def finish():
        d = _normalised(a1) - lam_ref[...] * _normalised(a2)
        y = d * lax.rsqrt(jnp.mean(d * d, axis=-1, keepdims=True) + EPS)
        o_ref[...] = y * sub_ref[...] * out_scale

    _attend_all([q1_ref[0], q2_ref[0]], [k1_ref.at[0], k2_ref.at[0]], [v_ref.at[0]] * 2, [a1, a2], ms,
                m_ctx, bq, bk, n_blocks, finish)


def _diff_attn(dqt, dk, dvt, lam, subln, m_ctx, out_scale, shifted):
    t = dk.shape[1]
    bq, bk, n_blocks = _attn_blocks(t, m_ctx)
    qspec = lambda c: pl.BlockSpec((1, DA_DIM, bq), lambda h, i: (2 * h + c, 0, i))
    kspec = lambda c: pl.BlockSpec((1, t, DA_DIM), lambda h, i: (2 * h + c, 0, 0))
    vec = pl.BlockSpec((1, DA_V), lambda h, i: (0, 0))
    return pl.pallas_call(
        functools.partial(_diff_attn_kernel, m_ctx=m_ctx, bq=bq, bk=bk, n_blocks=n_blocks,
                          out_scale=out_scale),
        grid=(HEADS, t // bq),
        in_specs=[qspec(0), qspec(1), kspec(0), kspec(1),
                  pl.BlockSpec((1, ATT_V, t), lambda h, i: (h, 0, 0)), vec, vec],
        out_specs=pl.BlockSpec((bq, DA_V), lambda h, i: (i, h)),
        out_shape=jax.ShapeDtypeStruct((t, HEADS * DA_V), F32),
        scratch_shapes=_attn_scratch(bq, 2, shifted),
        compiler_params=_cparams(("parallel", "arbitrary")),
        name="diff_attn_shifted" if shifted else "diff_attn",
    )(dqt, dqt, dk, dk, dvt, lam, subln)


ODD_ATT_W = 3 * HEADS * 2 * DA_DIM + 2 * HEADS * RT_K


def _odd_prep(p, cos, sa, sb, gq_ref, gk_ref, dqt_ref, dk_ref, dvt_ref, rq_ref, rk_ref):
    da_w = HEADS * 2 * DA_DIM

    def sub_rms(x, g):
        low, ss_lo, ss_hi = _half_sums(x * x)
        inv = jnp.where(low, lax.rsqrt(ss_lo * (1.0 / DA_DIM) + EPS), lax.rsqrt(ss_hi * (1.0 / DA_DIM) + EPS))
        return x * inv * g

    for h in range(HEADS):
        col = h * LANES
        yqt = (_rope128(sub_rms(p[:, col:col + LANES], gq_ref[...]), cos, sa, sb) * (DA_DIM ** -0.5)).T
        dqt_ref[2 * h] = yqt[:DA_DIM].astype(BF16)
        dqt_ref[2 * h + 1] = yqt[DA_DIM:].astype(BF16)
        yk = _rope128(sub_rms(p[:, da_w + col:da_w + col + LANES], gk_ref[...]), cos, sa, sb)
        dk_ref[2 * h] = yk[:, :DA_DIM].astype(BF16)
        dk_ref[2 * h + 1] = yk[:, DA_DIM:].astype(BF16)
        dvt_ref[h] = p[:, 2 * da_w + col:2 * da_w + col + LANES].T.astype(BF16)
    r0 = 2 * da_w + HEADS * DA_V
    for b in range(HEADS // 2):
        col = r0 + b * LANES
        rq_ref[:, b * LANES:(b + 1) * LANES] = _rope128(p[:, col:col + LANES], cos, sa, sb)
        col = r0 + HEADS * RT_K + b * LANES
        rk_ref[:, b * LANES:(b + 1) * LANES] = _rope128(p[:, col:col + LANES] * (RT_K ** -0.5), cos, sa, sb)


def _odd_front_kernel(x_ref, mod_ref, nw_ref, wa_ref, wr_ref, cos_ref, sa_ref, sb_ref, gq_ref, gk_ref,
                      dqt_ref, dk_ref, dvt_ref, rq_ref, rk_ref, rest_ref, *, m_ctx, rows):
    h = _mixer_input(x_ref, mod_ref, nw_ref, m_ctx, rows)
    p_att = _dot(h, wa_ref[...])
    rest_ref[...] = _dot(h, wr_ref[...])
    _odd_prep(p_att, cos_ref[...], sa_ref[...], sb_ref[...], gq_ref, gk_ref,
              dqt_ref, dk_ref, dvt_ref, rq_ref, rk_ref)


def _odd_front(xt, mod, nw, w_in, tables, gq, gk, m_ctx):
    t, d = xt.shape
    rows = _row_block(t, ROW_BLOCK)
    w_att, w_rest = w_in[:, :ODD_ATT_W], w_in[:, ODD_ATT_W:]
    rowb = lambda w: pl.BlockSpec((rows, w), lambda i: (i, 0))
    hm = lambda n, w: pl.BlockSpec((n, rows, w), lambda i: (0, i, 0))
    return pl.pallas_call(
        functools.partial(_odd_front_kernel, m_ctx=m_ctx, rows=rows),
        grid=(t // rows,),
        in_specs=[rowb(d), _resident(mod), _resident(nw), _resident(w_att), _resident(w_rest)]
                 + [rowb(LANES)] * 3 + [_resident(gq), _resident(gk)],
        out_specs=[pl.BlockSpec((2 * HEADS, DA_DIM, rows), lambda i: (0, 0, i)),
                   hm(2 * HEADS, DA_DIM),
                   pl.BlockSpec((HEADS, ATT_V, rows), lambda i: (0, 0, i)), rowb(HEADS * RT_K), rowb(HEADS * RT_K),
                   rowb(w_rest.shape[1])],
        out_shape=[jax.ShapeDtypeStruct((2 * HEADS, DA_DIM, t), BF16),
                   jax.ShapeDtypeStruct((2 * HEADS, t, DA_DIM), BF16),
                   jax.ShapeDtypeStruct((HEADS, ATT_V, t), BF16),
                   jax.ShapeDtypeStruct((t, HEADS * RT_K), F32),
                   jax.ShapeDtypeStruct((t, HEADS * RT_K), F32),
                   jax.ShapeDtypeStruct((t, w_rest.shape[1]), F32)],
        compiler_params=_cparams(("parallel",)),
        name="odd_front",
    )(xt, mod, nw, w_att, w_rest, *tables, gq, gk)


def _back_kernel(x_ref, att_ref, of_ref, ob_ref, gate_ref, nrm_ref, wa_ref, wr_ref, mod_ref, nw_ref, w1_ref, w2_ref,
                 o_ref, *, m_ctx, rows):
    row0 = pl.program_id(0) * rows
    o = of_ref[...] + ob_ref[...]
    gate = gate_ref[...]
    rec = []
    for h in range(HEADS):
        oh = o[:, h * LANES:(h + 1) * LANES]
        inv = lax.rsqrt(jnp.mean(oh * oh, axis=-1, keepdims=True) + EPS)
        rec.append(oh * inv * nrm_ref[...] * _silu(gate[:, h * LANES:(h + 1) * LANES]))
    rec = jnp.concatenate(rec, axis=-1).astype(BF16)
    y = _dot(att_ref[...].astype(BF16), wa_ref[...]) + _dot(rec, wr_ref[...])
    x1 = x_ref[...] + _row_mod(mod_ref, 2, row0, rows, m_ctx) * y
    h = _modnorm(x1, nw_ref[...], _row_mod(mod_ref, 4, row0, rows, m_ctx),
                 _row_mod(mod_ref, 3, row0, rows, m_ctx)).astype(BF16)
    u = jnp.maximum(_dot(h, w1_ref[...]), 0.0)
    o_ref[...] = x1 + _row_mod(mod_ref, 5, row0, rows, m_ctx) * _dot((u * u).astype(BF16), w2_ref[...])


def _back(xt, att, o_f, o_b, gate_arr, gate_block, nrm, w_att, w_rec, mod, nw, w1, w2, m_ctx):
    t, d = xt.shape
    rows = _row_block(t, ROW_BLOCK)
    w = att.shape[1]
    rowb = lambda c: pl.BlockSpec((rows, c), lambda i: (i, 0))
    return pl.pallas_call(
        functools.partial(_back_kernel, m_ctx=m_ctx, rows=rows),
        grid=(t // rows,),
        in_specs=[rowb(d), rowb(w), rowb(w), rowb(w),
                  pl.BlockSpec((rows, w), lambda i: (i, gate_block))]
                 + [_resident(a) for a in (nrm, w_att, w_rec, mod, nw, w1, w2)],
        out_specs=rowb(d),
        out_shape=jax.ShapeDtypeStruct((t, d), F32),
        compiler_params=_cparams(("parallel",)),
        name="layer_back",
    )(xt, att, o_f, o_b, gate_arr, nrm, w_att, w_rec, mod, nw, w1, w2)


def _head_major(w, parts):
    k = w.shape[0]
    wh = w.reshape(k, HEADS, sum(parts))
    out, off = [], 0
    for width in parts:
        out.append(wh[:, :, off:off + width].reshape(k, HEADS * width))
        off += width
    return jnp.concatenate(out, axis=1)


def kernel(x, c, ctx, c_ctx, ada_w, ada_b, norm_w, w_o, mlp_w1, mlp_w2, a_w_in, hg_lb, hg_norm, mla_q_norm,
           mla_kv_norm, mla_w_uq, mla_w_ukv, mla_qk_q, mla_qk_k, c_w_in, da_lambda, da_qk_q, da_qk_k,
           da_subln, rt_decay, rt_norm):
    assert x.shape[0] == 1 and ctx.shape[0] == 1
    n, d = x.shape[1], x.shape[2]
    m_ctx = ctx.shape[1]
    depth = ada_w.shape[0]
    xt = jnp.concatenate([ctx[0], x[0]], axis=0)

    cvec = jnp.zeros((8, d), F32).at[0].set(c_ctx).at[1].set(c[0])
    mods = _ada_table(cvec, ada_w, ada_b)[:, :2].reshape(depth, 2, 6, d)
    tables = _rope_tables(n, m_ctx)

    lb = jnp.cumsum(jax.nn.softmax(hg_lb.astype(F32), axis=0), axis=0)
    lb = lb - lb[:1]
    log_lb = jnp.log(lb).reshape(-1, 2, 1, HG_W)
    log1m_lb = jnp.log1p(-lb).reshape(-1, 2, 1, HG_W)

    for l in range(depth):
        j = l // 2
        mod = mods[l]
        nw = norm_w[l]
        wo = w_o[l].astype(BF16)
        if l % 2 == 0:
            w_in = a_w_in[j]
            w_main = w_in[:, :5 * HG_W].astype(BF16)
            w_mla = jnp.pad(w_in[:, 5 * HG_W:], ((0, 0), (0, MLA_IN - (w_in.shape[1] - 5 * HG_W)))).astype(BF16)
            gq = mla_qk_q[j]
            gk = mla_qk_k[j]
            p_main, qt, k, vt = _even_front(
                xt, mod, nw[0:1], w_main, w_mla, tables, mla_q_norm[j][None], mla_kv_norm[j][None],
                _head_major(mla_w_uq[j], (MLA_NOPE, MLA_ROPE)).astype(BF16),
                _head_major(mla_w_ukv[j], (MLA_NOPE, MLA_V)).astype(BF16),
                jnp.stack([gq[:MLA_NOPE], jnp.tile(gq[MLA_NOPE:], 2)]),
                jnp.stack([gk[:MLA_NOPE], jnp.tile(gk[MLA_NOPE:], 2)]), m_ctx)
            o_f, o_b = _hgrn2(p_main, log_lb[j], log1m_lb[j], m_ctx)
            bound = 1.02 * MLA_QK ** 0.5 * jnp.max(jnp.abs(gq)) * jnp.max(jnp.abs(gk))
            att = _by_score_bound(bound, functools.partial(_mla_attn, m_ctx=m_ctx), qt, k, vt)
            back = (att, o_f, o_b, p_main, 4, hg_norm[j][None], wo[HG_W:], wo[:HG_W])
        else:
            dq, dkt, dv, rq, rk, p = _odd_front(xt, mod, nw[0:1], c_w_in[j].astype(BF16), tables,
                                                 jnp.tile(da_qk_q[j], 2)[None], jnp.tile(da_qk_k[j], 2)[None], m_ctx)
            lam_init = 0.8 - 0.6 * math.exp(-0.3 * l)
            lf = da_lambda[j].astype(F32)
            lam = jnp.exp(jnp.sum(lf[0] * lf[1])) - jnp.exp(jnp.sum(lf[2] * lf[3])) + lam_init
            bound = 1.02 * DA_DIM ** 0.5 * jnp.max(jnp.abs(da_qk_q[j])) * jnp.max(jnp.abs(da_qk_k[j]))
            att = _by_score_bound(
                bound, functools.partial(_diff_attn, m_ctx=m_ctx, out_scale=1.0 - lam_init),
                dq, dkt, dv, jnp.full((1, DA_V), lam, F32), da_subln[j][None])
            lg = jax.nn.log_sigmoid(rt_decay[j].astype(F32))
            lg = jnp.broadcast_to(lg[:, :, None, None], (2, HEADS, 8, LANES))
            r_f, r_b = _retention(rq, rk, p, 0, lg, m_ctx)
            back = (att, r_f, r_b, p, 1, rt_norm[j][None], wo[:HG_W], wo[HG_W:])
        xt = _back(xt, *back, mod, nw[1:2], mlp_w1[l].astype(BF16), mlp_w2[l].astype(BF16), m_ctx)
    return xt[m_ctx:][None]
```

```python
import functools
import math

import numpy as np
import jax
import jax.numpy as jnp
from jax import lax
from jax.experimental import pallas as pl
from jax.experimental.pallas import tpu as pltpu

F32 = jnp.float32
BF16 = jnp.bfloat16

GRID_W = 64
ROPE_DIM = 64
ROPE_BASE = 10000.0
EPS = 1e-6
HEADS = 4
HG_DIM = 128
HG_W = HEADS * HG_DIM
MLA_NOPE = 128
MLA_ROPE = ROPE_DIM
MLA_V = 128
MLA_QK = MLA_NOPE + MLA_ROPE
MLA_Q_RANK = 384
MLA_KV_RANK = 256
DA_DIM = ROPE_DIM
DA_V = 2 * DA_DIM
RT_K = ROPE_DIM
RT_V = 128
MLA_IN = 768

LANES = 128
VMEM_LIMIT = 56 * 1024 * 1024
ROW_BLOCK = 640
ATT_BQ = 256
ATT_BK = 4096
SCAN_CHUNK = 64
SCAN_SUB = 4
RET_CHUNK = 128
RET_SUB = 2
SUBLANES = 8
ATT_V = 128
ACC_ROWS = ATT_V + SUBLANES
SCORE_LIMIT = 40.0


def _cparams(sem):
    return pltpu.CompilerParams(dimension_semantics=sem, vmem_limit_bytes=VMEM_LIMIT)


def _row_block(t, target):
    best = None
    for r in range(LANES, min(t, target) + 1, LANES):
        if t % r == 0:
            best = r
    assert best is not None, t
    return best


def _dot(a, b):
    return jnp.dot(a, b, preferred_element_type=F32)


def _dot_nt(a, b):
    return lax.dot_general(a, b, (((1,), (1,)), ((), ())), preferred_element_type=F32)


def _dot_tn(a, b):
    return lax.dot_general(a, b, (((0,), (0,)), ((), ())), preferred_element_type=F32)


def _silu(x):
    return x * (1.0 / (1.0 + jnp.exp(-x)))


def _row_mod(mod_ref, k, row0, rows, m_ctx):
    r = row0 + lax.broadcasted_iota(jnp.int32, (rows, 1), 0)
    return jnp.where(r < m_ctx, mod_ref[0, k:k + 1, :], mod_ref[1, k:k + 1, :])


def _modnorm(x, nw, sc, sh):
    y = x * lax.rsqrt(jnp.mean(x * x, axis=-1, keepdims=True) + EPS)
    return y * nw * (1.0 + sc) + sh


def _ada_kernel(c_ref, w_ref, b_ref, o_ref):
    cv = c_ref[...]
    o_ref[0] = jnp.dot(_silu(cv), w_ref[0], precision=lax.Precision.HIGHEST,
                       preferred_element_type=F32) + b_ref[0]


def _ada_table(cvec, ada_w, ada_b):
    depth, d, d6 = ada_w.shape
    tn = d6 // 4
    return pl.pallas_call(
        _ada_kernel,
        grid=(depth, d6 // tn),
        in_specs=[pl.BlockSpec((8, d), lambda l, j: (0, 0)),
                  pl.BlockSpec((1, d, tn), lambda l, j: (l, 0, j)),
                  pl.BlockSpec((1, 1, tn), lambda l, j: (l, 0, j))],
        out_specs=pl.BlockSpec((1, 8, tn), lambda l, j: (l, 0, j)),
        out_shape=jax.ShapeDtypeStruct((depth, 8, d6), F32),
        compiler_params=_cparams(("parallel", "parallel")),
        name="ada_table",
    )(cvec, ada_w, ada_b.reshape(depth, 1, d6))


def _mixer_input(x_ref, mod_ref, nw_ref, m_ctx, rows):
    row0 = pl.program_id(0) * rows
    return _modnorm(x_ref[...], nw_ref[...],
                    _row_mod(mod_ref, 1, row0, rows, m_ctx),
                    _row_mod(mod_ref, 0, row0, rows, m_ctx)).astype(BF16)


def _resident(a):
    return pl.BlockSpec(a.shape, lambda i: (0,) * a.ndim, pipeline_mode=pl.Buffered(1))


def _hier_tables(c, reverse):
    levels = int(math.log2(c))
    assert 1 << levels == c
    cums = np.zeros(((levels + 1) * c, c), np.float32)
    roles = np.zeros((levels, c, LANES), np.float32)
    masks = np.zeros((levels + 1, c, c), np.float32)
    for li in range(levels):
        h = c >> (li + 1)
        for t in range(c):
            base = (t // (2 * h)) * 2 * h
            late = (t - base) >= h
            if not reverse:
                if late:
                    cums[li * c + t, base + h:t + 1] = 1.0
                else:
                    cums[li * c + t, t + 1:base + h] = 1.0
            else:
                if late:
                    cums[li * c + t, base + h:t] = 1.0
                else:
                    cums[li * c + t, t:base + h] = 1.0
            is_query = late != reverse
            roles[li, t, :] = 1.0 if is_query else 0.0
        for t in range(c):
            for s in range(c):
                same = (t // (2 * h)) == (s // (2 * h))
                if same and roles[li, t, 0] == 1.0 and roles[li, s, 0] == 0.0:
                    masks[li, t, s] = 1.0
    for t in range(c):
        if not reverse:
            cums[levels * c + t, :t + 1] = 1.0
        else:
            cums[levels * c + t, t:] = 1.0
    masks[levels] = np.eye(c, dtype=np.float32)
    return cums, roles, masks


def _split3(x):
    hi = x.astype(BF16)
    r1 = x - hi.astype(F32)
    mid = r1.astype(BF16)
    lo = (r1 - mid.astype(F32)).astype(BF16)
    return hi, mid, lo


def _gated_chunks(sides, c):
    levels = sides[0][5].shape[0]
    chains = []
    for q_all, kk_all, v_all, logf_all, cum_ref, role_ref, mask_ref, st_ref, o_ref, reverse in sides:
        x_all = _dot(cum_ref[...], jnp.concatenate(_split3(logf_all), axis=0))
        for h in range(HEADS):
            sl = slice(h * HG_DIM, (h + 1) * HG_DIM)
            chains.append((x_all[:, sl], q_all[:, sl], kk_all[:, sl], v_all[:, sl].astype(BF16),
                           role_ref, mask_ref, st_ref, o_ref, reverse, h, sl))
    acc = [_dot_nt(q.astype(BF16), kk.astype(BF16)) * mask_ref[levels]
           for _, q, kk, _, _, mask_ref, *_ in chains]
    for li in range(levels):
        for n, (x, q, kk, _, role_ref, mask_ref, *_) in enumerate(chains):
            z = (jnp.where(role_ref[li] > 0.5, q, kk) * jnp.exp(x[li * c:(li + 1) * c])).astype(BF16)
            acc[n] = acc[n] + _dot_nt(z, z) * mask_ref[li]
    for a, (x, q, kk, vb, _, _, st_ref, o_ref, reverse, h, sl) in zip(acc, chains):
        run = x[levels * c:(levels + 1) * c]
        tot = run[0:1] if reverse else run[c - 1:c]
        st = st_ref[h]
        o_ref[:, sl] = _dot(a.astype(BF16), vb) + _dot_nt((q * jnp.exp(run)).astype(BF16), st.astype(BF16))
        kd = (kk * jnp.exp(tot - run)).astype(BF16)
        st_ref[h] = st * jnp.exp(tot) + _dot_tn(vb, kd)


def _hgrn2_gate(z, log_lb, log1m_lb):
    soft = jnp.log(1.0 + jnp.exp(-jnp.abs(z)))
    b = log1m_lb + (jnp.minimum(z, 0.0) - soft)
    logf = jnp.maximum(log_lb, b) + jnp.log(1.0 + jnp.exp(-jnp.abs(log_lb - b)))
    return logf, jnp.exp(log1m_lb + (jnp.minimum(-z, 0.0) - soft))


def _hgrn2_kernel(qf_ref, zf_ref, vf_ref, qb_ref, zb_ref, vb_ref, llb_ref, l1m_ref,
                  cumf_ref, rolef_ref, maskf_ref, cumb_ref, roleb_ref, maskb_ref,
                  of_ref, ob_ref, sf_ref, sb_ref, *, c, sub):
    @pl.when(pl.program_id(0) == 0)
    def _():
        sf_ref[...] = jnp.zeros_like(sf_ref)
        sb_ref[...] = jnp.zeros_like(sb_ref)

    sides = []
    for u in range(sub):
        rf, rb = pl.ds(u * c, c), pl.ds((sub - 1 - u) * c, c)
        logf, kk = _hgrn2_gate(zf_ref[rf, :], llb_ref[0], l1m_ref[0])
        sides.append((_silu(qf_ref[rf, :]), kk, vf_ref[rf, :], logf, cumf_ref, rolef_ref, maskf_ref, sf_ref,
                      of_ref.at[rf], False))
        logf, kk = _hgrn2_gate(zb_ref[rb, :], llb_ref[1], l1m_ref[1])
        sides.append((_silu(qb_ref[rb, :]), kk, vb_ref[rb, :], logf, cumb_ref, roleb_ref, maskb_ref, sb_ref,
                      ob_ref.at[rb], True))
    for side in sides:
        _gated_chunks([side], c)


def _bwd_chunk(i, mc, steps):
    return jnp.where(i < mc, mc - 1 - i, steps - 1 - i + mc)


def _hgrn2(p_main, log_lb, log1m_lb, m_ctx):
    t = p_main.shape[0]
    c, sub = SCAN_CHUNK, SCAN_SUB
    rows = c * sub
    steps, mc = t // rows, m_ctx // rows
    assert steps * rows == t and mc * rows == m_ctx
    tabs = [jnp.asarray(a) for rev in (False, True) for a in _hier_tables(c, rev)]
    for k in (0, 3):
        tabs[k] = jnp.tile(tabs[k], (1, 3)).astype(BF16)
    fwd = lambda sec: pl.BlockSpec((rows, HG_W), lambda i: (i, sec))
    bwd = lambda sec: pl.BlockSpec((rows, HG_W), lambda i: (_bwd_chunk(i, mc, steps), sec))
    full = lambda a: pl.BlockSpec(a.shape, lambda i: (0,) * a.ndim)
    return pl.pallas_call(
        functools.partial(_hgrn2_kernel, c=c, sub=sub),
        grid=(steps,),
        in_specs=[fwd(0), fwd(1), fwd(3), bwd(0), bwd(2), bwd(3), full(log_lb), full(log1m_lb)]
                 + [full(a) for a in tabs],
        out_specs=[pl.BlockSpec((rows, HG_W), lambda i: (i, 0)),
                   pl.BlockSpec((rows, HG_W), lambda i: (_bwd_chunk(i, mc, steps), 0))],
        out_shape=[jax.ShapeDtypeStruct((t, HG_W), F32)] * 2,
        scratch_shapes=[pltpu.VMEM((HEADS, HG_DIM, HG_DIM), F32)] * 2,
        compiler_params=_cparams(("arbitrary",)),
        name="hgrn2_scan",
    )(p_main, p_main, p_main, p_main, p_main, p_main, log_lb, log1m_lb, *tabs)


def _ret_head(x, h):
    low = lax.broadcasted_iota(jnp.int32, x.shape, 1) < RT_K
    return jnp.where(low if h % 2 == 0 else jnp.logical_not(low), x, 0.0)


def _ret_scores(q, k, lg, c, reverse):
    t = lax.broadcasted_iota(jnp.int32, (c, c), 0)
    s = lax.broadcasted_iota(jnp.int32, (c, c), 1)
    dlt = (s - t) if reverse else (t - s)
    dec = jnp.where(dlt >= 0, jnp.exp(lg[:, :c] * jnp.maximum(dlt, 0).astype(F32)), 0.0)
    return (_dot_nt(q.astype(BF16), k.astype(BF16)) * dec).astype(BF16)


def _ret_finish(a, q, k, v, lg, st_ref, c, reverse):
    r = lax.broadcasted_iota(jnp.int32, (c, LANES), 0).astype(F32)
    lk = lg
    qdec = jnp.exp(lk * ((c - r) if reverse else (r + 1.0)))
    kdec = jnp.exp(lk * (r if reverse else (c - 1.0 - r)))
    st = st_ref[...]
    vb = v.astype(BF16)
    o = _dot(a, vb) + _dot_nt((q * qdec).astype(BF16), st.astype(BF16))
    st_ref[...] = st * jnp.exp(lk * float(c)) + _dot_tn(vb, (k * kdec).astype(BF16))
    return o


def _ret_kernel(qf_ref, kf_ref, vf_ref, qb_ref, kb_ref, vb_ref, lg_ref,
                of_ref, ob_ref, sf_ref, sb_ref, *, c, sub):
    @pl.when(pl.program_id(0) == 0)
    def _():
        sf_ref[...] = jnp.zeros_like(sf_ref)
        sb_ref[...] = jnp.zeros_like(sb_ref)

    sides = ((qf_ref, kf_ref, vf_ref, of_ref, sf_ref), (qb_ref, kb_ref, vb_ref, ob_ref, sb_ref))
    chains = [(sides[d], d, h, pl.ds((sub - 1 - u if d else u) * c, c))
              for u in range(sub) for h in range(HEADS) for d in range(2)]
    blk = lambda h: slice(h // 2 * LANES, (h // 2 + 1) * LANES)
    scores = [_ret_scores(_ret_head(q_ref[rows, blk(h)], h), k_ref[rows, blk(h)], lg_ref[d, h, 0:1, :], c, d == 1)
              for (q_ref, k_ref, _, _, _), d, h, rows in chains]
    for a, ((q_ref, k_ref, v_ref, o_ref, s_ref), d, h, rows) in zip(scores, chains):
        sl = slice(h * RT_V, (h + 1) * RT_V)
        o_ref[rows, sl] = _ret_finish(a, _ret_head(q_ref[rows, blk(h)], h), k_ref[rows, blk(h)], v_ref[rows, sl],
                                      lg_ref[d, h, 0:1, :], s_ref.at[h], c, d == 1)


def _retention(rq, rk, p, v_block, lg, m_ctx):
    t = p.shape[0]
    c, sub = RET_CHUNK, RET_SUB
    rows = c * sub
    steps, mc = t // rows, m_ctx // rows
    assert steps * rows == t and mc * rows == m_ctx
    width = HEADS * RT_V
    fq = pl.BlockSpec((rows, HEADS * RT_K), lambda i: (i, 0))
    bq = pl.BlockSpec((rows, HEADS * RT_K), lambda i: (_bwd_chunk(i, mc, steps), 0))
    fv = pl.BlockSpec((rows, width), lambda i: (i, v_block))
    bv = pl.BlockSpec((rows, width), lambda i: (_bwd_chunk(i, mc, steps), v_block))
    return pl.pallas_call(
        functools.partial(_ret_kernel, c=c, sub=sub),
        grid=(steps,),
        in_specs=[fq, fq, fv, bq, bq, bv, pl.BlockSpec(lg.shape, lambda i: (0, 0, 0, 0))],
        out_specs=[pl.BlockSpec((rows, width), lambda i: (i, 0)),
                   pl.BlockSpec((rows, width), lambda i: (_bwd_chunk(i, mc, steps), 0))],
        out_shape=[jax.ShapeDtypeStruct((t, width), F32)] * 2,
        scratch_shapes=[pltpu.VMEM((HEADS, RT_V, LANES), F32)] * 2,
        compiler_params=_cparams(("arbitrary",)),
        name="retention_scan",
    )(rq, rk, p, rq, rk, p, lg)


def _rope128(y, cos, sin_a, sin_b):
    return y * cos + pltpu.roll(y, LANES - 16, 1) * sin_a + pltpu.roll(y, 16, 1) * sin_b


def _rope_tables(n, m_ctx):
    rows = n // GRID_W
    row = jnp.repeat(jnp.arange(rows, dtype=F32), GRID_W)
    col = jnp.tile(jnp.arange(GRID_W, dtype=F32), rows)
    quarter = ROPE_DIM // 4
    inv_freq = ROPE_BASE ** (-jnp.arange(quarter, dtype=F32) / quarter)
    ang_r = row[:, None] * inv_freq
    ang_c = col[:, None] * inv_freq
    ang = jnp.concatenate([ang_r, ang_r, ang_c, ang_c], axis=-1)
    cos, sin = jnp.cos(ang), jnp.sin(ang)
    first = (jnp.arange(ROPE_DIM) % 32) < 16
    sin_a = jnp.where(first, -sin, 0.0)
    sin_b = jnp.where(first, 0.0, sin)
    pad = lambda a, v: jnp.tile(jnp.concatenate([jnp.full((m_ctx, ROPE_DIM), v, F32), a], axis=0), (1, 2))
    return pad(cos, 1.0), pad(sin_a, 0.0), pad(sin_b, 0.0)


def _half_sums(sq):
    low = lax.broadcasted_iota(jnp.int32, sq.shape, 1) < 64
    lo = jnp.sum(jnp.where(low, sq, 0.0), axis=-1, keepdims=True)
    return low, lo, jnp.sum(sq, axis=-1, keepdims=True) - lo


def _mla_up(p, qn_ref, kvn_ref, wuq_ref, wukv_ref):
    def rms(x, w):
        return x * lax.rsqrt(jnp.mean(x * x, axis=-1, keepdims=True) + EPS) * w

    qu = _dot(rms(p[:, :MLA_Q_RANK], qn_ref[...]).astype(BF16), wuq_ref[...])
    kv = _dot(rms(p[:, MLA_Q_RANK:MLA_Q_RANK + MLA_KV_RANK], kvn_ref[...]).astype(BF16),
              wukv_ref[...])
    return qu, kv, p[:, MLA_Q_RANK + MLA_KV_RANK:]


def _mla_heads(qu, kv, kr, cos, sa, sb, gq_ref, gk_ref, qt_ref, k_ref, vt_ref):
    scale = MLA_QK ** -0.5
    ss_kr = jnp.sum(kr * kr, axis=-1, keepdims=True)
    gq, gk = gq_ref[...], gk_ref[...]
    for b in range(HEADS // 2):
        qr = qu[:, HEADS * MLA_NOPE + b * LANES:HEADS * MLA_NOPE + (b + 1) * LANES]
        low, ss_lo, ss_hi = _half_sums(qr * qr)
        inv = []
        for j in range(2):
            h = 2 * b + j
            qn = qu[:, h * MLA_NOPE:(h + 1) * MLA_NOPE]
            ss = jnp.sum(qn * qn, axis=-1, keepdims=True) + (ss_lo, ss_hi)[j]
            inv.append(lax.rsqrt(ss * (1.0 / MLA_QK) + EPS))
            qt_ref[h, 0:MLA_NOPE, :] = (qn * inv[j] * gq[0:1] * scale).T.astype(BF16)
        yrt = (_rope128(qr * jnp.where(low, inv[0], inv[1]) * gq[1:2], cos, sa, sb) * scale).T
        qt_ref[2 * b, MLA_NOPE:MLA_QK, :] = yrt[:MLA_ROPE].astype(BF16)
        qt_ref[2 * b + 1, MLA_NOPE:MLA_QK, :] = yrt[MLA_ROPE:].astype(BF16)
    low = lax.broadcasted_iota(jnp.int32, kr.shape, 1) < 64
    gk_rope = jnp.where(low, gk[1:2], 0.0)
    for h in range(HEADS):
        kn = kv[:, h * MLA_NOPE:(h + 1) * MLA_NOPE]
        inv = lax.rsqrt((jnp.sum(kn * kn, axis=-1, keepdims=True) + ss_kr) * (1.0 / MLA_QK) + EPS)
        k_ref[h, :, 0:MLA_NOPE] = (kn * inv * gk[0:1]).astype(BF16)
        yr = _rope128(kr * inv * gk_rope, cos, sa, sb)
        k_ref[h, :, MLA_NOPE:MLA_QK] = yr[:, :MLA_ROPE].astype(BF16)
        vt_ref[h] = kv[:, HEADS * MLA_NOPE + h * MLA_V:HEADS * MLA_NOPE + (h + 1) * MLA_V].T.astype(BF16)


def _even_front_kernel(x_ref, mod_ref, nw_ref, wm_ref, wa_ref, cos_ref, sa_ref, sb_ref, qn_ref, kvn_ref,
                       wuq_ref, wukv_ref, gq_ref, gk_ref, pm_ref, qt_ref, k_ref, vt_ref, *, m_ctx, rows):
    h = _mixer_input(x_ref, mod_ref, nw_ref, m_ctx, rows)
    qu, kv, kr = _mla_up(_dot(h, wa_ref[...]), qn_ref, kvn_ref, wuq_ref, wukv_ref)
    pm_ref[...] = _dot(h, wm_ref[...])
    _mla_heads(qu, kv, kr, cos_ref[...], sa_ref[...], sb_ref[...], gq_ref, gk_ref, qt_ref, k_ref, vt_ref)


def _even_front(xt, mod, nw, w_main, w_mla, tables, qn, kvn, wuq, wukv, gq, gk, m_ctx):
    t, d = xt.shape
    rows = _row_block(t, ROW_BLOCK)
    rowb = lambda w: pl.BlockSpec((rows, w), lambda i: (i, 0))
    consts = (w_main, w_mla), (qn, kvn, wuq, wukv, gq, gk)
    return pl.pallas_call(
        functools.partial(_even_front_kernel, m_ctx=m_ctx, rows=rows),
        grid=(t // rows,),
        in_specs=[rowb(d), _resident(mod), _resident(nw)] + [_resident(a) for a in consts[0]]
                 + [rowb(LANES)] * 3 + [_resident(a) for a in consts[1]],
        out_specs=[rowb(w_main.shape[1]),
                   pl.BlockSpec((HEADS, MLA_QK, rows), lambda i: (0, 0, i)),
                   pl.BlockSpec((HEADS, rows, MLA_QK), lambda i: (0, i, 0)),
                   pl.BlockSpec((HEADS, ATT_V, rows), lambda i: (0, 0, i))],
        out_shape=[jax.ShapeDtypeStruct((t, w_main.shape[1]), F32),
                   jax.ShapeDtypeStruct((HEADS, MLA_QK, t), BF16),
                   jax.ShapeDtypeStruct((HEADS, t, MLA_QK), BF16),
                   jax.ShapeDtypeStruct((HEADS, ATT_V, t), BF16)],
        compiler_params=_cparams(("parallel",)),
        name="even_front",
    )(xt, mod, nw, w_main, w_mla, *tables, qn, kvn, wuq, wukv, gq, gk)


def _pv_and_sums(vt, p):
    sums = p.reshape(p.shape[0] // SUBLANES, SUBLANES, p.shape[1]).sum(axis=0)
    return jnp.concatenate([_dot(vt, p.astype(BF16)), sums], axis=0)


def _attend(items, first):
    scores = [_dot(k_ref[rows, :], qt) for qt, k_ref, _, _, _, rows in items]
    for (_, _, vt_ref, acc_sc, m_sc, rows), s in zip(items, scores):
        vt = vt_ref[:, rows]
        if m_sc is None:
            pv = _pv_and_sums(vt, jnp.exp(s))
            acc_sc[...] = pv if first else acc_sc[...] + pv
        else:
            m_new = jnp.max(s, axis=0, keepdims=True)
            if not first:
                m_prev = m_sc[...]
                m_new = jnp.maximum(m_prev, m_new)
            pv = _pv_and_sums(vt, jnp.exp(s - m_new))
            acc_sc[...] = pv if first else jnp.exp(m_prev - m_new) * acc_sc[...] + pv
            m_sc[...] = m_new


def _attend_all(qts, k_refs, vt_refs, accs, ms, m_ctx, bq, bk, n_blocks, finish):
    pairs = list(zip(qts, k_refs, vt_refs, accs, ms))
    ctx_items = [p + (slice(0, m_ctx),) for p in pairs]
    split = max(1, 2 // len(pairs))
    sub = bk // split
    is_latent = pl.program_id(1) >= m_ctx // bq

    @pl.when(jnp.logical_not(is_latent))
    def _():
        _attend(ctx_items, True)
        finish()

    @pl.when(is_latent)
    def _():
        _attend(ctx_items, True)
        for j in range(n_blocks):
            _attend([p + (slice(m_ctx + j * bk + u * sub, m_ctx + j * bk + (u + 1) * sub),)
                     for p in pairs for u in range(split)], False)
        finish()


def _normalised(acc_sc):
    acc = acc_sc[...]
    return (acc[:ATT_V] / jnp.sum(acc[ATT_V:], axis=0, keepdims=True)).T


def _mla_attn_kernel(qt_ref, k_ref, vt_ref, o_ref, a0, a1, *m_sc, m_ctx, bq, bk, n_blocks):
    ms = list(m_sc) if m_sc else [None, None]

    def finish():
        o_ref[:, 0:MLA_V] = _normalised(a0)
        o_ref[:, MLA_V:] = _normalised(a1)

    _attend_all([qt_ref[0], qt_ref[1]], [k_ref.at[0], k_ref.at[1]], [vt_ref.at[0], vt_ref.at[1]], [a0, a1], ms,
                m_ctx, bq, bk, n_blocks, finish)


def _attn_blocks(t, m_ctx):
    bq = ATT_BQ
    assert m_ctx % bq == 0 and t % bq == 0
    n_lat = t - m_ctx
    bk = _row_block(n_lat, ATT_BK)
    return bq, bk, n_lat // bk


def _attn_scratch(bq, n_pairs, shifted):
    return [pltpu.VMEM((ACC_ROWS, bq), F32)] * n_pairs + ([pltpu.VMEM((1, bq), F32)] * n_pairs if shifted else [])


def _by_score_bound(bound, attend, *operands):
    return lax.cond(bound <= SCORE_LIMIT,
                    functools.partial(attend, shifted=False),
                    functools.partial(attend, shifted=True), *operands)


def _mla_attn(qt, k, vt, m_ctx, shifted):
    t = k.shape[1]
    bq, bk, n_blocks = _attn_blocks(t, m_ctx)
    return pl.pallas_call(
        functools.partial(_mla_attn_kernel, m_ctx=m_ctx, bq=bq, bk=bk, n_blocks=n_blocks),
        grid=(HEADS // 2, t // bq),
        in_specs=[pl.BlockSpec((2, MLA_QK, bq), lambda h, i: (h, 0, i)),
                  pl.BlockSpec((2, t, MLA_QK), lambda h, i: (h, 0, 0), pipeline_mode=pl.Buffered(1)),
                  pl.BlockSpec((2, ATT_V, t), lambda h, i: (h, 0, 0), pipeline_mode=pl.Buffered(1))],
        out_specs=pl.BlockSpec((bq, 2 * MLA_V), lambda h, i: (i, h)),
        out_shape=jax.ShapeDtypeStruct((t, HEADS * MLA_V), F32),
        scratch_shapes=_attn_scratch(bq, 2, shifted),
        compiler_params=_cparams(("parallel", "arbitrary")),
        name="mla_attn_shifted" if shifted else "mla_attn",
    )(qt, k, vt)


def _diff_attn_kernel(q1_ref, q2_ref, k_ref, v_ref, lam_ref, sub_ref, o_ref, a1, a2, *m_sc,
                      m_ctx, bq, bk, n_blocks, out_scale):
    ms = list(m_sc) if m_sc else [None, None]

    def finish():
        d = _normalised(a1) - lam_ref[...] * _normalised(a2)
        y = d * lax.rsqrt(jnp.mean(d * d, axis=-1, keepdims=True) + EPS)
        o_ref[...] = y * sub_ref[...] * out_scale

    _attend_all([q1_ref[0], q2_ref[0]], [k_ref.at[0]] * 2, [v_ref.at[0]] * 2, [a1, a2], ms,
                m_ctx, bq, bk, n_blocks, finish)


def _diff_attn(dqt, dk, dvt, lam, subln, m_ctx, out_scale, shifted):
    t = dk.shape[1]
    bq, bk, n_blocks = _attn_blocks(t, m_ctx)
    qspec = lambda c: pl.BlockSpec((1, 2 * DA_DIM, bq), lambda h, i: (2 * h + c, 0, i))
    vec = pl.BlockSpec((1, DA_V), lambda h, i: (0, 0))
    return pl.pallas_call(
        functools.partial(_diff_attn_kernel, m_ctx=m_ctx, bq=bq, bk=bk, n_blocks=n_blocks,
                          out_scale=out_scale),
        grid=(HEADS, t // bq),
        in_specs=[qspec(0), qspec(1), pl.BlockSpec((1, t, 2 * DA_DIM), lambda h, i: (h, 0, 0)),
                  pl.BlockSpec((1, ATT_V, t), lambda h, i: (h, 0, 0)), vec, vec],
        out_specs=pl.BlockSpec((bq, DA_V), lambda h, i: (i, h)),
        out_shape=jax.ShapeDtypeStruct((t, HEADS * DA_V), F32),
        scratch_shapes=_attn_scratch(bq, 2, shifted),
        compiler_params=_cparams(("parallel", "arbitrary")),
        name="diff_attn_shifted" if shifted else "diff_attn",
    )(dqt, dqt, dk, dvt, lam, subln)


ODD_ATT_W = 3 * HEADS * 2 * DA_DIM + 2 * HEADS * RT_K


def _odd_prep(p, cos, sa, sb, gq_ref, gk_ref, dqt_ref, dk_ref, dvt_ref, rq_ref, rk_ref):
    da_w = HEADS * 2 * DA_DIM

    def sub_rms(x, g):
        low, ss_lo, ss_hi = _half_sums(x * x)
        inv = jnp.where(low, lax.rsqrt(ss_lo * (1.0 / DA_DIM) + EPS), lax.rsqrt(ss_hi * (1.0 / DA_DIM) + EPS))
        return x * inv * g

    for h in range(HEADS):
        col = h * LANES
        yqt = (_rope128(sub_rms(p[:, col:col + LANES], gq_ref[...]), cos, sa, sb) * (DA_DIM ** -0.5)).T
        zeros = jnp.zeros((DA_DIM, yqt.shape[1]), BF16)
        dqt_ref[2 * h, 0:DA_DIM] = yqt[:DA_DIM].astype(BF16)
        dqt_ref[2 * h, DA_DIM:] = zeros
        dqt_ref[2 * h + 1, 0:DA_DIM] = zeros
        dqt_ref[2 * h + 1, DA_DIM:] = yqt[DA_DIM:].astype(BF16)
        yk = _rope128(sub_rms(p[:, da_w + col:da_w + col + LANES], gk_ref[...]), cos, sa, sb)
        dk_ref[h] = yk.astype(BF16)
        dvt_ref[h] = p[:, 2 * da_w + col:2 * da_w + col + LANES].T.astype(BF16)
    r0 = 2 * da_w + HEADS * DA_V
    for b in range(HEADS // 2):
        col = r0 + b * LANES
        rq_ref[:, b * LANES:(b + 1) * LANES] = _rope128(p[:, col:col + LANES], cos, sa, sb)
        col = r0 + HEADS * RT_K + b * LANES
        rk_ref[:, b * LANES:(b + 1) * LANES] = _rope128(p[:, col:col + LANES] * (RT_K ** -0.5), cos, sa, sb)


def _odd_front_kernel(x_ref, mod_ref, nw_ref, wa_ref, wr_ref, cos_ref, sa_ref, sb_ref, gq_ref, gk_ref,
                      dqt_ref, dk_ref, dvt_ref, rq_ref, rk_ref, rest_ref, *, m_ctx, rows):
    h = _mixer_input(x_ref, mod_ref, nw_ref, m_ctx, rows)
    p_att = _dot(h, wa_ref[...])
    rest_ref[...] = _dot(h, wr_ref[...])
    _odd_prep(p_att, cos_ref[...], sa_ref[...], sb_ref[...], gq_ref, gk_ref,
              dqt_ref, dk_ref, dvt_ref, rq_ref, rk_ref)


def _odd_front(xt, mod, nw, w_in, tables, gq, gk, m_ctx):
    t, d = xt.shape
    rows = _row_block(t, ROW_BLOCK)
    w_att, w_rest = w_in[:, :ODD_ATT_W], w_in[:, ODD_ATT_W:]
    rowb = lambda w: pl.BlockSpec((rows, w), lambda i: (i, 0))
    hm = lambda n, w: pl.BlockSpec((n, rows, w), lambda i: (0, i, 0))
    return pl.pallas_call(
        functools.partial(_odd_front_kernel, m_ctx=m_ctx, rows=rows),
        grid=(t // rows,),
        in_specs=[rowb(d), _resident(mod), _resident(nw), _resident(w_att), _resident(w_rest)]
                 + [rowb(LANES)] * 3 + [_resident(gq), _resident(gk)],
        out_specs=[pl.BlockSpec((2 * HEADS, 2 * DA_DIM, rows), lambda i: (0, 0, i)),
                   hm(HEADS, 2 * DA_DIM),
                   pl.BlockSpec((HEADS, ATT_V, rows), lambda i: (0, 0, i)), rowb(HEADS * RT_K), rowb(HEADS * RT_K),
                   rowb(w_rest.shape[1])],
        out_shape=[jax.ShapeDtypeStruct((2 * HEADS, 2 * DA_DIM, t), BF16),
                   jax.ShapeDtypeStruct((HEADS, t, 2 * DA_DIM), BF16),
                   jax.ShapeDtypeStruct((HEADS, ATT_V, t), BF16),
                   jax.ShapeDtypeStruct((t, HEADS * RT_K), F32),
                   jax.ShapeDtypeStruct((t, HEADS * RT_K), F32),
                   jax.ShapeDtypeStruct((t, w_rest.shape[1]), F32)],
        compiler_params=_cparams(("parallel",)),
        name="odd_front",
    )(xt, mod, nw, w_att, w_rest, *tables, gq, gk)


def _back_kernel(x_ref, att_ref, of_ref, ob_ref, gate_ref, nrm_ref, wa_ref, wr_ref, mod_ref, nw_ref, w1_ref, w2_ref,
                 o_ref, *, m_ctx, rows):
    row0 = pl.program_id(0) * rows
    o = of_ref[...] + ob_ref[...]
    gate = gate_ref[...]
    rec = []
    for h in range(HEADS):
        oh = o[:, h * LANES:(h + 1) * LANES]
        inv = lax.rsqrt(jnp.mean(oh * oh, axis=-1, keepdims=True) + EPS)
        rec.append(oh * inv * nrm_ref[...] * _silu(gate[:, h * LANES:(h + 1) * LANES]))
    rec = jnp.concatenate(rec, axis=-1).astype(BF16)
    y = _dot(att_ref[...].astype(BF16), wa_ref[...]) + _dot(rec, wr_ref[...])
    x1 = x_ref[...] + _row_mod(mod_ref, 2, row0, rows, m_ctx) * y
    h = _modnorm(x1, nw_ref[...], _row_mod(mod_ref, 4, row0, rows, m_ctx),
                 _row_mod(mod_ref, 3, row0, rows, m_ctx)).astype(BF16)
    u = jnp.maximum(_dot(h, w1_ref[...]), 0.0)
    o_ref[...] = x1 + _row_mod(mod_ref, 5, row0, rows, m_ctx) * _dot((u * u).astype(BF16), w2_ref[...])


def _back(xt, att, o_f, o_b, gate_arr, gate_block, nrm, w_att, w_rec, mod, nw, w1, w2, m_ctx):
    t, d = xt.shape
    rows = _row_block(t, ROW_BLOCK)
    w = att.shape[1]
    rowb = lambda c: pl.BlockSpec((rows, c), lambda i: (i, 0))
    return pl.pallas_call(
        functools.partial(_back_kernel, m_ctx=m_ctx, rows=rows),
        grid=(t // rows,),
        in_specs=[rowb(d), rowb(w), rowb(w), rowb(w),
                  pl.BlockSpec((rows, w), lambda i: (i, gate_block))]
                 + [_resident(a) for a in (nrm, w_att, w_rec, mod, nw, w1, w2)],
        out_specs=rowb(d),
        out_shape=jax.ShapeDtypeStruct((t, d), F32),
        compiler_params=_cparams(("parallel",)),
        name="layer_back",
    )(xt, att, o_f, o_b, gate_arr, nrm, w_att, w_rec, mod, nw, w1, w2)


def _head_major(w, parts):
    k = w.shape[0]
    wh = w.reshape(k, HEADS, sum(parts))
    out, off = [], 0
    for width in parts:
        out.append(wh[:, :, off:off + width].reshape(k, HEADS * width))
        off += width
    return jnp.concatenate(out, axis=1)


def kernel(x, c, ctx, c_ctx, ada_w, ada_b, norm_w, w_o, mlp_w1, mlp_w2, a_w_in, hg_lb, hg_norm, mla_q_norm,
           mla_kv_norm, mla_w_uq, mla_w_ukv, mla_qk_q, mla_qk_k, c_w_in, da_lambda, da_qk_q, da_qk_k,
           da_subln, rt_decay, rt_norm):
    assert x.shape[0] == 1 and ctx.shape[0] == 1
    n, d = x.shape[1], x.shape[2]
    m_ctx = ctx.shape[1]
    depth = ada_w.shape[0]
    xt = jnp.concatenate([ctx[0], x[0]], axis=0)

    cvec = jnp.zeros((8, d), F32).at[0].set(c_ctx).at[1].set(c[0])
    mods = _ada_table(cvec, ada_w, ada_b)[:, :2].reshape(depth, 2, 6, d)
    tables = _rope_tables(n, m_ctx)

    lb = jnp.cumsum(jax.nn.softmax(hg_lb.astype(F32), axis=0), axis=0)
    lb = lb - lb[:1]
    log_lb = jnp.log(lb).reshape(-1, 2, 1, HG_W)
    log1m_lb = jnp.log1p(-lb).reshape(-1, 2, 1, HG_W)

    for l in range(depth):
        j = l // 2
        mod = mods[l]
        nw = norm_w[l]
        wo = w_o[l].astype(BF16)
        if l % 2 == 0:
            w_in = a_w_in[j]
            w_main = w_in[:, :5 * HG_W].astype(BF16)
            w_mla = jnp.pad(w_in[:, 5 * HG_W:], ((0, 0), (0, MLA_IN - (w_in.shape[1] - 5 * HG_W)))).astype(BF16)
            gq = mla_qk_q[j]
            gk = mla_qk_k[j]
            p_main, qt, k, vt = _even_front(
                xt, mod, nw[0:1], w_main, w_mla, tables, mla_q_norm[j][None], mla_kv_norm[j][None],
                _head_major(mla_w_uq[j], (MLA_NOPE, MLA_ROPE)).astype(BF16),
                _head_major(mla_w_ukv[j], (MLA_NOPE, MLA_V)).astype(BF16),
                jnp.stack([gq[:MLA_NOPE], jnp.tile(gq[MLA_NOPE:], 2)]),
                jnp.stack([gk[:MLA_NOPE], jnp.tile(gk[MLA_NOPE:], 2)]), m_ctx)
            o_f, o_b = _hgrn2(p_main, log_lb[j], log1m_lb[j], m_ctx)
            bound = 1.02 * MLA_QK ** 0.5 * jnp.max(jnp.abs(gq)) * jnp.max(jnp.abs(gk))
            att = _by_score_bound(bound, functools.partial(_mla_attn, m_ctx=m_ctx), qt, k, vt)
            back = (att, o_f, o_b, p_main, 4, hg_norm[j][None], wo[HG_W:], wo[:HG_W])
        else:
            dqt, dk, dvt, rq, rk, p = _odd_front(xt, mod, nw[0:1], c_w_in[j].astype(BF16), tables,
                                                 jnp.tile(da_qk_q[j], 2)[None], jnp.tile(da_qk_k[j], 2)[None], m_ctx)
            lam_init = 0.8 - 0.6 * math.exp(-0.3 * l)
            lf = da_lambda[j].astype(F32)
            lam = jnp.exp(jnp.sum(lf[0] * lf[1])) - jnp.exp(jnp.sum(lf[2] * lf[3])) + lam_init
            bound = 1.02 * DA_DIM ** 0.5 * jnp.max(jnp.abs(da_qk_q[j])) * jnp.max(jnp.abs(da_qk_k[j]))
            att = _by_score_bound(
                bound, functools.partial(_diff_attn, m_ctx=m_ctx, out_scale=1.0 - lam_init),
                dqt, dk, dvt, jnp.full((1, DA_V), lam, F32), da_subln[j][None])
            lg = jax.nn.log_sigmoid(rt_decay[j].astype(F32))
            lg = jnp.broadcast_to(lg[:, :, None, None], (2, HEADS, 8, LANES))
            r_f, r_b = _retention(rq, rk, p, 0, lg, m_ctx)
            back = (att, r_f, r_b, p, 1, rt_norm[j][None], wo[:HG_W], wo[HG_W:])
        xt = _back(xt, *back, mod, nw[1:2], mlp_w1[l].astype(BF16), mlp_w2[l].astype(BF16), m_ctx)
    return xt[m_ctx:][None]
```

```python
import functools
import math

import numpy as np
import jax
import jax.numpy as jnp
from jax import lax
from jax.experimental import pallas as pl
from jax.experimental.pallas import tpu as pltpu

F32 = jnp.float32
BF16 = jnp.bfloat16

GRID_W = 64
ROPE_DIM = 64
ROPE_BASE = 10000.0
EPS = 1e-6
HEADS = 4
HG_DIM = 128
HG_W = HEADS * HG_DIM
MLA_NOPE = 128
MLA_ROPE = ROPE_DIM
MLA_V = 128
MLA_QK = MLA_NOPE + MLA_ROPE
MLA_Q_RANK = 384
MLA_KV_RANK = 256
DA_DIM = ROPE_DIM
DA_V = 2 * DA_DIM
RT_K = ROPE_DIM
RT_V = 128
MLA_IN = 768

LANES = 128
VMEM_LIMIT = 56 * 1024 * 1024
ROW_BLOCK = 640
ATT_BQ = 256
ATT_BK = 4096
SCAN_CHUNK = 64
SCAN_SUB = 4
RET_CHUNK = 128
RET_SUB = 2
SUBLANES = 8
ATT_V = 128
ACC_ROWS = ATT_V + SUBLANES
SCORE_LIMIT = 40.0


def _cparams(sem):
    return pltpu.CompilerParams(dimension_semantics=sem, vmem_limit_bytes=VMEM_LIMIT)


def _row_block(t, target):
    best = None
    for r in range(LANES, min(t, target) + 1, LANES):
        if t % r == 0:
            best = r
    assert best is not None, t
    return best


def _dot(a, b):
    return jnp.dot(a, b, preferred_element_type=F32)


def _dot_nt(a, b):
    return lax.dot_general(a, b, (((1,), (1,)), ((), ())), preferred_element_type=F32)


def _dot_tn(a, b):
    return lax.dot_general(a, b, (((0,), (0,)), ((), ())), preferred_element_type=F32)


def _silu(x):
    return x * (1.0 / (1.0 + jnp.exp(-x)))


def _row_mod(mod_ref, k, row0, rows, m_ctx):
    r = row0 + lax.broadcasted_iota(jnp.int32, (rows, 1), 0)
    return jnp.where(r < m_ctx, mod_ref[0, k:k + 1, :], mod_ref[1, k:k + 1, :])


def _modnorm(x, nw, sc, sh):
    y = x * lax.rsqrt(jnp.mean(x * x, axis=-1, keepdims=True) + EPS)
    return y * nw * (1.0 + sc) + sh


def _ada_kernel(c_ref, w_ref, b_ref, o_ref):
    cv = c_ref[...]
    o_ref[0] = jnp.dot(_silu(cv), w_ref[0], precision=lax.Precision.HIGHEST,
                       preferred_element_type=F32) + b_ref[0]


def _ada_table(cvec, ada_w, ada_b):
    depth, d, d6 = ada_w.shape
    tn = d6 // 4
    return pl.pallas_call(
        _ada_kernel,
        grid=(depth, d6 // tn),
        in_specs=[pl.BlockSpec((8, d), lambda l, j: (0, 0)),
                  pl.BlockSpec((1, d, tn), lambda l, j: (l, 0, j)),
                  pl.BlockSpec((1, 1, tn), lambda l, j: (l, 0, j))],
        out_specs=pl.BlockSpec((1, 8, tn), lambda l, j: (l, 0, j)),
        out_shape=jax.ShapeDtypeStruct((depth, 8, d6), F32),
        compiler_params=_cparams(("parallel", "parallel")),
        name="ada_table",
    )(cvec, ada_w, ada_b.reshape(depth, 1, d6))


def _mixer_input(x_ref, mod_ref, nw_ref, m_ctx, rows):
    row0 = pl.program_id(0) * rows
    return _modnorm(x_ref[...], nw_ref[...],
                    _row_mod(mod_ref, 1, row0, rows, m_ctx),
                    _row_mod(mod_ref, 0, row0, rows, m_ctx)).astype(BF16)


def _resident(a):
    return pl.BlockSpec(a.shape, lambda i: (0,) * a.ndim, pipeline_mode=pl.Buffered(1))


def _hier_tables(c, reverse):
    levels = int(math.log2(c))
    assert 1 << levels == c
    cums = np.zeros(((levels + 1) * c, c), np.float32)
    roles = np.zeros((levels, c, LANES), np.float32)
    masks = np.zeros((levels + 1, c, c), np.float32)
    for li in range(levels):
        h = c >> (li + 1)
        for t in range(c):
            base = (t // (2 * h)) * 2 * h
            late = (t - base) >= h
            if not reverse:
                if late:
                    cums[li * c + t, base + h:t + 1] = 1.0
                else:
                    cums[li * c + t, t + 1:base + h] = 1.0
            else:
                if late:
                    cums[li * c + t, base + h:t] = 1.0
                else:
                    cums[li * c + t, t:base + h] = 1.0
            is_query = late != reverse
            roles[li, t, :] = 1.0 if is_query else 0.0
        for t in range(c):
            for s in range(c):
                same = (t // (2 * h)) == (s // (2 * h))
                if same and roles[li, t, 0] == 1.0 and roles[li, s, 0] == 0.0:
                    masks[li, t, s] = 1.0
    for t in range(c):
        if not reverse:
            cums[levels * c + t, :t + 1] = 1.0
        else:
            cums[levels * c + t, t:] = 1.0
    masks[levels] = np.eye(c, dtype=np.float32)
    return cums, roles, masks


def _split3(x):
    hi = x.astype(BF16)
    r1 = x - hi.astype(F32)
    mid = r1.astype(BF16)
    lo = (r1 - mid.astype(F32)).astype(BF16)
    return hi, mid, lo


def _gated_chunks(sides, c):
    levels = sides[0][5].shape[0]
    chains = []
    for q_all, kk_all, v_all, logf_all, cum_ref, role_ref, mask_ref, st_ref, o_ref, reverse in sides:
        x_all = _dot(cum_ref[...], jnp.concatenate(_split3(logf_all), axis=0))
        for h in range(HEADS):
            sl = slice(h * HG_DIM, (h + 1) * HG_DIM)
            chains.append((x_all[:, sl], q_all[:, sl], kk_all[:, sl], v_all[:, sl].astype(BF16),
                           role_ref, mask_ref, st_ref, o_ref, reverse, h, sl))
    acc = [_dot_nt(q.astype(BF16), kk.astype(BF16)) * mask_ref[levels]
           for _, q, kk, _, _, mask_ref, *_ in chains]
    for li in range(levels):
        for n, (x, q, kk, _, role_ref, mask_ref, *_) in enumerate(chains):
            z = (jnp.where(role_ref[li] > 0.5, q, kk) * jnp.exp(x[li * c:(li + 1) * c])).astype(BF16)
            acc[n] = acc[n] + _dot_nt(z, z) * mask_ref[li]
    for a, (x, q, kk, vb, _, _, st_ref, o_ref, reverse, h, sl) in zip(acc, chains):
        run = x[levels * c:(levels + 1) * c]
        tot = run[0:1] if reverse else run[c - 1:c]
        st = st_ref[h]
        o_ref[:, sl] = _dot(a.astype(BF16), vb) + _dot_nt((q * jnp.exp(run)).astype(BF16), st.astype(BF16))
        kd = (kk * jnp.exp(tot - run)).astype(BF16)
        st_ref[h] = st * jnp.exp(tot) + _dot_tn(vb, kd)


def _hgrn2_gate(z, log_lb, log1m_lb):
    soft = jnp.log(1.0 + jnp.exp(-jnp.abs(z)))
    b = log1m_lb + (jnp.minimum(z, 0.0) - soft)
    logf = jnp.maximum(log_lb, b) + jnp.log(1.0 + jnp.exp(-jnp.abs(log_lb - b)))
    return logf, jnp.exp(log1m_lb + (jnp.minimum(-z, 0.0) - soft))


def _hgrn2_kernel(qf_ref, zf_ref, vf_ref, qb_ref, zb_ref, vb_ref, llb_ref, l1m_ref,
                  cumf_ref, rolef_ref, maskf_ref, cumb_ref, roleb_ref, maskb_ref,
                  of_ref, ob_ref, sf_ref, sb_ref, *, c, sub):
    @pl.when(pl.program_id(0) == 0)
    def _():
        sf_ref[...] = jnp.zeros_like(sf_ref)
        sb_ref[...] = jnp.zeros_like(sb_ref)

    sides = []
    for u in range(sub):
        rf, rb = pl.ds(u * c, c), pl.ds((sub - 1 - u) * c, c)
        logf, kk = _hgrn2_gate(zf_ref[rf, :], llb_ref[0], l1m_ref[0])
        sides.append((_silu(qf_ref[rf, :]), kk, vf_ref[rf, :], logf, cumf_ref, rolef_ref, maskf_ref, sf_ref,
                      of_ref.at[rf], False))
        logf, kk = _hgrn2_gate(zb_ref[rb, :], llb_ref[1], l1m_ref[1])
        sides.append((_silu(qb_ref[rb, :]), kk, vb_ref[rb, :], logf, cumb_ref, roleb_ref, maskb_ref, sb_ref,
                      ob_ref.at[rb], True))
    for side in sides:
        _gated_chunks([side], c)


def _bwd_chunk(i, mc, steps):
    return jnp.where(i < mc, mc - 1 - i, steps - 1 - i + mc)


def _hgrn2(p_main, log_lb, log1m_lb, m_ctx):
    t = p_main.shape[0]
    c, sub = SCAN_CHUNK, SCAN_SUB
    rows = c * sub
    steps, mc = t // rows, m_ctx // rows
    assert steps * rows == t and mc * rows == m_ctx
    tabs = [jnp.asarray(a) for rev in (False, True) for a in _hier_tables(c, rev)]
    for k in (0, 3):
        tabs[k] = jnp.tile(tabs[k], (1, 3)).astype(BF16)
    fwd = lambda sec: pl.BlockSpec((rows, HG_W), lambda i: (i, sec))
    bwd = lambda sec: pl.BlockSpec((rows, HG_W), lambda i: (_bwd_chunk(i, mc, steps), sec))
    full = lambda a: pl.BlockSpec(a.shape, lambda i: (0,) * a.ndim)
    return pl.pallas_call(
        functools.partial(_hgrn2_kernel, c=c, sub=sub),
        grid=(steps,),
        in_specs=[fwd(0), fwd(1), fwd(3), bwd(0), bwd(2), bwd(3), full(log_lb), full(log1m_lb)]
                 + [full(a) for a in tabs],
        out_specs=[pl.BlockSpec((rows, HG_W), lambda i: (i, 0)),
                   pl.BlockSpec((rows, HG_W), lambda i: (_bwd_chunk(i, mc, steps), 0))],
        out_shape=[jax.ShapeDtypeStruct((t, HG_W), F32)] * 2,
        scratch_shapes=[pltpu.VMEM((HEADS, HG_DIM, HG_DIM), F32)] * 2,
        compiler_params=_cparams(("arbitrary",)),
        name="hgrn2_scan",
    )(p_main, p_main, p_main, p_main, p_main, p_main, log_lb, log1m_lb, *tabs)


def _ret_head(x, h):
    low = lax.broadcasted_iota(jnp.int32, x.shape, 1) < RT_K
    return jnp.where(low if h % 2 == 0 else jnp.logical_not(low), x, 0.0)


def _ret_scores(q, k, lg, c, reverse):
    t = lax.broadcasted_iota(jnp.int32, (c, c), 0)
    s = lax.broadcasted_iota(jnp.int32, (c, c), 1)
    dlt = (s - t) if reverse else (t - s)
    dec = jnp.where(dlt >= 0, jnp.exp(lg[:, :c] * jnp.maximum(dlt, 0).astype(F32)), 0.0)
    return (_dot_nt(q.astype(BF16), k.astype(BF16)) * dec).astype(BF16)


def _ret_finish(a, q, k, v, lg, st_ref, c, reverse):
    r = lax.broadcasted_iota(jnp.int32, (c, LANES), 0).astype(F32)
    lk = lg
    qdec = jnp.exp(lk * ((c - r) if reverse else (r + 1.0)))
    kdec = jnp.exp(lk * (r if reverse else (c - 1.0 - r)))
    st = st_ref[...]
    vb = v.astype(BF16)
    o = _dot(a, vb) + _dot_nt((q * qdec).astype(BF16), st.astype(BF16))
    st_ref[...] = st * jnp.exp(lk * float(c)) + _dot_tn(vb, (k * kdec).astype(BF16))
    return o


def _ret_kernel(qf_ref, kf_ref, vf_ref, qb_ref, kb_ref, vb_ref, lg_ref,
                of_ref, ob_ref, sf_ref, sb_ref, *, c, sub):
    @pl.when(pl.program_id(0) == 0)
    def _():
        sf_ref[...] = jnp.zeros_like(sf_ref)
        sb_ref[...] = jnp.zeros_like(sb_ref)

    sides = ((qf_ref, kf_ref, vf_ref, of_ref, sf_ref), (qb_ref, kb_ref, vb_ref, ob_ref, sb_ref))
    chains = [(sides[d], d, h, pl.ds((sub - 1 - u if d else u) * c, c))
              for u in range(sub) for h in range(HEADS) for d in range(2)]
    blk = lambda h: slice(h // 2 * LANES, (h // 2 + 1) * LANES)
    scores = [_ret_scores(_ret_head(q_ref[rows, blk(h)], h), k_ref[rows, blk(h)], lg_ref[d, h, 0:1, :], c, d == 1)
              for (q_ref, k_ref, _, _, _), d, h, rows in chains]
    for a, ((q_ref, k_ref, v_ref, o_ref, s_ref), d, h, rows) in zip(scores, chains):
        sl = slice(h * RT_V, (h + 1) * RT_V)
        o_ref[rows, sl] = _ret_finish(a, _ret_head(q_ref[rows, blk(h)], h), k_ref[rows, blk(h)], v_ref[rows, sl],
                                      lg_ref[d, h, 0:1, :], s_ref.at[h], c, d == 1)


def _retention(rq, rk, p, v_block, lg, m_ctx):
    t = p.shape[0]
    c, sub = RET_CHUNK, RET_SUB
    rows = c * sub
    steps, mc = t // rows, m_ctx // rows
    assert steps * rows == t and mc * rows == m_ctx
    width = HEADS * RT_V
    fq = pl.BlockSpec((rows, HEADS * RT_K), lambda i: (i, 0))
    bq = pl.BlockSpec((rows, HEADS * RT_K), lambda i: (_bwd_chunk(i, mc, steps), 0))
    fv = pl.BlockSpec((rows, width), lambda i: (i, v_block))
    bv = pl.BlockSpec((rows, width), lambda i: (_bwd_chunk(i, mc, steps), v_block))
    return pl.pallas_call(
        functools.partial(_ret_kernel, c=c, sub=sub),
        grid=(steps,),
        in_specs=[fq, fq, fv, bq, bq, bv, pl.BlockSpec(lg.shape, lambda i: (0, 0, 0, 0))],
        out_specs=[pl.BlockSpec((rows, width), lambda i: (i, 0)),
                   pl.BlockSpec((rows, width), lambda i: (_bwd_chunk(i, mc, steps), 0))],
        out_shape=[jax.ShapeDtypeStruct((t, width), F32)] * 2,
        scratch_shapes=[pltpu.VMEM((HEADS, RT_V, LANES), F32)] * 2,
        compiler_params=_cparams(("arbitrary",)),
        name="retention_scan",
    )(rq, rk, p, rq, rk, p, lg)


def _rope128(y, cos, sin_a, sin_b):
    return y * cos + pltpu.roll(y, LANES - 16, 1) * sin_a + pltpu.roll(y, 16, 1) * sin_b


def _rope_tables(n, m_ctx):
    rows = n // GRID_W
    row = jnp.repeat(jnp.arange(rows, dtype=F32), GRID_W)
    col = jnp.tile(jnp.arange(GRID_W, dtype=F32), rows)
    quarter = ROPE_DIM // 4
    inv_freq = ROPE_BASE ** (-jnp.arange(quarter, dtype=F32) / quarter)
    ang_r = row[:, None] * inv_freq
    ang_c = col[:, None] * inv_freq
    ang = jnp.concatenate([ang_r, ang_r, ang_c, ang_c], axis=-1)
    cos, sin = jnp.cos(ang), jnp.sin(ang)
    first = (jnp.arange(ROPE_DIM) % 32) < 16
    sin_a = jnp.where(first, -sin, 0.0)
    sin_b = jnp.where(first, 0.0, sin)
    pad = lambda a, v: jnp.tile(jnp.concatenate([jnp.full((m_ctx, ROPE_DIM), v, F32), a], axis=0), (1, 2))
    return pad(cos, 1.0), pad(sin_a, 0.0), pad(sin_b, 0.0)


def _half_sums(sq):
    low = lax.broadcasted_iota(jnp.int32, sq.shape, 1) < 64
    lo = jnp.sum(jnp.where(low, sq, 0.0), axis=-1, keepdims=True)
    return low, lo, jnp.sum(sq, axis=-1, keepdims=True) - lo


def _mla_up(p, qn_ref, kvn_ref, wuq_ref, wukv_ref):
    def rms(x, w):
        return x * lax.rsqrt(jnp.mean(x * x, axis=-1, keepdims=True) + EPS) * w

    qu = _dot(rms(p[:, :MLA_Q_RANK], qn_ref[...]).astype(BF16), wuq_ref[...])
    kv = _dot(rms(p[:, MLA_Q_RANK:MLA_Q_RANK + MLA_KV_RANK], kvn_ref[...]).astype(BF16),
              wukv_ref[...])
    return qu, kv, p[:, MLA_Q_RANK + MLA_KV_RANK:]


def _mla_heads(qu, kv, kr, cos, sa, sb, gq_ref, gk_ref, qt_ref, k_ref, vt_ref):
    scale = MLA_QK ** -0.5
    ss_kr = jnp.sum(kr * kr, axis=-1, keepdims=True)
    gq, gk = gq_ref[...], gk_ref[...]
    for b in range(HEADS // 2):
        qr = qu[:, HEADS * MLA_NOPE + b * LANES:HEADS * MLA_NOPE + (b + 1) * LANES]
        low, ss_lo, ss_hi = _half_sums(qr * qr)
        inv = []
        for j in range(2):
            h = 2 * b + j
            qn = qu[:, h * MLA_NOPE:(h + 1) * MLA_NOPE]
            ss = jnp.sum(qn * qn, axis=-1, keepdims=True) + (ss_lo, ss_hi)[j]
            inv.append(lax.rsqrt(ss * (1.0 / MLA_QK) + EPS))
            qt_ref[h, 0:MLA_NOPE, :] = (qn * inv[j] * gq[0:1] * scale).T.astype(BF16)
        yrt = (_rope128(qr * jnp.where(low, inv[0], inv[1]) * gq[1:2], cos, sa, sb) * scale).T
        qt_ref[2 * b, MLA_NOPE:MLA_QK, :] = yrt[:MLA_ROPE].astype(BF16)
        qt_ref[2 * b + 1, MLA_NOPE:MLA_QK, :] = yrt[MLA_ROPE:].astype(BF16)
    low = lax.broadcasted_iota(jnp.int32, kr.shape, 1) < 64
    gk_rope = jnp.where(low, gk[1:2], 0.0)
    for h in range(HEADS):
        kn = kv[:, h * MLA_NOPE:(h + 1) * MLA_NOPE]
        inv = lax.rsqrt((jnp.sum(kn * kn, axis=-1, keepdims=True) + ss_kr) * (1.0 / MLA_QK) + EPS)
        k_ref[h, :, 0:MLA_NOPE] = (kn * inv * gk[0:1]).astype(BF16)
        yr = _rope128(kr * inv * gk_rope, cos, sa, sb)
        k_ref[h, :, MLA_NOPE:MLA_QK] = yr[:, :MLA_ROPE].astype(BF16)
        vt_ref[h] = kv[:, HEADS * MLA_NOPE + h * MLA_V:HEADS * MLA_NOPE + (h + 1) * MLA_V].T.astype(BF16)


def _even_front_kernel(x_ref, mod_ref, nw_ref, wm_ref, wa_ref, cos_ref, sa_ref, sb_ref, qn_ref, kvn_ref,
                       wuq_ref, wukv_ref, gq_ref, gk_ref, pm_ref, qt_ref, k_ref, vt_ref, *, m_ctx, rows):
    h = _mixer_input(x_ref, mod_ref, nw_ref, m_ctx, rows)
    qu, kv, kr = _mla_up(_dot(h, wa_ref[...]), qn_ref, kvn_ref, wuq_ref, wukv_ref)
    pm_ref[...] = _dot(h, wm_ref[...])
    _mla_heads(qu, kv, kr, cos_ref[...], sa_ref[...], sb_ref[...], gq_ref, gk_ref, qt_ref, k_ref, vt_ref)


def _even_front(xt, mod, nw, w_main, w_mla, tables, qn, kvn, wuq, wukv, gq, gk, m_ctx):
    t, d = xt.shape
    rows = _row_block(t, ROW_BLOCK)
    rowb = lambda w: pl.BlockSpec((rows, w), lambda i: (i, 0))
    consts = (w_main, w_mla), (qn, kvn, wuq, wukv, gq, gk)
    return pl.pallas_call(
        functools.partial(_even_front_kernel, m_ctx=m_ctx, rows=rows),
        grid=(t // rows,),
        in_specs=[rowb(d), _resident(mod), _resident(nw)] + [_resident(a) for a in consts[0]]
                 + [rowb(LANES)] * 3 + [_resident(a) for a in consts[1]],
        out_specs=[rowb(w_main.shape[1]),
                   pl.BlockSpec((HEADS, MLA_QK, rows), lambda i: (0, 0, i)),
                   pl.BlockSpec((HEADS, rows, MLA_QK), lambda i: (0, i, 0)),
                   pl.BlockSpec((HEADS, ATT_V, rows), lambda i: (0, 0, i))],
        out_shape=[jax.ShapeDtypeStruct((t, w_main.shape[1]), F32),
                   jax.ShapeDtypeStruct((HEADS, MLA_QK, t), BF16),
                   jax.ShapeDtypeStruct((HEADS, t, MLA_QK), BF16),
                   jax.ShapeDtypeStruct((HEADS, ATT_V, t), BF16)],
        compiler_params=_cparams(("parallel",)),
        name="even_front",
    )(xt, mod, nw, w_main, w_mla, *tables, qn, kvn, wuq, wukv, gq, gk)


def _pv_and_sums(vt, p):
    sums = p.reshape(p.shape[0] // SUBLANES, SUBLANES, p.shape[1]).sum(axis=0)
    return jnp.concatenate([_dot(vt, p.astype(BF16)), sums], axis=0)


def _attend(items, first):
    scores = [_dot(k_ref[rows, :], qt) for qt, k_ref, _, _, _, rows in items]
    for (_, _, vt_ref, acc_sc, m_sc, rows), s in zip(items, scores):
        vt = vt_ref[:, rows]
        if m_sc is None:
            pv = _pv_and_sums(vt, jnp.exp(s))
            acc_sc[...] = pv if first else acc_sc[...] + pv
        else:
            m_new = jnp.max(s, axis=0, keepdims=True)
            if not first:
                m_prev = m_sc[...]
                m_new = jnp.maximum(m_prev, m_new)
            pv = _pv_and_sums(vt, jnp.exp(s - m_new))
            acc_sc[...] = pv if first else jnp.exp(m_prev - m_new) * acc_sc[...] + pv
            m_sc[...] = m_new


def _attend_all(qts, k_refs, vt_refs, accs, ms, m_ctx, bq, bk, n_blocks, finish):
    pairs = list(zip(qts, k_refs, vt_refs, accs, ms))
    ctx_items = [p + (slice(0, m_ctx),) for p in pairs]
    split = max(1, 2 // len(pairs))
    sub = bk // split
    is_latent = pl.program_id(1) >= m_ctx // bq

    @pl.when(jnp.logical_not(is_latent))
    def _():
        _attend(ctx_items, True)
        finish()

    @pl.when(is_latent)
    def _():
        _attend(ctx_items, True)
        for j in range(n_blocks):
            _attend([p + (slice(m_ctx + j * bk + u * sub, m_ctx + j * bk + (u + 1) * sub),)
                     for p in pairs for u in range(split)], False)
        finish()


def _normalised(acc_sc):
    acc = acc_sc[...]
    return (acc[:ATT_V] / jnp.sum(acc[ATT_V:], axis=0, keepdims=True)).T


def _mla_attn_kernel(qt_ref, k_ref, vt_ref, o_ref, a0, a1, *m_sc, m_ctx, bq, bk, n_blocks):
    ms = list(m_sc) if m_sc else [None, None]

    def finish():
        o_ref[:, 0:MLA_V] = _normalised(a0).astype(BF16)
        o_ref[:, MLA_V:] = _normalised(a1).astype(BF16)

    _attend_all([qt_ref[0], qt_ref[1]], [k_ref.at[0], k_ref.at[1]], [vt_ref.at[0], vt_ref.at[1]], [a0, a1], ms,
                m_ctx, bq, bk, n_blocks, finish)


def _attn_blocks(t, m_ctx):
    bq = ATT_BQ
    assert m_ctx % bq == 0 and t % bq == 0
    n_lat = t - m_ctx
    bk = _row_block(n_lat, ATT_BK)
    return bq, bk, n_lat // bk


def _attn_scratch(bq, n_pairs, shifted):
    return [pltpu.VMEM((ACC_ROWS, bq), F32)] * n_pairs + ([pltpu.VMEM((1, bq), F32)] * n_pairs if shifted else [])


def _by_score_bound(bound, attend, *operands):
    return lax.cond(bound <= SCORE_LIMIT,
                    functools.partial(attend, shifted=False),
                    functools.partial(attend, shifted=True), *operands)


def _mla_attn(qt, k, vt, m_ctx, shifted):
    t = k.shape[1]
    bq, bk, n_blocks = _attn_blocks(t, m_ctx)
    return pl.pallas_call(
        functools.partial(_mla_attn_kernel, m_ctx=m_ctx, bq=bq, bk=bk, n_blocks=n_blocks),
        grid=(HEADS // 2, t // bq),
        in_specs=[pl.BlockSpec((2, MLA_QK, bq), lambda h, i: (h, 0, i)),
                  pl.BlockSpec((2, t, MLA_QK), lambda h, i: (h, 0, 0), pipeline_mode=pl.Buffered(1)),
                  pl.BlockSpec((2, ATT_V, t), lambda h, i: (h, 0, 0), pipeline_mode=pl.Buffered(1))],
        out_specs=pl.BlockSpec((bq, 2 * MLA_V), lambda h, i: (i, h)),
        out_shape=jax.ShapeDtypeStruct((t, HEADS * MLA_V), BF16),
        scratch_shapes=_attn_scratch(bq, 2, shifted),
        compiler_params=_cparams(("parallel", "arbitrary")),
        name="mla_attn_shifted" if shifted else "mla_attn",
    )(qt, k, vt)


def _diff_attn_kernel(q1_ref, q2_ref, k_ref, v_ref, lam_ref, sub_ref, o_ref, a1, a2, *m_sc,
                      m_ctx, bq, bk, n_blocks, out_scale):
    ms = list(m_sc) if m_sc else [None, None]

    def finish():
        d = _normalised(a1) - lam_ref[...] * _normalised(a2)
        y = d * lax.rsqrt(jnp.mean(d * d, axis=-1, keepdims=True) + EPS)
        o_ref[...] = (y * sub_ref[...] * out_scale).astype(BF16)

    _attend_all([q1_ref[0], q2_ref[0]], [k_ref.at[0]] * 2, [v_ref.at[0]] * 2, [a1, a2], ms,
                m_ctx, bq, bk, n_blocks, finish)


def _diff_attn(dqt, dk, dvt, lam, subln, m_ctx, out_scale, shifted):
    t = dk.shape[1]
    bq, bk, n_blocks = _attn_blocks(t, m_ctx)
    qspec = lambda c: pl.BlockSpec((1, 2 * DA_DIM, bq), lambda h, i: (2 * h + c, 0, i))
    vec = pl.BlockSpec((1, DA_V), lambda h, i: (0, 0))
    return pl.pallas_call(
        functools.partial(_diff_attn_kernel, m_ctx=m_ctx, bq=bq, bk=bk, n_blocks=n_blocks,
                          out_scale=out_scale),
        grid=(HEADS, t // bq),
        in_specs=[qspec(0), qspec(1),
                  pl.BlockSpec((1, t, 2 * DA_DIM), lambda h, i: (h, 0, 0)),
                  pl.BlockSpec((1, ATT_V, t), lambda h, i: (h, 0, 0)), vec, vec],
        out_specs=pl.BlockSpec((bq, DA_V), lambda h, i: (i, h)),
        out_shape=jax.ShapeDtypeStruct((t, HEADS * DA_V), BF16),
        scratch_shapes=_attn_scratch(bq, 2, shifted),
        compiler_params=_cparams(("parallel", "arbitrary")),
        name="diff_attn_shifted" if shifted else "diff_attn",
    )(dqt, dqt, dk, dvt, lam, subln)


ODD_ATT_W = 3 * HEADS * 2 * DA_DIM + 2 * HEADS * RT_K


def _odd_prep(p, cos, sa, sb, gq_ref, gk_ref, dqt_ref, dk_ref, dvt_ref, rq_ref, rk_ref):
    da_w = HEADS * 2 * DA_DIM

    def sub_rms(x, g):
        low, ss_lo, ss_hi = _half_sums(x * x)
        inv = jnp.where(low, lax.rsqrt(ss_lo * (1.0 / DA_DIM) + EPS), lax.rsqrt(ss_hi * (1.0 / DA_DIM) + EPS))
        return x * inv * g

    for h in range(HEADS):
        col = h * LANES
        yqt = (_rope128(sub_rms(p[:, col:col + LANES], gq_ref[...]), cos, sa, sb) * (DA_DIM ** -0.5)).T
        zeros = jnp.zeros((DA_DIM, yqt.shape[1]), BF16)
        dqt_ref[2 * h, 0:DA_DIM] = yqt[:DA_DIM].astype(BF16)
        dqt_ref[2 * h, DA_DIM:] = zeros
        dqt_ref[2 * h + 1, 0:DA_DIM] = zeros
        dqt_ref[2 * h + 1, DA_DIM:] = yqt[DA_DIM:].astype(BF16)
        yk = _rope128(sub_rms(p[:, da_w + col:da_w + col + LANES], gk_ref[...]), cos, sa, sb)
        dk_ref[h] = yk.astype(BF16)
        dvt_ref[h] = p[:, 2 * da_w + col:2 * da_w + col + LANES].T.astype(BF16)
    r0 = 2 * da_w + HEADS * DA_V
    for b in range(HEADS // 2):
        col = r0 + b * LANES
        rq_ref[:, b * LANES:(b + 1) * LANES] = _rope128(p[:, col:col + LANES], cos, sa, sb)
        col = r0 + HEADS * RT_K + b * LANES
        rk_ref[:, b * LANES:(b + 1) * LANES] = _rope128(p[:, col:col + LANES] * (RT_K ** -0.5), cos, sa, sb)


def _odd_front_kernel(x_ref, mod_ref, nw_ref, wa_ref, wr_ref, cos_ref, sa_ref, sb_ref, gq_ref, gk_ref,
                      dqt_ref, dk_ref, dvt_ref, rq_ref, rk_ref, rest_ref, *, m_ctx, rows):
    h = _mixer_input(x_ref, mod_ref, nw_ref, m_ctx, rows)
    p_att = _dot(h, wa_ref[...])
    rest_ref[...] = _dot(h, wr_ref[...])
    _odd_prep(p_att, cos_ref[...], sa_ref[...], sb_ref[...], gq_ref, gk_ref,
              dqt_ref, dk_ref, dvt_ref, rq_ref, rk_ref)


def _odd_front(xt, mod, nw, w_in, tables, gq, gk, m_ctx):
    t, d = xt.shape
    rows = _row_block(t, ROW_BLOCK)
    w_att, w_rest = w_in[:, :ODD_ATT_W], w_in[:, ODD_ATT_W:]
    rowb = lambda w: pl.BlockSpec((rows, w), lambda i: (i, 0))
    hm = lambda n, w: pl.BlockSpec((n, rows, w), lambda i: (0, i, 0))
    return pl.pallas_call(
        functools.partial(_odd_front_kernel, m_ctx=m_ctx, rows=rows),
        grid=(t // rows,),
        in_specs=[rowb(d), _resident(mod), _resident(nw), _resident(w_att), _resident(w_rest)]
                 + [rowb(LANES)] * 3 + [_resident(gq), _resident(gk)],
        out_specs=[pl.BlockSpec((2 * HEADS, 2 * DA_DIM, rows), lambda i: (0, 0, i)),
                   hm(HEADS, 2 * DA_DIM),
                   pl.BlockSpec((HEADS, ATT_V, rows), lambda i: (0, 0, i)), rowb(HEADS * RT_K), rowb(HEADS * RT_K),
                   rowb(w_rest.shape[1])],
        out_shape=[jax.ShapeDtypeStruct((2 * HEADS, 2 * DA_DIM, t), BF16),
                   jax.ShapeDtypeStruct((HEADS, t, 2 * DA_DIM), BF16),
                   jax.ShapeDtypeStruct((HEADS, ATT_V, t), BF16),
                   jax.ShapeDtypeStruct((t, HEADS * RT_K), F32),
                   jax.ShapeDtypeStruct((t, HEADS * RT_K), F32),
                   jax.ShapeDtypeStruct((t, w_rest.shape[1]), F32)],
        compiler_params=_cparams(("parallel",)),
        name="odd_front",
    )(xt, mod, nw, w_att, w_rest, *tables, gq, gk)


def _back_kernel(x_ref, att_ref, of_ref, ob_ref, gate_ref, nrm_ref, wa_ref, wr_ref, mod_ref, nw_ref, w1_ref, w2_ref,
                 o_ref, *, m_ctx, rows):
    row0 = pl.program_id(0) * rows
    o = of_ref[...] + ob_ref[...]
    gate = gate_ref[...]
    rec = []
    for h in range(HEADS):
        oh = o[:, h * LANES:(h + 1) * LANES]
        inv = lax.rsqrt(jnp.mean(oh * oh, axis=-1, keepdims=True) + EPS)
        rec.append(oh * inv * nrm_ref[...] * _silu(gate[:, h * LANES:(h + 1) * LANES]))
    rec = jnp.concatenate(rec, axis=-1).astype(BF16)
    y = _dot(att_ref[...], wa_ref[...]) + _dot(rec, wr_ref[...])
    x1 = x_ref[...] + _row_mod(mod_ref, 2, row0, rows, m_ctx) * y
    h = _modnorm(x1, nw_ref[...], _row_mod(mod_ref, 4, row0, rows, m_ctx),
                 _row_mod(mod_ref, 3, row0, rows, m_ctx)).astype(BF16)
    u = jnp.maximum(_dot(h, w1_ref[...]), 0.0)
    o_ref[...] = x1 + _row_mod(mod_ref, 5, row0, rows, m_ctx) * _dot((u * u).astype(BF16), w2_ref[...])


def _back(xt, att, o_f, o_b, gate_arr, gate_block, nrm, w_att, w_rec, mod, nw, w1, w2, m_ctx):
    t, d = xt.shape
    rows = _row_block(t, ROW_BLOCK)
    w = att.shape[1]
    rowb = lambda c: pl.BlockSpec((rows, c), lambda i: (i, 0))
    return pl.pallas_call(
        functools.partial(_back_kernel, m_ctx=m_ctx, rows=rows),
        grid=(t // rows,),
        in_specs=[rowb(d), rowb(w), rowb(w), rowb(w),
                  pl.BlockSpec((rows, w), lambda i: (i, gate_block))]
                 + [_resident(a) for a in (nrm, w_att, w_rec, mod, nw, w1, w2)],
        out_specs=rowb(d),
        out_shape=jax.ShapeDtypeStruct((t, d), F32),
        compiler_params=_cparams(("parallel",)),
        name="layer_back",
    )(xt, att, o_f, o_b, gate_arr, nrm, w_att, w_rec, mod, nw, w1, w2)


def _head_major(w, parts):
    k = w.shape[0]
    wh = w.reshape(k, HEADS, sum(parts))
    out, off = [], 0
    for width in parts:
        out.append(wh[:, :, off:off + width].reshape(k, HEADS * width))
        off += width
    return jnp.concatenate(out, axis=1)


def kernel(x, c, ctx, c_ctx, ada_w, ada_b, norm_w, w_o, mlp_w1, mlp_w2, a_w_in, hg_lb, hg_norm, mla_q_norm,
           mla_kv_norm, mla_w_uq, mla_w_ukv, mla_qk_q, mla_qk_k, c_w_in, da_lambda, da_qk_q, da_qk_k,
           da_subln, rt_decay, rt_norm):
    assert x.shape[0] == 1 and ctx.shape[0] == 1
    n, d = x.shape[1], x.shape[2]
    m_ctx = ctx.shape[1]
    depth = ada_w.shape[0]
    xt = jnp.concatenate([ctx[0], x[0]], axis=0)

    cvec = jnp.zeros((8, d), F32).at[0].set(c_ctx).at[1].set(c[0])
    mods = _ada_table(cvec, ada_w, ada_b)[:, :2].reshape(depth, 2, 6, d)
    tables = _rope_tables(n, m_ctx)

    lb = jnp.cumsum(jax.nn.softmax(hg_lb.astype(F32), axis=0), axis=0)
    lb = lb - lb[:1]
    log_lb = jnp.log(lb).reshape(-1, 2, 1, HG_W)
    log1m_lb = jnp.log1p(-lb).reshape(-1, 2, 1, HG_W)

    for l in range(depth):
        j = l // 2
        mod = mods[l]
        nw = norm_w[l]
        wo = w_o[l].astype(BF16)
        if l % 2 == 0:
            w_in = a_w_in[j]
            w_main = w_in[:, :5 * HG_W].astype(BF16)
            w_mla = jnp.pad(w_in[:, 5 * HG_W:], ((0, 0), (0, MLA_IN - (w_in.shape[1] - 5 * HG_W)))).astype(BF16)
            gq = mla_qk_q[j]
            gk = mla_qk_k[j]
            p_main, qt, k, vt = _even_front(
                xt, mod, nw[0:1], w_main, w_mla, tables, mla_q_norm[j][None], mla_kv_norm[j][None],
                _head_major(mla_w_uq[j], (MLA_NOPE, MLA_ROPE)).astype(BF16),
                _head_major(mla_w_ukv[j], (MLA_NOPE, MLA_V)).astype(BF16),
                jnp.stack([gq[:MLA_NOPE], jnp.tile(gq[MLA_NOPE:], 2)]),
                jnp.stack([gk[:MLA_NOPE], jnp.tile(gk[MLA_NOPE:], 2)]), m_ctx)
            o_f, o_b = _hgrn2(p_main, log_lb[j], log1m_lb[j], m_ctx)
            bound = 1.02 * MLA_QK ** 0.5 * jnp.max(jnp.abs(gq)) * jnp.max(jnp.abs(gk))
            att = _by_score_bound(bound, functools.partial(_mla_attn, m_ctx=m_ctx), qt, k, vt)
            back = (att, o_f, o_b, p_main, 4, hg_norm[j][None], wo[HG_W:], wo[:HG_W])
        else:
            dqt, dk, dvt, rq, rk, p = _odd_front(xt, mod, nw[0:1], c_w_in[j].astype(BF16), tables,
                                                 jnp.tile(da_qk_q[j], 2)[None], jnp.tile(da_qk_k[j], 2)[None], m_ctx)
            lam_init = 0.8 - 0.6 * math.exp(-0.3 * l)
            lf = da_lambda[j].astype(F32)
            lam = jnp.exp(jnp.sum(lf[0] * lf[1])) - jnp.exp(jnp.sum(lf[2] * lf[3])) + lam_init
            bound = 1.02 * DA_DIM ** 0.5 * jnp.max(jnp.abs(da_qk_q[j])) * jnp.max(jnp.abs(da_qk_k[j]))
            att = _by_score_bound(
                bound, functools.partial(_diff_attn, m_ctx=m_ctx, out_scale=1.0 - lam_init),
                dqt, dk, dvt, jnp.full((1, DA_V), lam, F32), da_subln[j][None])
            lg = jax.nn.log_sigmoid(rt_decay[j].astype(F32))
            lg = jnp.broadcast_to(lg[:, :, None, None], (2, HEADS, 8, LANES))
            r_f, r_b = _retention(rq, rk, p, 0, lg, m_ctx)
            back = (att, r_f, r_b, p, 1, rt_norm[j][None], wo[:HG_W], wo[HG_W:])
        xt = _back(xt, *back, mod, nw[1:2], mlp_w1[l].astype(BF16), mlp_w2[l].astype(BF16), m_ctx)
    return xt[m_ctx:][None]
```
